```python
import jax, jax.numpy as jnp
from jax import lax
import numpy as np

D_MODEL = 1024
BATCH = 8
SEQ = 2048
DEPTH = 1

HEAD_DIM = 64
NA_HEADS = 8
NA_WIDTH = NA_HEADS * HEAD_DIM
NA_KH = 8
NA_KW = 16
GRID_W = 64
SWA_Q_HEADS = 8
SWA_KV_HEADS = 2
SWA_WIDTH = SWA_Q_HEADS * HEAD_DIM
SWA_KV_WIDTH = SWA_KV_HEADS * HEAD_DIM
SWA_WINDOW = 128
SWA_BLOCK = 128
ROPE_THETA = 10000.0

MIX_WIDTH = NA_WIDTH + SWA_WIDTH
LN_EPS = 1e-5
MASK_VALUE = -1e30
SECTION_WIDTHS = (NA_WIDTH, NA_WIDTH, NA_WIDTH, NA_WIDTH, SWA_WIDTH, SWA_KV_WIDTH, SWA_KV_WIDTH, SWA_WIDTH)
IN_WIDTH = sum(SECTION_WIDTHS)
SPLIT_POINTS = tuple(sum(SECTION_WIDTHS[:i + 1]) for i in range(len(SECTION_WIDTHS) - 1))
DEEPNORM_ALPHA = (2.0 * DEPTH) ** 0.25
DEEPNORM_BETA = (8.0 * DEPTH) ** -0.25

kernel_name = "hymba_natten_swa_deepnorm_encoder"


def layer_norm(x, gain, bias):
    xf = x.astype(jnp.float32)
    mu = jnp.mean(xf, axis=-1, keepdims=True)
    var = jnp.mean(jnp.square(xf - mu), axis=-1, keepdims=True)
    y = (xf - mu) * lax.rsqrt(var + LN_EPS) * gain.astype(jnp.float32) + bias.astype(jnp.float32)
    return y.astype(x.dtype)


def rotary(t, positions):
    d = t.shape[-1]
    inv_freq = ROPE_THETA ** (-jnp.arange(0, d, 2, dtype=jnp.float32) / d)
    ang = positions.astype(jnp.float32)[:, None] * inv_freq[None, :]
    cos = jnp.cos(ang)[None, :, None, :]
    sin = jnp.sin(ang)[None, :, None, :]
    tf = t.astype(jnp.float32)
    t1, t2 = tf[..., : d // 2], tf[..., d // 2:]
    out = jnp.concatenate([t1 * cos - t2 * sin, t2 * cos + t1 * sin], axis=-1)
    return out.astype(t.dtype)


def neighborhood_attention(q, k, v, rel_bias):
    B, S, H, D = q.shape
    rows = S // GRID_W
    kh = min(NA_KH, rows)
    kw = NA_KW
    qg = q.reshape(B, rows, GRID_W, H, D)
    kg = k.reshape(B, rows, GRID_W, H, D)
    vg = v.reshape(B, rows, GRID_W, H, D)
    cols = jnp.arange(GRID_W)
    col_start = jnp.clip(cols - kw // 2, 0, GRID_W - kw)
    col_idx = col_start[:, None] + jnp.arange(kw)[None, :]
    dc_idx = col_idx - cols[:, None] + (NA_KW - 1)
    scale = D ** -0.5

    def row_block(r):
        r0 = jnp.clip(r - kh // 2, 0, rows - kh)
        q_r = lax.dynamic_index_in_dim(qg, r, axis=1, keepdims=False)
        k_slab = lax.dynamic_slice_in_dim(kg, r0, kh, axis=1)
        v_slab = lax.dynamic_slice_in_dim(vg, r0, kh, axis=1)
        k_win = k_slab[:, :, col_idx]
        v_win = v_slab[:, :, col_idx]
        s = jnp.einsum('bqhd,biqjhd->bhqij', q_r, k_win).astype(jnp.float32) * scale
        dr_idx = r0 + jnp.arange(kh) - r + (NA_KH - 1)
        bias = rel_bias[:, dr_idx[None, :, None], dc_idx[:, None, :]]
        s = s + bias.astype(jnp.float32)[None]
        p = jax.nn.softmax(s.reshape(B, H, GRID_W, kh * kw), axis=-1).reshape(B, H, GRID_W, kh, kw)
        return jnp.einsum('bhqij,biqjhd->bqhd', p.astype(v.dtype), v_win)

    out = lax.map(row_block, jnp.arange(rows))
    return jnp.transpose(out, (1, 0, 2, 3, 4)).reshape(B, S, H * D)


def windowed_gqa_with_sink(q, k, v, sink):
    B, S, HQ, D = q.shape
    HKV = k.shape[2]
    G = HQ // HKV
    wb = SWA_BLOCK
    nb = S // wb
    qb = q.reshape(B, nb, wb, HKV, G, D)
    pad = ((0, 0), (wb, wb), (0, 0), (0, 0))
    kp = jnp.pad(k, pad).reshape(B, nb + 2, wb, HKV, D)
    vp = jnp.pad(v, pad).reshape(B, nb + 2, wb, HKV, D)
    k_band = jnp.concatenate([kp[:, :-2], kp[:, 1:-1], kp[:, 2:]], axis=2)
    v_band = jnp.concatenate([vp[:, :-2], vp[:, 1:-1], vp[:, 2:]], axis=2)
    s = jnp.einsum('bnqkgd,bnjkd->bnkgqj', qb, k_band).astype(jnp.float32) * (D ** -0.5)
    q_off = jnp.arange(wb)[:, None]
    k_off = jnp.arange(3 * wb)[None, :] - wb
    k_abs = jnp.arange(nb)[:, None, None] * wb + k_off[None]
    valid = (jnp.abs(k_off - q_off) <= SWA_WINDOW)[None] & (k_abs >= 0) & (k_abs < S)
    s = jnp.where(valid[None, :, None, None], s, MASK_VALUE)
    sink_col = jnp.broadcast_to(sink.astype(jnp.float32).reshape(1, 1, HKV, G, 1, 1), s.shape[:-1] + (1,))
    p = jax.nn.softmax(jnp.concatenate([s, sink_col], axis=-1), axis=-1)[..., :-1]
    out = jnp.einsum('bnkgqj,bnjkd->bnqkgd', p.astype(v.dtype), v_band)
    return out.reshape(B, S, HQ * D)


def setup_inputs(seed: int = 0) -> dict:
    key = jax.random.key(seed)
    k_x, k_in, k_rpb, k_sink, k_out, k_g, k_b = jax.random.split(key, 7)
    x = jax.random.normal(k_x, (BATCH, SEQ, D_MODEL), jnp.float32)
    col_scale = jnp.concatenate([
        jnp.ones((2 * NA_WIDTH,), jnp.float32),
        jnp.full((NA_WIDTH,), DEEPNORM_BETA, jnp.float32),
        jnp.ones((NA_WIDTH + SWA_WIDTH + SWA_KV_WIDTH,), jnp.float32),
        jnp.full((SWA_KV_WIDTH,), DEEPNORM_BETA, jnp.float32),
        jnp.ones((SWA_WIDTH,), jnp.float32),
    ])
    w_in = jax.random.normal(k_in, (DEPTH, D_MODEL, IN_WIDTH), jnp.float32) * (D_MODEL ** -0.5) * col_scale
    rel_pos_bias = 0.02 * jax.random.normal(k_rpb, (DEPTH, NA_HEADS, 2 * NA_KH - 1, 2 * NA_KW - 1), jnp.float32)
    sink_logits = 0.5 * jax.random.normal(k_sink, (DEPTH, SWA_Q_HEADS), jnp.float32)
    w_out = jax.random.normal(k_out, (DEPTH, MIX_WIDTH, D_MODEL), jnp.float32) * (MIX_WIDTH ** -0.5) * DEEPNORM_BETA
    ln_gain = 1.0 + 0.02 * jax.random.normal(k_g, (DEPTH, D_MODEL), jnp.float32)
    ln_bias = 0.02 * jax.random.normal(k_b, (DEPTH, D_MODEL), jnp.float32)
    return {"x": x, "w_in": w_in, "rel_pos_bias": rel_pos_bias, "sink_logits": sink_logits,
            "w_out": w_out, "ln_gain": ln_gain, "ln_bias": ln_bias}


def reference(x, w_in, rel_pos_bias, sink_logits, w_out, ln_gain, ln_bias):
    B, S, _ = x.shape
    positions = jnp.arange(S, dtype=jnp.int32)
    for layer in range(DEPTH):
        h = jnp.einsum('bsd,de->bse', x, w_in[layer])
        q_a, k_a, v_a, z_a, q_b, k_b, v_b, z_b = jnp.split(h, SPLIT_POINTS, axis=-1)
        y_a = neighborhood_attention(q_a.reshape(B, S, NA_HEADS, HEAD_DIM),
                                     k_a.reshape(B, S, NA_HEADS, HEAD_DIM),
                                     v_a.reshape(B, S, NA_HEADS, HEAD_DIM),
                                     rel_pos_bias[layer])
        qb = rotary(q_b.reshape(B, S, SWA_Q_HEADS, HEAD_DIM), positions)
        kb = rotary(k_b.reshape(B, S, SWA_KV_HEADS, HEAD_DIM), positions)
        y_b = windowed_gqa_with_sink(qb, kb, v_b.reshape(B, S, SWA_KV_HEADS, HEAD_DIM), sink_logits[layer])
        mixed = jnp.concatenate([y_a * jax.nn.silu(z_a), y_b * jax.nn.silu(z_b)], axis=-1)
        y = jnp.einsum('bse,ed->bsd', mixed, w_out[layer])
        x = layer_norm(DEEPNORM_ALPHA * x + y, ln_gain[layer], ln_bias[layer])
    return x
```

```python
import functools

import numpy as np
import jax
import jax.numpy as jnp
from jax import lax
from jax.experimental import pallas as pl
from jax.experimental.pallas import tpu as pltpu

D_MODEL = 1024
HEAD_DIM = 64
NA_HEADS = 8
NA_WIDTH = NA_HEADS * HEAD_DIM
NA_KH = 8
NA_KW = 16
GRID_W = 64
SWA_Q_HEADS = 8
SWA_KV_HEADS = 2
SWA_GROUP = SWA_Q_HEADS // SWA_KV_HEADS
SWA_WIDTH = SWA_Q_HEADS * HEAD_DIM
SWA_KV_WIDTH = SWA_KV_HEADS * HEAD_DIM
SWA_WINDOW = 128
ROPE_THETA = 10000.0
MIX_WIDTH = NA_WIDTH + SWA_WIDTH
LN_EPS = 1e-5
MASK_VALUE = -1e30
DEPTH = 1
DEEPNORM_ALPHA = (2.0 * DEPTH) ** 0.25

LANES = 128
PAIR = 2 * HEAD_DIM
assert PAIR == LANES

PROJ_TM = 512
TQ = 128
Q_ROWS = TQ // GRID_W
NA_KEY_ROWS = 10
NA_KEYS = NA_KEY_ROWS * GRID_W
KEY_PAIRS = NA_KEYS // LANES
SWA_KEYS = 3 * TQ

_QA = (0, NA_WIDTH)
_VA = (_QA[1], _QA[1] + NA_WIDTH)
_QB = (_VA[1], _VA[1] + SWA_WIDTH)
_VB = (_QB[1], _QB[1] + 2 * SWA_KV_WIDTH)
_Z = (_VB[1], _VB[1] + MIX_WIDTH)
TOK_WIDTH = _Z[1]
KT_ROWS = NA_WIDTH + 2 * SWA_KV_WIDTH


def _na_tile_plan(rows):
    kh = min(NA_KH, rows)
    entries, lookup = [], {}
    n_steps = rows // Q_ROWS
    idx = np.zeros((n_steps, Q_ROWS * KEY_PAIRS), np.int32)
    for t in range(n_steps):
        ks = int(np.clip(Q_ROWS * t - kh // 2, 0, rows - NA_KEY_ROWS))
        for qrow in range(Q_ROWS):
            r = Q_ROWS * t + qrow
            r0 = int(np.clip(r - kh // 2, 0, rows - kh))
            assert ks <= r0 and r0 + kh <= ks + NA_KEY_ROWS
            for j in range(KEY_PAIRS):
                key = []
                for i in (2 * j, 2 * j + 1):
                    kr = ks + i
                    key.append(kr - r + (NA_KH - 1) if r0 <= kr < r0 + kh else -1)
                key = tuple(key)
                if key not in lookup:
                    lookup[key] = len(entries)
                    entries.append(key)
                idx[t, qrow * KEY_PAIRS + j] = lookup[key]
    return entries, idx


def _na_bias_tiles(rel_bias, entries):
    cols = np.arange(GRID_W)
    col_start = np.clip(cols - NA_KW // 2, 0, GRID_W - NA_KW)
    kc = np.arange(GRID_W)
    in_win = (kc[None, :] >= col_start[:, None]) & (kc[None, :] < col_start[:, None] + NA_KW)
    dc = np.clip(kc[None, :] - cols[:, None] + (NA_KW - 1), 0, 2 * NA_KW - 2)
    toe = rel_bias.astype(jnp.float32)[:, :, dc]
    toe = jnp.where(jnp.asarray(in_win)[None, None], toe, MASK_VALUE)
    masked = jnp.full((rel_bias.shape[0], GRID_W, GRID_W), MASK_VALUE, jnp.float32)
    tiles = []
    for dr_a, dr_b in entries:
        a = toe[:, dr_a] if dr_a >= 0 else masked
        b = toe[:, dr_b] if dr_b >= 0 else masked
        tiles.append(jnp.concatenate([a, b], axis=-1))
    return jnp.stack(tiles, axis=0)


def _swa_band_tiles():
    q = np.arange(TQ)[:, None]
    k = np.arange(SWA_KEYS)[None, :] - TQ
    band = np.abs(k - q) <= SWA_WINDOW
    left_ok = np.concatenate([np.zeros((1, TQ), bool), np.ones((1, 2 * TQ), bool)], axis=1)
    right_ok = np.concatenate([np.ones((1, 2 * TQ), bool), np.zeros((1, TQ), bool)], axis=1)
    variants = np.stack([band & left_ok, band, band & right_ok])
    return np.where(variants, 0.0, MASK_VALUE).astype(np.float32)


def _rope_tables(seq):
    inv_freq = ROPE_THETA ** (-jnp.arange(0, HEAD_DIM, 2, dtype=jnp.float32) / HEAD_DIM)
    ang = jnp.arange(seq, dtype=jnp.int32).astype(jnp.float32)[:, None] * inv_freq[None, :]
    cos = jnp.cos(ang)
    sin = jnp.sin(ang)
    cos_head = jnp.concatenate([cos, cos], axis=1)
    sin_head = jnp.concatenate([-sin, sin], axis=1)
    cos_tok = jnp.concatenate([cos_head, cos_head], axis=1)
    sin_tok = jnp.concatenate([sin_head, sin_head], axis=1)
    return cos_tok, sin_tok, cos_head.T, sin_head.T


def _proj_kernel(x_ref, wt_ref, wk_ref, cos_ref, sin_ref, cost_ref, sint_ref,
                 qa_ref, va_ref, qb_ref, vb_ref, gate_ref, kta_ref, ktb_ref):
    bf16 = jnp.bfloat16
    xb = x_ref[...].astype(bf16)

    def proj(span):
        return jnp.dot(xb, wt_ref[:, span[0]:span[1]], preferred_element_type=jnp.float32)

    qa_ref[...] = proj(_QA).astype(bf16)
    va_ref[...] = proj(_VA).astype(bf16)
    vb_ref[...] = proj(_VB).astype(bf16)

    t = proj(_QB)
    cos = cos_ref[...]
    sin = sin_ref[...]
    lane = lax.broadcasted_iota(jnp.int32, (PROJ_TM, LANES), 1)
    low_half = (lane % HEAD_DIM) < (HEAD_DIM // 2)
    for c in range(SWA_WIDTH // LANES):
        tc = t[:, c * LANES:(c + 1) * LANES]
        partner = jnp.where(low_half, pltpu.roll(tc, LANES - HEAD_DIM // 2, 1), pltpu.roll(tc, HEAD_DIM // 2, 1))
        qb_ref[:, c * LANES:(c + 1) * LANES] = (tc * cos + partner * sin).astype(bf16)

    z = proj(_Z)
    gate_ref[...] = (z * jax.nn.sigmoid(z)).astype(bf16)

    kt = lax.dot_general(wk_ref[...], xb, (((1,), (1,)), ((), ())), preferred_element_type=jnp.float32)
    for j in range(PROJ_TM // LANES):
        kta_ref[0, j] = kt[0:NA_WIDTH, j * LANES:(j + 1) * LANES].astype(bf16)
    cos_t = cost_ref[...]
    sin_t = sint_ref[...]
    half = HEAD_DIM // 2
    parts = []
    for u in range(2 * SWA_KV_HEADS):
        blk = kt[NA_WIDTH + u * HEAD_DIM:NA_WIDTH + (u + 1) * HEAD_DIM]
        partner = jnp.concatenate([blk[half:], blk[:half]], axis=0)
        parts.append(blk * cos_t + partner * sin_t)
    kb = jnp.concatenate(parts, axis=0).astype(bf16)
    for j in range(PROJ_TM // LANES):
        ktb_ref[0, j] = kb[:, j * LANES:(j + 1) * LANES]


def _attn_kernel(idx_ref, sink_ref,
                 qa_ref, kta_ref, va_ref, qb_ref, ktb_ref, vb_ref, gate_ref, x_ref, wout_ref,
                 tbl_ref, band_ref, gain_ref, beta_ref, out_ref, *, rows, n_blocks):
    f32, bf16 = jnp.float32, jnp.bfloat16
    t = pl.program_id(1)
    lane = lax.broadcasted_iota(jnp.int32, (TQ, LANES), 1)
    low = lane < HEAD_DIM

    ks = jnp.clip(Q_ROWS * t - min(NA_KH, rows) // 2, 0, rows - NA_KEY_ROWS)
    kblk = ks // 2
    key0 = pl.multiple_of(ks * GRID_W, LANES)
    mixed_cols = []
    for p in range(NA_HEADS // 2):
        cols = slice(p * LANES, (p + 1) * LANES)
        qp = qa_ref[:, cols]
        zero = jnp.zeros_like(qp)
        kt = jnp.concatenate([kta_ref[0, kblk + j, cols, :] for j in range(KEY_PAIRS)], axis=1)
        v = va_ref[0, pl.ds(key0, NA_KEYS), cols]
        outs = []
        for par in range(2):
            h = 2 * p + par
            qm = jnp.where(low, qp, zero) if par == 0 else jnp.where(low, zero, qp)
            s = jnp.dot(qm, kt, preferred_element_type=f32)
            bias = jnp.concatenate(
                [jnp.concatenate([tbl_ref[idx_ref[t * (Q_ROWS * KEY_PAIRS) + qrow * KEY_PAIRS + j], h]
                                  for j in range(KEY_PAIRS)], axis=1)
                 for qrow in range(Q_ROWS)], axis=0)
            s = s + bias
            m = jnp.max(s, axis=-1, keepdims=True)
            e = jnp.exp(s - m)
            l = jnp.sum(e, axis=-1, keepdims=True)
            o = jnp.dot(e.astype(bf16), v, preferred_element_type=f32)
            outs.append(o * (1.0 / l))
        mixed_cols.append(jnp.where(low, outs[0], outs[1]))

    jl = jnp.maximum(t - 1, 0)
    jr = jnp.minimum(t + 1, n_blocks - 1)
    variant = jnp.where(t == 0, 0, jnp.where(t == n_blocks - 1, 2, 1))
    band = band_ref[variant]
    band4 = jnp.concatenate([band] * SWA_GROUP, axis=0)
    yb_cols = [None] * (SWA_Q_HEADS // 2)
    for g in range(SWA_KV_HEADS):
        gcols = slice(g * LANES, (g + 1) * LANES)
        kt = jnp.concatenate([ktb_ref[0, j, gcols, :] for j in (jl, t, jr)], axis=1)
        v = jnp.concatenate([vb_ref[0, pl.ds(pl.multiple_of(j * TQ, TQ), TQ), gcols] for j in (jl, t, jr)],
                            axis=0)
        qs = []
        for pp in range(SWA_GROUP // 2):
            qp = qb_ref[:, (2 * g + pp) * LANES:(2 * g + pp + 1) * LANES]
            zero = jnp.zeros_like(qp)
            qs += [jnp.where(low, qp, zero), jnp.where(low, zero, qp)]
        q4 = jnp.concatenate(qs, axis=0)
        s = jnp.dot(q4, kt, preferred_element_type=f32) + band4
        sink = jnp.concatenate([jnp.full((TQ, 1), sink_ref[SWA_GROUP * g + j], f32) for j in range(SWA_GROUP)],
                               axis=0)
        m = jnp.maximum(jnp.max(s, axis=-1, keepdims=True), sink)
        e = jnp.exp(s - m)
        l = jnp.sum(e, axis=-1, keepdims=True) + jnp.exp(sink - m)
        o = jnp.dot(e.astype(bf16), v, preferred_element_type=f32) * (1.0 / l)
        for pp in range(SWA_GROUP // 2):
            yb_cols[2 * g + pp] = jnp.where(low, o[(2 * pp) * TQ:(2 * pp + 1) * TQ],
                                            o[(2 * pp + 1) * TQ:(2 * pp + 2) * TQ])

    mixed = jnp.concatenate(mixed_cols + yb_cols, axis=1)
    mixed = (mixed * gate_ref[...].astype(f32)).astype(bf16)
    y = jnp.dot(mixed, wout_ref[...], preferred_element_type=f32)
    r = DEEPNORM_ALPHA * x_ref[...] + y
    mu = jnp.mean(r, axis=-1, keepdims=True)
    d = r - mu
    var = jnp.mean(d * d, axis=-1, keepdims=True)
    out_ref[...] = d * lax.rsqrt(var + LN_EPS) * gain_ref[...] + beta_ref[...]


def _rotate_half_free_weights(w_in):
    offs = np.cumsum([0, NA_WIDTH, NA_WIDTH, NA_WIDTH, NA_WIDTH, SWA_WIDTH, SWA_KV_WIDTH, SWA_KV_WIDTH, SWA_WIDTH])
    sec = [w_in[:, offs[i]:offs[i + 1]] for i in range(8)]
    q_a, k_a, v_a, z_a, q_b, k_b, v_b, z_b = sec
    scale = HEAD_DIM ** -0.5
    dup = lambda w: jnp.concatenate(
        [w[:, g * HEAD_DIM:(g + 1) * HEAD_DIM] for g in range(SWA_KV_HEADS) for _ in range(2)], axis=1)
    w_tok = jnp.concatenate([q_a * scale, v_a, q_b * scale, dup(v_b), z_a, z_b], axis=1).astype(jnp.bfloat16)
    w_kt = jnp.concatenate([k_a, dup(k_b)], axis=1).T.astype(jnp.bfloat16)
    return w_tok, w_kt


def kernel(x, w_in, rel_pos_bias, sink_logits, w_out, ln_gain, ln_bias):
    B, S, D = x.shape
    assert D == D_MODEL and S % PROJ_TM == 0 and S % GRID_W == 0
    rows = S // GRID_W
    n_blocks = S // TQ
    n_tok = B * S
    bf16 = jnp.bfloat16
    x2 = x.reshape(n_tok, D)

    cos_tok, sin_tok, cos_feat, sin_feat = _rope_tables(S)
    entries, idx_np = _na_tile_plan(rows)
    band = jnp.asarray(_swa_band_tiles())

    for layer in range(DEPTH):
        w_tok, w_kt = _rotate_half_free_weights(w_in[layer])
        tiles_per_seq = S // PROJ_TM
        blk128 = PROJ_TM // LANES
        qa, va, qb, vb, gate, kta, ktb = pl.pallas_call(
            _proj_kernel,
            grid=(n_tok // PROJ_TM,),
            in_specs=[
                pl.BlockSpec((PROJ_TM, D), lambda i: (i, 0)),
                pl.BlockSpec((D, TOK_WIDTH), lambda i: (0, 0)),
                pl.BlockSpec((KT_ROWS, D), lambda i: (0, 0)),
                pl.BlockSpec((PROJ_TM, LANES), lambda i: (i % tiles_per_seq, 0)),
                pl.BlockSpec((PROJ_TM, LANES), lambda i: (i % tiles_per_seq, 0)),
                pl.BlockSpec((HEAD_DIM, PROJ_TM), lambda i: (0, i % tiles_per_seq)),
                pl.BlockSpec((HEAD_DIM, PROJ_TM), lambda i: (0, i % tiles_per_seq)),
            ],
            out_specs=[
                pl.BlockSpec((PROJ_TM, NA_WIDTH), lambda i: (i, 0)),
                pl.BlockSpec((PROJ_TM, NA_WIDTH), lambda i: (i, 0)),
                pl.BlockSpec((PROJ_TM, SWA_WIDTH), lambda i: (i, 0)),
                pl.BlockSpec((PROJ_TM, 2 * SWA_KV_WIDTH), lambda i: (i, 0)),
                pl.BlockSpec((PROJ_TM, MIX_WIDTH), lambda i: (i, 0)),
                pl.BlockSpec((1, blk128, NA_WIDTH, LANES), lambda i: (i // tiles_per_seq, i % tiles_per_seq, 0, 0)),
                pl.BlockSpec((1, blk128, 2 * SWA_KV_WIDTH, LANES),
                             lambda i: (i // tiles_per_seq, i % tiles_per_seq, 0, 0)),
            ],
            out_shape=[
                jax.ShapeDtypeStruct((n_tok, NA_WIDTH), bf16),
                jax.ShapeDtypeStruct((n_tok, NA_WIDTH), bf16),
                jax.ShapeDtypeStruct((n_tok, SWA_WIDTH), bf16),
                jax.ShapeDtypeStruct((n_tok, 2 * SWA_KV_WIDTH), bf16),
                jax.ShapeDtypeStruct((n_tok, MIX_WIDTH), bf16),
                jax.ShapeDtypeStruct((B, S // LANES, NA_WIDTH, LANES), bf16),
                jax.ShapeDtypeStruct((B, S // LANES, 2 * SWA_KV_WIDTH, LANES), bf16),
            ],
            compiler_params=pltpu.CompilerParams(
                dimension_semantics=("arbitrary",), vmem_limit_bytes=48 * 1024 * 1024),
            name="in_proj",
        )(x2, w_tok, w_kt, cos_tok, sin_tok, cos_feat, sin_feat)

        tbl = _na_bias_tiles(rel_pos_bias[layer], entries)
        idx = jnp.asarray(idx_np.reshape(-1))
        n_ent = len(entries)
        grid_spec = pltpu.PrefetchScalarGridSpec(
            num_scalar_prefetch=2,
            grid=(B, n_blocks),
            in_specs=[
                pl.BlockSpec((TQ, NA_WIDTH), lambda b, t, *_: (b * n_blocks + t, 0)),
                pl.BlockSpec((1, S // LANES, NA_WIDTH, LANES), lambda b, t, *_: (b, 0, 0, 0)),
                pl.BlockSpec((1, S, NA_WIDTH), lambda b, t, *_: (b, 0, 0)),
                pl.BlockSpec((TQ, SWA_WIDTH), lambda b, t, *_: (b * n_blocks + t, 0)),
                pl.BlockSpec((1, S // LANES, 2 * SWA_KV_WIDTH, LANES), lambda b, t, *_: (b, 0, 0, 0)),
                pl.BlockSpec((1, S, 2 * SWA_KV_WIDTH), lambda b, t, *_: (b, 0, 0)),
                pl.BlockSpec((TQ, MIX_WIDTH), lambda b, t, *_: (b * n_blocks + t, 0)),
                pl.BlockSpec((TQ, D), lambda b, t, *_: (b * n_blocks + t, 0)),
                pl.BlockSpec((MIX_WIDTH, D), lambda b, t, *_: (0, 0)),
                pl.BlockSpec((n_ent, NA_HEADS, GRID_W, 2 * GRID_W), lambda b, t, *_: (0, 0, 0, 0)),
                pl.BlockSpec((3, TQ, SWA_KEYS), lambda b, t, *_: (0, 0, 0)),
                pl.BlockSpec((1, D), lambda b, t, *_: (0, 0)),
                pl.BlockSpec((1, D), lambda b, t, *_: (0, 0)),
            ],
            out_specs=pl.BlockSpec((TQ, D), lambda b, t, *_: (b * n_blocks + t, 0)),
        )
        x2 = pl.pallas_call(
            functools.partial(_attn_kernel, rows=rows, n_blocks=n_blocks),
            grid_spec=grid_spec,
            out_shape=jax.ShapeDtypeStruct((n_tok, D), jnp.float32),
            compiler_params=pltpu.CompilerParams(
                dimension_semantics=("arbitrary", "arbitrary"), vmem_limit_bytes=56 * 1024 * 1024),
            name="attn_out",
        )(idx, sink_logits[layer].astype(jnp.float32),
          qa, kta, va.reshape(B, S, NA_WIDTH), qb, ktb, vb.reshape(B, S, 2 * SWA_KV_WIDTH), gate, x2,
          w_out[layer].astype(bf16), tbl, band, ln_gain[layer].reshape(1, D), ln_bias[layer].reshape(1, D))
    return x2.reshape(B, S, D)
```

```python
import functools

import numpy as np
import jax
import jax.numpy as jnp
from jax import lax
from jax.experimental import pallas as pl
from jax.experimental.pallas import tpu as pltpu

D_MODEL = 1024
HEAD_DIM = 64
NA_HEADS = 8
NA_WIDTH = NA_HEADS * HEAD_DIM
NA_KH = 8
NA_KW = 16
GRID_W = 64
SWA_Q_HEADS = 8
SWA_KV_HEADS = 2
SWA_GROUP = SWA_Q_HEADS // SWA_KV_HEADS
SWA_WIDTH = SWA_Q_HEADS * HEAD_DIM
SWA_KV_WIDTH = SWA_KV_HEADS * HEAD_DIM
SWA_WINDOW = 128
ROPE_THETA = 10000.0
MIX_WIDTH = NA_WIDTH + SWA_WIDTH
LN_EPS = 1e-5
MASK_VALUE = -1e30
DEPTH = 1
DEEPNORM_ALPHA = (2.0 * DEPTH) ** 0.25

LANES = 128
PAIR = 2 * HEAD_DIM
assert PAIR == LANES

PROJ_TM = 512
TQ = 128
Q_ROWS = TQ // GRID_W
NA_KEY_ROWS = 10
NA_KEYS = NA_KEY_ROWS * GRID_W
KEY_PAIRS = NA_KEYS // LANES
SWA_KEYS = 3 * TQ

_QA = (0, NA_WIDTH)
_VA = (_QA[1], _QA[1] + NA_WIDTH)
_QB = (_VA[1], _VA[1] + SWA_WIDTH)
_VB = (_QB[1], _QB[1] + 2 * SWA_KV_WIDTH)
_Z = (_VB[1], _VB[1] + MIX_WIDTH)
TOK_WIDTH = _Z[1]
KT_ROWS = NA_WIDTH + 2 * SWA_KV_WIDTH


def _na_tile_plan(rows):
    kh = min(NA_KH, rows)
    entries, lookup = [], {}
    n_steps = rows // Q_ROWS
    idx = np.zeros((n_steps, Q_ROWS * KEY_PAIRS), np.int32)
    for t in range(n_steps):
        ks = int(np.clip(Q_ROWS * t - kh // 2, 0, rows - NA_KEY_ROWS))
        for qrow in range(Q_ROWS):
            r = Q_ROWS * t + qrow
            r0 = int(np.clip(r - kh // 2, 0, rows - kh))
            assert ks <= r0 and r0 + kh <= ks + NA_KEY_ROWS
            for j in range(KEY_PAIRS):
                key = []
                for i in (2 * j, 2 * j + 1):
                    kr = ks + i
                    key.append(kr - r + (NA_KH - 1) if r0 <= kr < r0 + kh else -1)
                key = tuple(key)
                if key not in lookup:
                    lookup[key] = len(entries)
                    entries.append(key)
                idx[t, qrow * KEY_PAIRS + j] = lookup[key]
    return entries, idx


def _na_bias_tiles(rel_bias, entries):
    cols = np.arange(GRID_W)
    col_start = np.clip(cols - NA_KW // 2, 0, GRID_W - NA_KW)
    kc = np.arange(GRID_W)
    in_win = (kc[None, :] >= col_start[:, None]) & (kc[None, :] < col_start[:, None] + NA_KW)
    dc = np.clip(kc[None, :] - cols[:, None] + (NA_KW - 1), 0, 2 * NA_KW - 2)
    toe = rel_bias.astype(jnp.float32)[:, :, dc]
    toe = jnp.where(jnp.asarray(in_win)[None, None], toe, MASK_VALUE)
    masked = jnp.full((rel_bias.shape[0], GRID_W, GRID_W), MASK_VALUE, jnp.float32)
    tiles = []
    for dr_a, dr_b in entries:
        a = toe[:, dr_a] if dr_a >= 0 else masked
        b = toe[:, dr_b] if dr_b >= 0 else masked
        tiles.append(jnp.concatenate([a, b], axis=-1))
    return jnp.stack(tiles, axis=0)


def _swa_band_tiles():
    q = np.arange(TQ)[:, None]
    k = np.arange(SWA_KEYS)[None, :] - TQ
    band = np.abs(k - q) <= SWA_WINDOW
    left_ok = np.concatenate([np.zeros((1, TQ), bool), np.ones((1, 2 * TQ), bool)], axis=1)
    right_ok = np.concatenate([np.ones((1, 2 * TQ), bool), np.zeros((1, TQ), bool)], axis=1)
    variants = np.stack([band & left_ok, band, band & right_ok])
    return np.where(variants, 0.0, MASK_VALUE).astype(np.float32)


def _rope_tables(seq):
    inv_freq = ROPE_THETA ** (-jnp.arange(0, HEAD_DIM, 2, dtype=jnp.float32) / HEAD_DIM)
    ang = jnp.arange(seq, dtype=jnp.int32).astype(jnp.float32)[:, None] * inv_freq[None, :]
    cos = jnp.cos(ang)
    sin = jnp.sin(ang)
    cos_head = jnp.concatenate([cos, cos], axis=1)
    sin_head = jnp.concatenate([-sin, sin], axis=1)
    cos_tok = jnp.concatenate([cos_head, cos_head], axis=1)
    sin_tok = jnp.concatenate([sin_head, sin_head], axis=1)
    return cos_tok, sin_tok, cos_head.T, sin_head.T


def _proj_kernel(x_ref, wt_ref, wk_ref, cos_ref, sin_ref, cost_ref, sint_ref,
                 qa_ref, va_ref, qb_ref, vb_ref, gate_ref, kta_ref, ktb_ref):
    bf16 = jnp.bfloat16
    xb = x_ref[...].astype(bf16)

    def proj(span):
        return jnp.dot(xb, wt_ref[:, span[0]:span[1]], preferred_element_type=jnp.float32)

    qa_ref[...] = proj(_QA).astype(bf16)
    va_ref[...] = proj(_VA).astype(bf16)
    vb_ref[...] = proj(_VB).astype(bf16)

    t = proj(_QB)
    cos = cos_ref[...]
    sin = sin_ref[...]
    lane = lax.broadcasted_iota(jnp.int32, (PROJ_TM, LANES), 1)
    low_half = (lane % HEAD_DIM) < (HEAD_DIM // 2)
    for c in range(SWA_WIDTH // LANES):
        tc = t[:, c * LANES:(c + 1) * LANES]
        partner = jnp.where(low_half, pltpu.roll(tc, LANES - HEAD_DIM // 2, 1), pltpu.roll(tc, HEAD_DIM // 2, 1))
        qb_ref[:, c * LANES:(c + 1) * LANES] = (tc * cos + partner * sin).astype(bf16)

    z = proj(_Z)
    gate_ref[...] = (z * jax.nn.sigmoid(z)).astype(bf16)

    kt = lax.dot_general(wk_ref[...], xb, (((1,), (1,)), ((), ())), preferred_element_type=jnp.float32)
    for j in range(PROJ_TM // LANES):
        kta_ref[0, j] = kt[0:NA_WIDTH, j * LANES:(j + 1) * LANES].astype(bf16)
    cos_t = cost_ref[...]
    sin_t = sint_ref[...]
    half = HEAD_DIM // 2
    parts = []
    for u in range(2 * SWA_KV_HEADS):
        blk = kt[NA_WIDTH + u * HEAD_DIM:NA_WIDTH + (u + 1) * HEAD_DIM]
        partner = jnp.concatenate([blk[half:], blk[:half]], axis=0)
        parts.append(blk * cos_t + partner * sin_t)
    kb = jnp.concatenate(parts, axis=0).astype(bf16)
    for j in range(PROJ_TM // LANES):
        ktb_ref[0, j] = kb[:, j * LANES:(j + 1) * LANES]


def _attn_kernel(idx_ref, sink_ref,
                 qa_ref, kta_ref, va_ref, qb_ref, ktb_ref, vb_ref, gate_ref, x_ref, wout_ref,
                 tbl_ref, band_ref, gain_ref, beta_ref, out_ref, *, rows, n_blocks):
    f32, bf16 = jnp.float32, jnp.bfloat16
    t = pl.program_id(1)
    lane = lax.broadcasted_iota(jnp.int32, (TQ, LANES), 1)
    low = lane < HEAD_DIM

    ks = jnp.clip(Q_ROWS * t - min(NA_KH, rows) // 2, 0, rows - NA_KEY_ROWS)
    kblk = ks // 2
    key0 = pl.multiple_of(ks * GRID_W, LANES)
    jl = jnp.maximum(t - 1, 0)
    jr = jnp.minimum(t + 1, n_blocks - 1)
    variant = jnp.where(t == 0, 0, jnp.where(t == n_blocks - 1, 2, 1))

    na_s, na_v = [], []
    for p in range(NA_HEADS // 2):
        cols = slice(p * LANES, (p + 1) * LANES)
        qp = qa_ref[:, cols]
        zero = jnp.zeros_like(qp)
        kt = jnp.concatenate([kta_ref[0, kblk + j, cols, :] for j in range(KEY_PAIRS)], axis=1)
        na_v.append(va_ref[0, pl.ds(key0, NA_KEYS), cols])
        for par in range(2):
            qm = jnp.where(low, qp, zero) if par == 0 else jnp.where(low, zero, qp)
            na_s.append(jnp.dot(qm, kt, preferred_element_type=f32))
    sw_s, sw_v = [], []
    for g in range(SWA_KV_HEADS):
        gcols = slice(g * LANES, (g + 1) * LANES)
        kt = jnp.concatenate([ktb_ref[0, j, gcols, :] for j in (jl, t, jr)], axis=1)
        sw_v.append(jnp.concatenate(
            [vb_ref[0, pl.ds(pl.multiple_of(j * TQ, TQ), TQ), gcols] for j in (jl, t, jr)], axis=0))
        qs = []
        for pp in range(SWA_GROUP // 2):
            qp = qb_ref[:, (2 * g + pp) * LANES:(2 * g + pp + 1) * LANES]
            zero = jnp.zeros_like(qp)
            qs += [jnp.where(low, qp, zero), jnp.where(low, zero, qp)]
        q4 = jnp.concatenate(qs, axis=0)
        sw_s.append(jnp.dot(q4, kt, preferred_element_type=f32))

    na_e, na_l = [], []
    for h in range(NA_HEADS):
        bias = jnp.concatenate(
            [jnp.concatenate([tbl_ref[idx_ref[t * (Q_ROWS * KEY_PAIRS) + qrow * KEY_PAIRS + j], h]
                              for j in range(KEY_PAIRS)], axis=1)
             for qrow in range(Q_ROWS)], axis=0)
        s = na_s[h] + bias
        m = jnp.max(s, axis=-1, keepdims=True)
        e = jnp.exp(s - m)
        na_l.append(jnp.sum(e, axis=-1, keepdims=True))
        na_e.append(e.astype(bf16))
    band = band_ref[variant]
    band4 = jnp.concatenate([band] * SWA_GROUP, axis=0)
    sw_e, sw_l = [], []
    for g in range(SWA_KV_HEADS):
        s = sw_s[g] + band4
        sink = jnp.concatenate([jnp.full((TQ, 1), sink_ref[SWA_GROUP * g + j], f32) for j in range(SWA_GROUP)],
                               axis=0)
        m = jnp.maximum(jnp.max(s, axis=-1, keepdims=True), sink)
        e = jnp.exp(s - m)
        sw_l.append(jnp.sum(e, axis=-1, keepdims=True) + jnp.exp(sink - m))
        sw_e.append(e.astype(bf16))

    mixed_cols = []
    for p in range(NA_HEADS // 2):
        outs = [jnp.dot(na_e[2 * p + par], na_v[p], preferred_element_type=f32) * (1.0 / na_l[2 * p + par])
                for par in range(2)]
        mixed_cols.append(jnp.where(low, outs[0], outs[1]))
    yb_cols = [None] * (SWA_Q_HEADS // 2)
    for g in range(SWA_KV_HEADS):
        o = jnp.dot(sw_e[g], sw_v[g], preferred_element_type=f32) * (1.0 / sw_l[g])
        for pp in range(SWA_GROUP // 2):
            yb_cols[2 * g + pp] = jnp.where(low, o[(2 * pp) * TQ:(2 * pp + 1) * TQ],
                                            o[(2 * pp + 1) * TQ:(2 * pp + 2) * TQ])

    mixed = jnp.concatenate(mixed_cols + yb_cols, axis=1)
    mixed = (mixed * gate_ref[...].astype(f32)).astype(bf16)
    y = jnp.dot(mixed, wout_ref[...], preferred_element_type=f32)
    r = DEEPNORM_ALPHA * x_ref[...] + y
    mu = jnp.mean(r, axis=-1, keepdims=True)
    d = r - mu
    var = jnp.mean(d * d, axis=-1, keepdims=True)
    out_ref[...] = d * lax.rsqrt(var + LN_EPS) * gain_ref[...] + beta_ref[...]


def _rotate_half_free_weights(w_in):
    offs = np.cumsum([0, NA_WIDTH, NA_WIDTH, NA_WIDTH, NA_WIDTH, SWA_WIDTH, SWA_KV_WIDTH, SWA_KV_WIDTH, SWA_WIDTH])
    sec = [w_in[:, offs[i]:offs[i + 1]] for i in range(8)]
    q_a, k_a, v_a, z_a, q_b, k_b, v_b, z_b = sec
    scale = HEAD_DIM ** -0.5
    dup = lambda w: jnp.concatenate(
        [w[:, g * HEAD_DIM:(g + 1) * HEAD_DIM] for g in range(SWA_KV_HEADS) for _ in range(2)], axis=1)
    w_tok = jnp.concatenate([q_a * scale, v_a, q_b * scale, dup(v_b), z_a, z_b], axis=1).astype(jnp.bfloat16)
    w_kt = jnp.concatenate([k_a, dup(k_b)], axis=1).T.astype(jnp.bfloat16)
    return w_tok, w_kt


def kernel(x, w_in, rel_pos_bias, sink_logits, w_out, ln_gain, ln_bias):
    B, S, D = x.shape
    assert D == D_MODEL and S % PROJ_TM == 0 and S % GRID_W == 0
    rows = S // GRID_W
    n_blocks = S // TQ
    n_tok = B * S
    bf16 = jnp.bfloat16
    x2 = x.reshape(n_tok, D)

    cos_tok, sin_tok, cos_feat, sin_feat = _rope_tables(S)
    entries, idx_np = _na_tile_plan(rows)
    band = jnp.asarray(_swa_band_tiles())

    for layer in range(DEPTH):
        w_tok, w_kt = _rotate_half_free_weights(w_in[layer])
        tiles_per_seq = S // PROJ_TM
        blk128 = PROJ_TM // LANES
        qa, va, qb, vb, gate, kta, ktb = pl.pallas_call(
            _proj_kernel,
            grid=(n_tok // PROJ_TM,),
            in_specs=[
                pl.BlockSpec((PROJ_TM, D), lambda i: (i, 0)),
                pl.BlockSpec((D, TOK_WIDTH), lambda i: (0, 0)),
                pl.BlockSpec((KT_ROWS, D), lambda i: (0, 0)),
                pl.BlockSpec((PROJ_TM, LANES), lambda i: (i % tiles_per_seq, 0)),
                pl.BlockSpec((PROJ_TM, LANES), lambda i: (i % tiles_per_seq, 0)),
                pl.BlockSpec((HEAD_DIM, PROJ_TM), lambda i: (0, i % tiles_per_seq)),
                pl.BlockSpec((HEAD_DIM, PROJ_TM), lambda i: (0, i % tiles_per_seq)),
            ],
            out_specs=[
                pl.BlockSpec((PROJ_TM, NA_WIDTH), lambda i: (i, 0)),
                pl.BlockSpec((PROJ_TM, NA_WIDTH), lambda i: (i, 0)),
                pl.BlockSpec((PROJ_TM, SWA_WIDTH), lambda i: (i, 0)),
                pl.BlockSpec((PROJ_TM, 2 * SWA_KV_WIDTH), lambda i: (i, 0)),
                pl.BlockSpec((PROJ_TM, MIX_WIDTH), lambda i: (i, 0)),
                pl.BlockSpec((1, blk128, NA_WIDTH, LANES), lambda i: (i // tiles_per_seq, i % tiles_per_seq, 0, 0)),
                pl.BlockSpec((1, blk128, 2 * SWA_KV_WIDTH, LANES),
                             lambda i: (i // tiles_per_seq, i % tiles_per_seq, 0, 0)),
            ],
            out_shape=[
                jax.ShapeDtypeStruct((n_tok, NA_WIDTH), bf16),
                jax.ShapeDtypeStruct((n_tok, NA_WIDTH), bf16),
                jax.ShapeDtypeStruct((n_tok, SWA_WIDTH), bf16),
                jax.ShapeDtypeStruct((n_tok, 2 * SWA_KV_WIDTH), bf16),
                jax.ShapeDtypeStruct((n_tok, MIX_WIDTH), bf16),
                jax.ShapeDtypeStruct((B, S // LANES, NA_WIDTH, LANES), bf16),
                jax.ShapeDtypeStruct((B, S // LANES, 2 * SWA_KV_WIDTH, LANES), bf16),
            ],
            compiler_params=pltpu.CompilerParams(
                dimension_semantics=("arbitrary",), vmem_limit_bytes=48 * 1024 * 1024),
            name="in_proj",
        )(x2, w_tok, w_kt, cos_tok, sin_tok, cos_feat, sin_feat)

        tbl = _na_bias_tiles(rel_pos_bias[layer], entries)
        idx = jnp.asarray(idx_np.reshape(-1))
        n_ent = len(entries)
        grid_spec = pltpu.PrefetchScalarGridSpec(
            num_scalar_prefetch=2,
            grid=(B, n_blocks),
            in_specs=[
                pl.BlockSpec((TQ, NA_WIDTH), lambda b, t, *_: (b * n_blocks + t, 0)),
                pl.BlockSpec((1, S // LANES, NA_WIDTH, LANES), lambda b, t, *_: (b, 0, 0, 0)),
                pl.BlockSpec((1, S, NA_WIDTH), lambda b, t, *_: (b, 0, 0)),
                pl.BlockSpec((TQ, SWA_WIDTH), lambda b, t, *_: (b * n_blocks + t, 0)),
                pl.BlockSpec((1, S // LANES, 2 * SWA_KV_WIDTH, LANES), lambda b, t, *_: (b, 0, 0, 0)),
                pl.BlockSpec((1, S, 2 * SWA_KV_WIDTH), lambda b, t, *_: (b, 0, 0)),
                pl.BlockSpec((TQ, MIX_WIDTH), lambda b, t, *_: (b * n_blocks + t, 0)),
                pl.BlockSpec((TQ, D), lambda b, t, *_: (b * n_blocks + t, 0)),
                pl.BlockSpec((MIX_WIDTH, D), lambda b, t, *_: (0, 0)),
                pl.BlockSpec((n_ent, NA_HEADS, GRID_W, 2 * GRID_W), lambda b, t, *_: (0, 0, 0, 0)),
                pl.BlockSpec((3, TQ, SWA_KEYS), lambda b, t, *_: (0, 0, 0)),
                pl.BlockSpec((1, D), lambda b, t, *_: (0, 0)),
                pl.BlockSpec((1, D), lambda b, t, *_: (0, 0)),
            ],
            out_specs=pl.BlockSpec((TQ, D), lambda b, t, *_: (b * n_blocks + t, 0)),
        )
        x2 = pl.pallas_call(
            functools.partial(_attn_kernel, rows=rows, n_blocks=n_blocks),
            grid_spec=grid_spec,
            out_shape=jax.ShapeDtypeStruct((n_tok, D), jnp.float32),
            compiler_params=pltpu.CompilerParams(
                dimension_semantics=("arbitrary", "arbitrary"), vmem_limit_bytes=56 * 1024 * 1024),
            name="attn_out",
        )(idx, sink_logits[layer].astype(jnp.float32),
          qa, kta, va.reshape(B, S, NA_WIDTH), qb, ktb, vb.reshape(B, S, 2 * SWA_KV_WIDTH), gate, x2,
          w_out[layer].astype(bf16), tbl, band, ln_gain[layer].reshape(1, D), ln_bias[layer].reshape(1, D))
    return x2.reshape(B, S, D)
```

```python
import functools

import numpy as np
import jax
import jax.numpy as jnp
from jax import lax
from jax.experimental import pallas as pl
from jax.experimental.pallas import tpu as pltpu

D_MODEL = 1024
HEAD_DIM = 64
NA_HEADS = 8
NA_WIDTH = NA_HEADS * HEAD_DIM
NA_KH = 8
NA_KW = 16
GRID_W = 64
SWA_Q_HEADS = 8
SWA_KV_HEADS = 2
SWA_GROUP = SWA_Q_HEADS // SWA_KV_HEADS
SWA_WIDTH = SWA_Q_HEADS * HEAD_DIM
SWA_KV_WIDTH = SWA_KV_HEADS * HEAD_DIM
SWA_WINDOW = 128
ROPE_THETA = 10000.0
MIX_WIDTH = NA_WIDTH + SWA_WIDTH
LN_EPS = 1e-5
MASK_VALUE = -1e30
DEPTH = 1
DEEPNORM_ALPHA = (2.0 * DEPTH) ** 0.25

LANES = 128
PAIR = 2 * HEAD_DIM
assert PAIR == LANES

PROJ_TM = 512
TQ = 128
Q_ROWS = TQ // GRID_W
NA_KEY_ROWS = 10
NA_KEYS = NA_KEY_ROWS * GRID_W
KEY_PAIRS = NA_KEYS // LANES
SWA_KEYS = 3 * TQ

_QA = (0, NA_WIDTH)
_VA = (_QA[1], _QA[1] + NA_WIDTH)
_QB = (_VA[1], _VA[1] + SWA_WIDTH)
_VB = (_QB[1], _QB[1] + 2 * SWA_KV_WIDTH)
_Z = (_VB[1], _VB[1] + MIX_WIDTH)
TOK_WIDTH = _Z[1]
_W_QA = 0
_W_KA = _W_QA + NA_WIDTH
_W_VA = _W_KA + NA_WIDTH
_W_ZA = _W_VA + NA_WIDTH
_W_QB = _W_ZA + NA_WIDTH
_W_KB = _W_QB + SWA_WIDTH
_W_VB = _W_KB + SWA_KV_WIDTH
_W_ZB = _W_VB + SWA_KV_WIDTH
IN_WIDTH = _W_ZB + SWA_WIDTH
KT_ROWS = NA_WIDTH + 2 * SWA_KV_WIDTH


def _na_tile_plan(rows):
    kh = min(NA_KH, rows)
    entries, lookup = [], {}
    n_steps = rows // Q_ROWS
    idx = np.zeros((n_steps, Q_ROWS * KEY_PAIRS), np.int32)
    for t in range(n_steps):
        ks = int(np.clip(Q_ROWS * t - kh // 2, 0, rows - NA_KEY_ROWS))
        for qrow in range(Q_ROWS):
            r = Q_ROWS * t + qrow
            r0 = int(np.clip(r - kh // 2, 0, rows - kh))
            assert ks <= r0 and r0 + kh <= ks + NA_KEY_ROWS
            for j in range(KEY_PAIRS):
                key = []
                for i in (2 * j, 2 * j + 1):
                    kr = ks + i
                    key.append(kr - r + (NA_KH - 1) if r0 <= kr < r0 + kh else -1)
                key = tuple(key)
                if key not in lookup:
                    lookup[key] = len(entries)
                    entries.append(key)
                idx[t, qrow * KEY_PAIRS + j] = lookup[key]
    return entries, idx


def _na_bias_tiles(rel_bias, entries):
    cols = np.arange(GRID_W)
    col_start = np.clip(cols - NA_KW // 2, 0, GRID_W - NA_KW)
    kc = np.arange(GRID_W)
    in_win = (kc[None, :] >= col_start[:, None]) & (kc[None, :] < col_start[:, None] + NA_KW)
    dc = np.clip(kc[None, :] - cols[:, None] + (NA_KW - 1), 0, 2 * NA_KW - 2)
    toe = rel_bias.astype(jnp.float32)[:, :, dc]
    toe = jnp.where(jnp.asarray(in_win)[None, None], toe, MASK_VALUE)
    masked = jnp.full((rel_bias.shape[0], GRID_W, GRID_W), MASK_VALUE, jnp.float32)
    tiles = []
    for dr_a, dr_b in entries:
        a = toe[:, dr_a] if dr_a >= 0 else masked
        b = toe[:, dr_b] if dr_b >= 0 else masked
        tiles.append(jnp.concatenate([a, b], axis=-1))
    return jnp.stack(tiles, axis=0)


def _swa_band_tiles():
    q = np.arange(TQ)[:, None]
    k = np.arange(SWA_KEYS)[None, :] - TQ
    band = np.abs(k - q) <= SWA_WINDOW
    left_ok = np.concatenate([np.zeros((1, TQ), bool), np.ones((1, 2 * TQ), bool)], axis=1)
    right_ok = np.concatenate([np.ones((1, 2 * TQ), bool), np.zeros((1, TQ), bool)], axis=1)
    variants = np.stack([band & left_ok, band, band & right_ok])
    return np.where(variants, 0.0, MASK_VALUE).astype(np.float32)


def _rope_tables(seq):
    inv_freq = ROPE_THETA ** (-jnp.arange(0, HEAD_DIM, 2, dtype=jnp.float32) / HEAD_DIM)
    ang = jnp.arange(seq, dtype=jnp.int32).astype(jnp.float32)[:, None] * inv_freq[None, :]
    cos = jnp.cos(ang)
    sin = jnp.sin(ang)
    cos_head = jnp.concatenate([cos, cos], axis=1)
    sin_head = jnp.concatenate([-sin, sin], axis=1)
    cos_tok = jnp.concatenate([cos_head, cos_head], axis=1)
    sin_tok = jnp.concatenate([sin_head, sin_head], axis=1)
    return cos_tok, sin_tok, cos_head.T, sin_head.T


def _build_proj_weights(win_ref, wtok_ref, wkt_ref):
    bf16 = jnp.bfloat16
    scale = HEAD_DIM ** -0.5
    wtok_ref[:, _QA[0]:_QA[1]] = (win_ref[:, _W_QA:_W_QA + NA_WIDTH] * scale).astype(bf16)
    wtok_ref[:, _VA[0]:_VA[1]] = win_ref[:, _W_VA:_W_VA + NA_WIDTH].astype(bf16)
    wtok_ref[:, _QB[0]:_QB[1]] = (win_ref[:, _W_QB:_W_QB + SWA_WIDTH] * scale).astype(bf16)
    wtok_ref[:, _Z[0]:_Z[0] + NA_WIDTH] = win_ref[:, _W_ZA:_W_ZA + NA_WIDTH].astype(bf16)
    wtok_ref[:, _Z[0] + NA_WIDTH:_Z[1]] = win_ref[:, _W_ZB:_W_ZB + SWA_WIDTH].astype(bf16)
    vpair = win_ref[:, _W_VB:_W_VB + SWA_KV_WIDTH]
    swapped = pltpu.roll(vpair, HEAD_DIM, 1)
    low = lax.broadcasted_iota(jnp.int32, vpair.shape, 1) < HEAD_DIM
    wtok_ref[:, _VB[0]:_VB[0] + LANES] = jnp.where(low, vpair, swapped).astype(bf16)
    wtok_ref[:, _VB[0] + LANES:_VB[1]] = jnp.where(low, swapped, vpair).astype(bf16)
    for c in range(NA_WIDTH // LANES):
        wkt_ref[c * LANES:(c + 1) * LANES, :] = win_ref[:, _W_KA + c * LANES:_W_KA + (c + 1) * LANES].T.astype(bf16)
    kb_t = win_ref[:, _W_KB:_W_KB + SWA_KV_WIDTH].T.astype(bf16)
    for g in range(SWA_KV_HEADS):
        for rep in range(2):
            r0 = NA_WIDTH + (2 * g + rep) * HEAD_DIM
            wkt_ref[r0:r0 + HEAD_DIM, :] = kb_t[g * HEAD_DIM:(g + 1) * HEAD_DIM]


def _proj_kernel(x_ref, win_ref, cos_ref, sin_ref, cost_ref, sint_ref,
                 qa_ref, va_ref, qb_ref, vb_ref, gate_ref, kta_ref, ktb_ref, wtok_ref, wkt_ref):
    bf16 = jnp.bfloat16

    @pl.when(pl.program_id(0) == 0)
    def _():
        _build_proj_weights(win_ref, wtok_ref, wkt_ref)

    xb = x_ref[...].astype(bf16)

    def proj(span):
        return jnp.dot(xb, wtok_ref[:, span[0]:span[1]], preferred_element_type=jnp.float32)

    qa_ref[...] = proj(_QA).astype(bf16)
    va_ref[...] = proj(_VA).astype(bf16)
    vb_ref[...] = proj(_VB).astype(bf16)

    t = proj(_QB)
    cos = cos_ref[...]
    sin = sin_ref[...]
    lane = lax.broadcasted_iota(jnp.int32, (PROJ_TM, LANES), 1)
    low_half = (lane % HEAD_DIM) < (HEAD_DIM // 2)
    for c in range(SWA_WIDTH // LANES):
        tc = t[:, c * LANES:(c + 1) * LANES]
        partner = jnp.where(low_half, pltpu.roll(tc, LANES - HEAD_DIM // 2, 1), pltpu.roll(tc, HEAD_DIM // 2, 1))
        qb_ref[:, c * LANES:(c + 1) * LANES] = (tc * cos + partner * sin).astype(bf16)

    z = proj(_Z)
    gate_ref[...] = (z * jax.nn.sigmoid(z)).astype(bf16)

    kt = lax.dot_general(wkt_ref[...], xb, (((1,), (1,)), ((), ())), preferred_element_type=jnp.float32)
    for j in range(PROJ_TM // LANES):
        kta_ref[0, j] = kt[0:NA_WIDTH, j * LANES:(j + 1) * LANES].astype(bf16)
    cos_t = cost_ref[...]
    sin_t = sint_ref[...]
    half = HEAD_DIM // 2
    parts = []
    for u in range(2 * SWA_KV_HEADS):
        blk = kt[NA_WIDTH + u * HEAD_DIM:NA_WIDTH + (u + 1) * HEAD_DIM]
        partner = jnp.concatenate([blk[half:], blk[:half]], axis=0)
        parts.append(blk * cos_t + partner * sin_t)
    kb = jnp.concatenate(parts, axis=0).astype(bf16)
    for j in range(PROJ_TM // LANES):
        ktb_ref[0, j] = kb[:, j * LANES:(j + 1) * LANES]


def _attn_kernel(idx_ref, sink_ref,
                 qa_ref, kta_ref, va_ref, qb_ref, ktb_ref, vb_ref, gate_ref, x_ref, wout_ref,
                 tbl_ref, band_ref, gain_ref, beta_ref, out_ref, woutb_ref, *, rows, n_blocks):
    f32, bf16 = jnp.float32, jnp.bfloat16
    t = pl.program_id(1)

    @pl.when((pl.program_id(0) == 0) & (t == 0))
    def _():
        woutb_ref[...] = wout_ref[...].astype(bf16)

    lane = lax.broadcasted_iota(jnp.int32, (TQ, LANES), 1)
    low = lane < HEAD_DIM

    ks = jnp.clip(Q_ROWS * t - min(NA_KH, rows) // 2, 0, rows - NA_KEY_ROWS)
    kblk = ks // 2
    key0 = pl.multiple_of(ks * GRID_W, LANES)
    jl = jnp.maximum(t - 1, 0)
    jr = jnp.minimum(t + 1, n_blocks - 1)
    variant = jnp.where(t == 0, 0, jnp.where(t == n_blocks - 1, 2, 1))

    na_s, na_v = [], []
    for p in range(NA_HEADS // 2):
        cols = slice(p * LANES, (p + 1) * LANES)
        qp = qa_ref[:, cols]
        zero = jnp.zeros_like(qp)
        kt = jnp.concatenate([kta_ref[0, kblk + j, cols, :] for j in range(KEY_PAIRS)], axis=1)
        na_v.append(va_ref[0, pl.ds(key0, NA_KEYS), cols])
        for par in range(2):
            qm = jnp.where(low, qp, zero) if par == 0 else jnp.where(low, zero, qp)
            na_s.append(jnp.dot(qm, kt, preferred_element_type=f32))
    sw_s, sw_v = [], []
    for g in range(SWA_KV_HEADS):
        gcols = slice(g * LANES, (g + 1) * LANES)
        kt = jnp.concatenate([ktb_ref[0, j, gcols, :] for j in (jl, t, jr)], axis=1)
        sw_v.append(jnp.concatenate(
            [vb_ref[0, pl.ds(pl.multiple_of(j * TQ, TQ), TQ), gcols] for j in (jl, t, jr)], axis=0))
        qs = []
        for pp in range(SWA_GROUP // 2):
            qp = qb_ref[:, (2 * g + pp) * LANES:(2 * g + pp + 1) * LANES]
            zero = jnp.zeros_like(qp)
            qs += [jnp.where(low, qp, zero), jnp.where(low, zero, qp)]
        q4 = jnp.concatenate(qs, axis=0)
        sw_s.append(jnp.dot(q4, kt, preferred_element_type=f32))

    na_e, na_l = [], []
    for h in range(NA_HEADS):
        bias = jnp.concatenate(
            [jnp.concatenate([tbl_ref[idx_ref[t * (Q_ROWS * KEY_PAIRS) + qrow * KEY_PAIRS + j], h]
                              for j in range(KEY_PAIRS)], axis=1)
             for qrow in range(Q_ROWS)], axis=0)
        s = na_s[h] + bias
        m = jnp.max(s, axis=-1, keepdims=True)
        e = jnp.exp(s - m)
        na_l.append(jnp.sum(e, axis=-1, keepdims=True))
        na_e.append(e.astype(bf16))
    band = band_ref[variant]
    band4 = jnp.concatenate([band] * SWA_GROUP, axis=0)
    sw_e, sw_l = [], []
    for g in range(SWA_KV_HEADS):
        s = sw_s[g] + band4
        sink = jnp.concatenate([jnp.full((TQ, 1), sink_ref[SWA_GROUP * g + j], f32) for j in range(SWA_GROUP)],
                               axis=0)
        m = jnp.maximum(jnp.max(s, axis=-1, keepdims=True), sink)
        e = jnp.exp(s - m)
        sw_l.append(jnp.sum(e, axis=-1, keepdims=True) + jnp.exp(sink - m))
        sw_e.append(e.astype(bf16))

    mixed_cols = []
    for p in range(NA_HEADS // 2):
        outs = [jnp.dot(na_e[2 * p + par], na_v[p], preferred_element_type=f32) * (1.0 / na_l[2 * p + par])
                for par in range(2)]
        mixed_cols.append(jnp.where(low, outs[0], outs[1]))
    yb_cols = [None] * (SWA_Q_HEADS // 2)
    for g in range(SWA_KV_HEADS):
        o = jnp.dot(sw_e[g], sw_v[g], preferred_element_type=f32) * (1.0 / sw_l[g])
        for pp in range(SWA_GROUP // 2):
            yb_cols[2 * g + pp] = jnp.where(low, o[(2 * pp) * TQ:(2 * pp + 1) * TQ],
                                            o[(2 * pp + 1) * TQ:(2 * pp + 2) * TQ])

    mixed = jnp.concatenate(mixed_cols + yb_cols, axis=1)
    mixed = (mixed * gate_ref[...].astype(f32)).astype(bf16)
    y = jnp.dot(mixed, woutb_ref[...], preferred_element_type=f32)
    r = DEEPNORM_ALPHA * x_ref[...] + y
    mu = jnp.mean(r, axis=-1, keepdims=True)
    d = r - mu
    var = jnp.mean(d * d, axis=-1, keepdims=True)
    out_ref[...] = d * lax.rsqrt(var + LN_EPS) * gain_ref[...] + beta_ref[...]


def kernel(x, w_in, rel_pos_bias, sink_logits, w_out, ln_gain, ln_bias):
    B, S, D = x.shape
    assert D == D_MODEL and S % PROJ_TM == 0 and S % GRID_W == 0
    rows = S // GRID_W
    n_blocks = S // TQ
    n_tok = B * S
    bf16 = jnp.bfloat16
    x2 = x.reshape(n_tok, D)

    cos_tok, sin_tok, cos_feat, sin_feat = _rope_tables(S)
    entries, idx_np = _na_tile_plan(rows)
    band = jnp.asarray(_swa_band_tiles())

    for layer in range(DEPTH):
        tiles_per_seq = S // PROJ_TM
        blk128 = PROJ_TM // LANES
        qa, va, qb, vb, gate, kta, ktb = pl.pallas_call(
            _proj_kernel,
            grid=(n_tok // PROJ_TM,),
            in_specs=[
                pl.BlockSpec((PROJ_TM, D), lambda i: (i, 0)),
                pl.BlockSpec((D, IN_WIDTH), lambda i: (0, 0), pipeline_mode=pl.Buffered(1)),
                pl.BlockSpec((PROJ_TM, LANES), lambda i: (i % tiles_per_seq, 0)),
                pl.BlockSpec((PROJ_TM, LANES), lambda i: (i % tiles_per_seq, 0)),
                pl.BlockSpec((HEAD_DIM, PROJ_TM), lambda i: (0, i % tiles_per_seq)),
                pl.BlockSpec((HEAD_DIM, PROJ_TM), lambda i: (0, i % tiles_per_seq)),
            ],
            out_specs=[
                pl.BlockSpec((PROJ_TM, NA_WIDTH), lambda i: (i, 0)),
                pl.BlockSpec((PROJ_TM, NA_WIDTH), lambda i: (i, 0)),
                pl.BlockSpec((PROJ_TM, SWA_WIDTH), lambda i: (i, 0)),
                pl.BlockSpec((PROJ_TM, 2 * SWA_KV_WIDTH), lambda i: (i, 0)),
                pl.BlockSpec((PROJ_TM, MIX_WIDTH), lambda i: (i, 0)),
                pl.BlockSpec((1, blk128, NA_WIDTH, LANES), lambda i: (i // tiles_per_seq, i % tiles_per_seq, 0, 0)),
                pl.BlockSpec((1, blk128, 2 * SWA_KV_WIDTH, LANES),
                             lambda i: (i // tiles_per_seq, i % tiles_per_seq, 0, 0)),
            ],
            out_shape=[
                jax.ShapeDtypeStruct((n_tok, NA_WIDTH), bf16),
                jax.ShapeDtypeStruct((n_tok, NA_WIDTH), bf16),
                jax.ShapeDtypeStruct((n_tok, SWA_WIDTH), bf16),
                jax.ShapeDtypeStruct((n_tok, 2 * SWA_KV_WIDTH), bf16),
                jax.ShapeDtypeStruct((n_tok, MIX_WIDTH), bf16),
                jax.ShapeDtypeStruct((B, S // LANES, NA_WIDTH, LANES), bf16),
                jax.ShapeDtypeStruct((B, S // LANES, 2 * SWA_KV_WIDTH, LANES), bf16),
            ],
            scratch_shapes=[pltpu.VMEM((D, TOK_WIDTH), bf16), pltpu.VMEM((KT_ROWS, D), bf16)],
            compiler_params=pltpu.CompilerParams(
                dimension_semantics=("arbitrary",), vmem_limit_bytes=56 * 1024 * 1024),
            name="in_proj",
        )(x2, w_in[layer], cos_tok, sin_tok, cos_feat, sin_feat)

        tbl = _na_bias_tiles(rel_pos_bias[layer], entries)
        idx = jnp.asarray(idx_np.reshape(-1))
        n_ent = len(entries)
        grid_spec = pltpu.PrefetchScalarGridSpec(
            num_scalar_prefetch=2,
            grid=(B, n_blocks),
            in_specs=[
                pl.BlockSpec((TQ, NA_WIDTH), lambda b, t, *_: (b * n_blocks + t, 0)),
                pl.BlockSpec((1, S // LANES, NA_WIDTH, LANES), lambda b, t, *_: (b, 0, 0, 0)),
                pl.BlockSpec((1, S, NA_WIDTH), lambda b, t, *_: (b, 0, 0)),
                pl.BlockSpec((TQ, SWA_WIDTH), lambda b, t, *_: (b * n_blocks + t, 0)),
                pl.BlockSpec((1, S // LANES, 2 * SWA_KV_WIDTH, LANES), lambda b, t, *_: (b, 0, 0, 0)),
                pl.BlockSpec((1, S, 2 * SWA_KV_WIDTH), lambda b, t, *_: (b, 0, 0)),
                pl.BlockSpec((TQ, MIX_WIDTH), lambda b, t, *_: (b * n_blocks + t, 0)),
                pl.BlockSpec((TQ, D), lambda b, t, *_: (b * n_blocks + t, 0)),
                pl.BlockSpec((MIX_WIDTH, D), lambda b, t, *_: (0, 0), pipeline_mode=pl.Buffered(1)),
                pl.BlockSpec((n_ent, NA_HEADS, GRID_W, 2 * GRID_W), lambda b, t, *_: (0, 0, 0, 0)),
                pl.BlockSpec((3, TQ, SWA_KEYS), lambda b, t, *_: (0, 0, 0)),
                pl.BlockSpec((1, D), lambda b, t, *_: (0, 0)),
                pl.BlockSpec((1, D), lambda b, t, *_: (0, 0)),
            ],
            out_specs=pl.BlockSpec((TQ, D), lambda b, t, *_: (b * n_blocks + t, 0)),
            scratch_shapes=[pltpu.VMEM((MIX_WIDTH, D), bf16)],
        )
        x2 = pl.pallas_call(
            functools.partial(_attn_kernel, rows=rows, n_blocks=n_blocks),
            grid_spec=grid_spec,
            out_shape=jax.ShapeDtypeStruct((n_tok, D), jnp.float32),
            compiler_params=pltpu.CompilerParams(
                dimension_semantics=("arbitrary", "arbitrary"), vmem_limit_bytes=56 * 1024 * 1024),
            name="attn_out",
        )(idx, sink_logits[layer].astype(jnp.float32),
          qa, kta, va.reshape(B, S, NA_WIDTH), qb, ktb, vb.reshape(B, S, 2 * SWA_KV_WIDTH), gate, x2,
          w_out[layer], tbl, band, ln_gain[layer].reshape(1, D), ln_bias[layer].reshape(1, D))
    return x2.reshape(B, S, D)
```

```python
import functools

import numpy as np
import jax
import jax.numpy as jnp
from jax import lax
from jax.experimental import pallas as pl
from jax.experimental.pallas import tpu as pltpu

D_MODEL = 1024
HEAD_DIM = 64
NA_HEADS = 8
NA_WIDTH = NA_HEADS * HEAD_DIM
NA_KH = 8
NA_KW = 16
GRID_W = 64
SWA_Q_HEADS = 8
SWA_KV_HEADS = 2
SWA_GROUP = SWA_Q_HEADS // SWA_KV_HEADS
SWA_WIDTH = SWA_Q_HEADS * HEAD_DIM
SWA_KV_WIDTH = SWA_KV_HEADS * HEAD_DIM
SWA_WINDOW = 128
ROPE_THETA = 10000.0
MIX_WIDTH = NA_WIDTH + SWA_WIDTH
LN_EPS = 1e-5
MASK_VALUE = -1e30
DEPTH = 1
DEEPNORM_ALPHA = (2.0 * DEPTH) ** 0.25

LANES = 128
PAIR = 2 * HEAD_DIM
assert PAIR == LANES

PROJ_TM = 512
TQ = 128
Q_ROWS = TQ // GRID_W
NA_KEY_ROWS = 10
NA_KEYS = NA_KEY_ROWS * GRID_W
KEY_PAIRS = NA_KEYS // LANES
SWA_KEYS = 3 * TQ

_QA = (0, NA_WIDTH)
_VA = (_QA[1], _QA[1] + NA_WIDTH)
_QB = (_VA[1], _VA[1] + SWA_WIDTH)
_VB = (_QB[1], _QB[1] + 2 * SWA_KV_WIDTH)
_Z = (_VB[1], _VB[1] + MIX_WIDTH)
TOK_WIDTH = _Z[1]
_W_QA = 0
_W_KA = _W_QA + NA_WIDTH
_W_VA = _W_KA + NA_WIDTH
_W_ZA = _W_VA + NA_WIDTH
_W_QB = _W_ZA + NA_WIDTH
_W_KB = _W_QB + SWA_WIDTH
_W_VB = _W_KB + SWA_KV_WIDTH
_W_ZB = _W_VB + SWA_KV_WIDTH
IN_WIDTH = _W_ZB + SWA_WIDTH
KT_ROWS = NA_WIDTH + 2 * SWA_KV_WIDTH


def _na_tile_plan(rows):
    kh = min(NA_KH, rows)
    entries, lookup = [], {}
    n_steps = rows // Q_ROWS
    idx = np.zeros((n_steps, Q_ROWS * KEY_PAIRS), np.int32)
    for t in range(n_steps):
        ks = int(np.clip(Q_ROWS * t - kh // 2, 0, rows - NA_KEY_ROWS))
        for qrow in range(Q_ROWS):
            r = Q_ROWS * t + qrow
            r0 = int(np.clip(r - kh // 2, 0, rows - kh))
            assert ks <= r0 and r0 + kh <= ks + NA_KEY_ROWS
            for j in range(KEY_PAIRS):
                key = []
                for i in (2 * j, 2 * j + 1):
                    kr = ks + i
                    key.append(kr - r + (NA_KH - 1) if r0 <= kr < r0 + kh else -1)
                key = tuple(key)
                if key not in lookup:
                    lookup[key] = len(entries)
                    entries.append(key)
                idx[t, qrow * KEY_PAIRS + j] = lookup[key]
    return entries, idx


def _build_na_bias_tiles(rel_ref, tbl_ref, entries):
    n_dr = 2 * NA_KH - 1
    c = lax.broadcasted_iota(jnp.int32, (GRID_W, LANES), 0)
    lane = lax.broadcasted_iota(jnp.int32, (GRID_W, LANES), 1)
    kc = lane & (GRID_W - 1)
    start = jnp.clip(c - NA_KW // 2, 0, GRID_W - NA_KW)
    in_win = (kc >= start) & (kc < start + NA_KW)
    low = lane < GRID_W
    masked = jnp.full((GRID_W, LANES), MASK_VALUE, jnp.float32)
    for h in range(NA_HEADS):
        halves = {}

        def half(dr, upper):
            if dr < 0:
                return masked
            if (dr, upper) not in halves:
                row = jnp.broadcast_to(rel_ref[pl.ds(h * n_dr + dr, 1), :], (GRID_W, LANES))
                shift = (LANES - (NA_KW - 1) + (GRID_W if upper else 0)) % LANES
                halves[(dr, upper)] = pltpu.roll(row, shift, 1, stride=1, stride_axis=0)
            return halves[(dr, upper)]

        for e, (dr_a, dr_b) in enumerate(entries):
            val = jnp.where(low, half(dr_a, False), half(dr_b, True))
            tbl_ref[e, h] = jnp.where(in_win, val, masked)


def _swa_band_tiles():
    q = np.arange(TQ)[:, None]
    k = np.arange(SWA_KEYS)[None, :] - TQ
    band = np.abs(k - q) <= SWA_WINDOW
    left_ok = np.concatenate([np.zeros((1, TQ), bool), np.ones((1, 2 * TQ), bool)], axis=1)
    right_ok = np.concatenate([np.ones((1, 2 * TQ), bool), np.zeros((1, TQ), bool)], axis=1)
    variants = np.stack([band & left_ok, band, band & right_ok])
    return np.where(variants, 0.0, MASK_VALUE).astype(np.float32)


def _rope_tables(seq):
    inv_freq = ROPE_THETA ** (-jnp.arange(0, HEAD_DIM, 2, dtype=jnp.float32) / HEAD_DIM)
    ang = jnp.arange(seq, dtype=jnp.int32).astype(jnp.float32)[:, None] * inv_freq[None, :]
    cos = jnp.cos(ang)
    sin = jnp.sin(ang)
    cos_head = jnp.concatenate([cos, cos], axis=1)
    sin_head = jnp.concatenate([-sin, sin], axis=1)
    cos_tok = jnp.concatenate([cos_head, cos_head], axis=1)
    sin_tok = jnp.concatenate([sin_head, sin_head], axis=1)
    return cos_tok, sin_tok, cos_head.T, sin_head.T


def _build_proj_weights(win_ref, wtok_ref, wkt_ref):
    bf16 = jnp.bfloat16
    scale = HEAD_DIM ** -0.5
    wtok_ref[:, _QA[0]:_QA[1]] = (win_ref[:, _W_QA:_W_QA + NA_WIDTH] * scale).astype(bf16)
    wtok_ref[:, _VA[0]:_VA[1]] = win_ref[:, _W_VA:_W_VA + NA_WIDTH].astype(bf16)
    wtok_ref[:, _QB[0]:_QB[1]] = (win_ref[:, _W_QB:_W_QB + SWA_WIDTH] * scale).astype(bf16)
    wtok_ref[:, _Z[0]:_Z[0] + NA_WIDTH] = win_ref[:, _W_ZA:_W_ZA + NA_WIDTH].astype(bf16)
    wtok_ref[:, _Z[0] + NA_WIDTH:_Z[1]] = win_ref[:, _W_ZB:_W_ZB + SWA_WIDTH].astype(bf16)
    vpair = win_ref[:, _W_VB:_W_VB + SWA_KV_WIDTH]
    swapped = pltpu.roll(vpair, HEAD_DIM, 1)
    low = lax.broadcasted_iota(jnp.int32, vpair.shape, 1) < HEAD_DIM
    wtok_ref[:, _VB[0]:_VB[0] + LANES] = jnp.where(low, vpair, swapped).astype(bf16)
    wtok_ref[:, _VB[0] + LANES:_VB[1]] = jnp.where(low, swapped, vpair).astype(bf16)
    for c in range(NA_WIDTH // LANES):
        wkt_ref[c * LANES:(c + 1) * LANES, :] = win_ref[:, _W_KA + c * LANES:_W_KA + (c + 1) * LANES].T.astype(bf16)
    kb_t = win_ref[:, _W_KB:_W_KB + SWA_KV_WIDTH].T.astype(bf16)
    for g in range(SWA_KV_HEADS):
        for rep in range(2):
            r0 = NA_WIDTH + (2 * g + rep) * HEAD_DIM
            wkt_ref[r0:r0 + HEAD_DIM, :] = kb_t[g * HEAD_DIM:(g + 1) * HEAD_DIM]


def _proj_kernel(x_ref, win_ref, cos_ref, sin_ref, cost_ref, sint_ref,
                 qa_ref, va_ref, qb_ref, vb_ref, gate_ref, kta_ref, ktb_ref, wtok_ref, wkt_ref):
    bf16 = jnp.bfloat16

    @pl.when(pl.program_id(0) == 0)
    def _():
        _build_proj_weights(win_ref, wtok_ref, wkt_ref)

    xb = x_ref[...].astype(bf16)

    def proj(span):
        return jnp.dot(xb, wtok_ref[:, span[0]:span[1]], preferred_element_type=jnp.float32)

    qa_ref[...] = proj(_QA).astype(bf16)
    va_ref[...] = proj(_VA).astype(bf16)
    vb_ref[...] = proj(_VB).astype(bf16)

    t = proj(_QB)
    cos = cos_ref[...]
    sin = sin_ref[...]
    lane = lax.broadcasted_iota(jnp.int32, (PROJ_TM, LANES), 1)
    low_half = (lane % HEAD_DIM) < (HEAD_DIM // 2)
    for c in range(SWA_WIDTH // LANES):
        tc = t[:, c * LANES:(c + 1) * LANES]
        partner = jnp.where(low_half, pltpu.roll(tc, LANES - HEAD_DIM // 2, 1), pltpu.roll(tc, HEAD_DIM // 2, 1))
        qb_ref[:, c * LANES:(c + 1) * LANES] = (tc * cos + partner * sin).astype(bf16)

    z = proj(_Z)
    gate_ref[...] = (z * jax.nn.sigmoid(z)).astype(bf16)

    kt = lax.dot_general(wkt_ref[...], xb, (((1,), (1,)), ((), ())), preferred_element_type=jnp.float32)
    for j in range(PROJ_TM // LANES):
        kta_ref[0, j] = kt[0:NA_WIDTH, j * LANES:(j + 1) * LANES].astype(bf16)
    cos_t = cost_ref[...]
    sin_t = sint_ref[...]
    half = HEAD_DIM // 2
    parts = []
    for u in range(2 * SWA_KV_HEADS):
        blk = kt[NA_WIDTH + u * HEAD_DIM:NA_WIDTH + (u + 1) * HEAD_DIM]
        partner = jnp.concatenate([blk[half:], blk[:half]], axis=0)
        parts.append(blk * cos_t + partner * sin_t)
    kb = jnp.concatenate(parts, axis=0).astype(bf16)
    for j in range(PROJ_TM // LANES):
        ktb_ref[0, j] = kb[:, j * LANES:(j + 1) * LANES]


def _attn_kernel(idx_ref, sink_ref,
                 qa_ref, kta_ref, va_ref, qb_ref, ktb_ref, vb_ref, gate_ref, x_ref, wout_ref,
                 rel_ref, band_ref, gain_ref, beta_ref, out_ref, woutb_ref, tbl_ref, *, rows, n_blocks, entries):
    f32, bf16 = jnp.float32, jnp.bfloat16
    t = pl.program_id(1)

    @pl.when((pl.program_id(0) == 0) & (t == 0))
    def _():
        woutb_ref[...] = wout_ref[...].astype(bf16)
        _build_na_bias_tiles(rel_ref, tbl_ref, entries)

    lane = lax.broadcasted_iota(jnp.int32, (TQ, LANES), 1)
    low = lane < HEAD_DIM

    ks = jnp.clip(Q_ROWS * t - min(NA_KH, rows) // 2, 0, rows - NA_KEY_ROWS)
    kblk = ks // 2
    key0 = pl.multiple_of(ks * GRID_W, LANES)
    jl = jnp.maximum(t - 1, 0)
    jr = jnp.minimum(t + 1, n_blocks - 1)
    variant = jnp.where(t == 0, 0, jnp.where(t == n_blocks - 1, 2, 1))

    na_s, na_v = [], []
    for p in range(NA_HEADS // 2):
        cols = slice(p * LANES, (p + 1) * LANES)
        qp = qa_ref[:, cols]
        zero = jnp.zeros_like(qp)
        kt = jnp.concatenate([kta_ref[0, kblk + j, cols, :] for j in range(KEY_PAIRS)], axis=1)
        na_v.append(va_ref[0, pl.ds(key0, NA_KEYS), cols])
        for par in range(2):
            qm = jnp.where(low, qp, zero) if par == 0 else jnp.where(low, zero, qp)
            na_s.append(jnp.dot(qm, kt, preferred_element_type=f32))
    sw_s, sw_v = [], []
    for g in range(SWA_KV_HEADS):
        gcols = slice(g * LANES, (g + 1) * LANES)
        kt = jnp.concatenate([ktb_ref[0, j, gcols, :] for j in (jl, t, jr)], axis=1)
        sw_v.append(jnp.concatenate(
            [vb_ref[0, pl.ds(pl.multiple_of(j * TQ, TQ), TQ), gcols] for j in (jl, t, jr)], axis=0))
        qs = []
        for pp in range(SWA_GROUP // 2):
            qp = qb_ref[:, (2 * g + pp) * LANES:(2 * g + pp + 1) * LANES]
            zero = jnp.zeros_like(qp)
            qs += [jnp.where(low, qp, zero), jnp.where(low, zero, qp)]
        q4 = jnp.concatenate(qs, axis=0)
        sw_s.append(jnp.dot(q4, kt, preferred_element_type=f32))

    na_e, na_l = [], []
    for h in range(NA_HEADS):
        bias = jnp.concatenate(
            [jnp.concatenate([tbl_ref[idx_ref[t * (Q_ROWS * KEY_PAIRS) + qrow * KEY_PAIRS + j], h]
                              for j in range(KEY_PAIRS)], axis=1)
             for qrow in range(Q_ROWS)], axis=0)
        s = na_s[h] + bias
        m = jnp.max(s, axis=-1, keepdims=True)
        e = jnp.exp(s - m)
        na_l.append(jnp.sum(e, axis=-1, keepdims=True))
        na_e.append(e.astype(bf16))
    band = band_ref[variant]
    band4 = jnp.concatenate([band] * SWA_GROUP, axis=0)
    sw_e, sw_l = [], []
    for g in range(SWA_KV_HEADS):
        s = sw_s[g] + band4
        sink = jnp.concatenate([jnp.full((TQ, 1), sink_ref[SWA_GROUP * g + j], f32) for j in range(SWA_GROUP)],
                               axis=0)
        m = jnp.maximum(jnp.max(s, axis=-1, keepdims=True), sink)
        e = jnp.exp(s - m)
        sw_l.append(jnp.sum(e, axis=-1, keepdims=True) + jnp.exp(sink - m))
        sw_e.append(e.astype(bf16))

    mixed_cols = []
    for p in range(NA_HEADS // 2):
        outs = [jnp.dot(na_e[2 * p + par], na_v[p], preferred_element_type=f32) * (1.0 / na_l[2 * p + par])
                for par in range(2)]
        mixed_cols.append(jnp.where(low, outs[0], outs[1]))
    yb_cols = [None] * (SWA_Q_HEADS // 2)
    for g in range(SWA_KV_HEADS):
        o = jnp.dot(sw_e[g], sw_v[g], preferred_element_type=f32) * (1.0 / sw_l[g])
        for pp in range(SWA_GROUP // 2):
            yb_cols[2 * g + pp] = jnp.where(low, o[(2 * pp) * TQ:(2 * pp + 1) * TQ],
                                            o[(2 * pp + 1) * TQ:(2 * pp + 2) * TQ])

    mixed = jnp.concatenate(mixed_cols + yb_cols, axis=1)
    mixed = (mixed * gate_ref[...].astype(f32)).astype(bf16)
    y = jnp.dot(mixed, woutb_ref[...], preferred_element_type=f32)
    r = DEEPNORM_ALPHA * x_ref[...] + y
    mu = jnp.mean(r, axis=-1, keepdims=True)
    d = r - mu
    var = jnp.mean(d * d, axis=-1, keepdims=True)
    out_ref[...] = d * lax.rsqrt(var + LN_EPS) * gain_ref[...] + beta_ref[...]


def kernel(x, w_in, rel_pos_bias, sink_logits, w_out, ln_gain, ln_bias):
    B, S, D = x.shape
    assert D == D_MODEL and S % PROJ_TM == 0 and S % GRID_W == 0
    rows = S // GRID_W
    n_blocks = S // TQ
    n_tok = B * S
    bf16 = jnp.bfloat16
    x2 = x.reshape(n_tok, D)

    cos_tok, sin_tok, cos_feat, sin_feat = _rope_tables(S)
    entries, idx_np = _na_tile_plan(rows)
    band = jnp.asarray(_swa_band_tiles())

    for layer in range(DEPTH):
        tiles_per_seq = S // PROJ_TM
        blk128 = PROJ_TM // LANES
        qa, va, qb, vb, gate, kta, ktb = pl.pallas_call(
            _proj_kernel,
            grid=(n_tok // PROJ_TM,),
            in_specs=[
                pl.BlockSpec((PROJ_TM, D), lambda i: (i, 0)),
                pl.BlockSpec((D, IN_WIDTH), lambda i: (0, 0), pipeline_mode=pl.Buffered(1)),
                pl.BlockSpec((PROJ_TM, LANES), lambda i: (i % tiles_per_seq, 0)),
                pl.BlockSpec((PROJ_TM, LANES), lambda i: (i % tiles_per_seq, 0)),
                pl.BlockSpec((HEAD_DIM, PROJ_TM), lambda i: (0, i % tiles_per_seq)),
                pl.BlockSpec((HEAD_DIM, PROJ_TM), lambda i: (0, i % tiles_per_seq)),
            ],
            out_specs=[
                pl.BlockSpec((PROJ_TM, NA_WIDTH), lambda i: (i, 0)),
                pl.BlockSpec((PROJ_TM, NA_WIDTH), lambda i: (i, 0)),
                pl.BlockSpec((PROJ_TM, SWA_WIDTH), lambda i: (i, 0)),
                pl.BlockSpec((PROJ_TM, 2 * SWA_KV_WIDTH), lambda i: (i, 0)),
                pl.BlockSpec((PROJ_TM, MIX_WIDTH), lambda i: (i, 0)),
                pl.BlockSpec((1, blk128, NA_WIDTH, LANES), lambda i: (i // tiles_per_seq, i % tiles_per_seq, 0, 0)),
                pl.BlockSpec((1, blk128, 2 * SWA_KV_WIDTH, LANES),
                             lambda i: (i // tiles_per_seq, i % tiles_per_seq, 0, 0)),
            ],
            out_shape=[
                jax.ShapeDtypeStruct((n_tok, NA_WIDTH), bf16),
                jax.ShapeDtypeStruct((n_tok, NA_WIDTH), bf16),
                jax.ShapeDtypeStruct((n_tok, SWA_WIDTH), bf16),
                jax.ShapeDtypeStruct((n_tok, 2 * SWA_KV_WIDTH), bf16),
                jax.ShapeDtypeStruct((n_tok, MIX_WIDTH), bf16),
                jax.ShapeDtypeStruct((B, S // LANES, NA_WIDTH, LANES), bf16),
                jax.ShapeDtypeStruct((B, S // LANES, 2 * SWA_KV_WIDTH, LANES), bf16),
            ],
            scratch_shapes=[pltpu.VMEM((D, TOK_WIDTH), bf16), pltpu.VMEM((KT_ROWS, D), bf16)],
            compiler_params=pltpu.CompilerParams(
                dimension_semantics=("arbitrary",), vmem_limit_bytes=56 * 1024 * 1024),
            name="in_proj",
        )(x2, w_in[layer], cos_tok, sin_tok, cos_feat, sin_feat)

        n_dr, n_dc = 2 * NA_KH - 1, 2 * NA_KW - 1
        rel_rows = jnp.pad(rel_pos_bias[layer].astype(jnp.float32).reshape(NA_HEADS * n_dr, n_dc),
                           ((0, 0), (0, LANES - n_dc)))
        idx = jnp.asarray(idx_np.reshape(-1))
        n_ent = len(entries)
        grid_spec = pltpu.PrefetchScalarGridSpec(
            num_scalar_prefetch=2,
            grid=(B, n_blocks),
            in_specs=[
                pl.BlockSpec((TQ, NA_WIDTH), lambda b, t, *_: (b * n_blocks + t, 0)),
                pl.BlockSpec((1, S // LANES, NA_WIDTH, LANES), lambda b, t, *_: (b, 0, 0, 0)),
                pl.BlockSpec((1, S, NA_WIDTH), lambda b, t, *_: (b, 0, 0)),
                pl.BlockSpec((TQ, SWA_WIDTH), lambda b, t, *_: (b * n_blocks + t, 0)),
                pl.BlockSpec((1, S // LANES, 2 * SWA_KV_WIDTH, LANES), lambda b, t, *_: (b, 0, 0, 0)),
                pl.BlockSpec((1, S, 2 * SWA_KV_WIDTH), lambda b, t, *_: (b, 0, 0)),
                pl.BlockSpec((TQ, MIX_WIDTH), lambda b, t, *_: (b * n_blocks + t, 0)),
                pl.BlockSpec((TQ, D), lambda b, t, *_: (b * n_blocks + t, 0)),
                pl.BlockSpec((MIX_WIDTH, D), lambda b, t, *_: (0, 0), pipeline_mode=pl.Buffered(1)),
                pl.BlockSpec((NA_HEADS * n_dr, LANES), lambda b, t, *_: (0, 0)),
                pl.BlockSpec((3, TQ, SWA_KEYS), lambda b, t, *_: (0, 0, 0)),
                pl.BlockSpec((1, D), lambda b, t, *_: (0, 0)),
                pl.BlockSpec((1, D), lambda b, t, *_: (0, 0)),
            ],
            out_specs=pl.BlockSpec((TQ, D), lambda b, t, *_: (b * n_blocks + t, 0)),
            scratch_shapes=[pltpu.VMEM((MIX_WIDTH, D), bf16),
                            pltpu.VMEM((n_ent, NA_HEADS, GRID_W, 2 * GRID_W), jnp.float32)],
        )
        x2 = pl.pallas_call(
            functools.partial(_attn_kernel, rows=rows, n_blocks=n_blocks, entries=tuple(entries)),
            grid_spec=grid_spec,
            out_shape=jax.ShapeDtypeStruct((n_tok, D), jnp.float32),
            compiler_params=pltpu.CompilerParams(
                dimension_semantics=("arbitrary", "arbitrary"), vmem_limit_bytes=56 * 1024 * 1024),
            name="attn_out",
        )(idx, sink_logits[layer].astype(jnp.float32),
          qa, kta, va.reshape(B, S, NA_WIDTH), qb, ktb, vb.reshape(B, S, 2 * SWA_KV_WIDTH), gate, x2,
          w_out[layer], rel_rows, band, ln_gain[layer].reshape(1, D), ln_bias[layer].reshape(1, D))
    return x2.reshape(B, S, D)
```

```python
import functools

import numpy as np
import jax
import jax.numpy as jnp
from jax import lax
from jax.experimental import pallas as pl
from jax.experimental.pallas import tpu as pltpu

D_MODEL = 1024
HEAD_DIM = 64
NA_HEADS = 8
NA_WIDTH = NA_HEADS * HEAD_DIM
NA_KH = 8
NA_KW = 16
GRID_W = 64
SWA_Q_HEADS = 8
SWA_KV_HEADS = 2
SWA_GROUP = SWA_Q_HEADS // SWA_KV_HEADS
SWA_WIDTH = SWA_Q_HEADS * HEAD_DIM
SWA_KV_WIDTH = SWA_KV_HEADS * HEAD_DIM
SWA_WINDOW = 128
ROPE_THETA = 10000.0
MIX_WIDTH = NA_WIDTH + SWA_WIDTH
LN_EPS = 1e-5
MASK_VALUE = -1e30
DEPTH = 1
DEEPNORM_ALPHA = (2.0 * DEPTH) ** 0.25
LOG2E = 1.4426950408889634
Q_SCALE = HEAD_DIM ** -0.5 * LOG2E

LANES = 128
PAIR = 2 * HEAD_DIM
assert PAIR == LANES

PROJ_TM = 512
TQ = 128
Q_ROWS = TQ // GRID_W
NA_KEY_ROWS = 10
NA_KEYS = NA_KEY_ROWS * GRID_W
KEY_PAIRS = NA_KEYS // LANES
SWA_KEYS = 3 * TQ

_QA = (0, NA_WIDTH)
_VA = (_QA[1], _QA[1] + NA_WIDTH)
_QB = (_VA[1], _VA[1] + SWA_WIDTH)
_VB = (_QB[1], _QB[1] + 2 * SWA_KV_WIDTH)
_Z = (_VB[1], _VB[1] + MIX_WIDTH)
TOK_WIDTH = _Z[1]
_W_QA = 0
_W_KA = _W_QA + NA_WIDTH
_W_VA = _W_KA + NA_WIDTH
_W_ZA = _W_VA + NA_WIDTH
_W_QB = _W_ZA + NA_WIDTH
_W_KB = _W_QB + SWA_WIDTH
_W_VB = _W_KB + SWA_KV_WIDTH
_W_ZB = _W_VB + SWA_KV_WIDTH
IN_WIDTH = _W_ZB + SWA_WIDTH
KT_ROWS = NA_WIDTH + 2 * SWA_KV_WIDTH


def _na_tile_plan(rows):
    kh = min(NA_KH, rows)
    entries, lookup = [], {}
    n_steps = rows // Q_ROWS
    idx = np.zeros((n_steps, Q_ROWS * KEY_PAIRS), np.int32)
    for t in range(n_steps):
        ks = int(np.clip(Q_ROWS * t - kh // 2, 0, rows - NA_KEY_ROWS))
        for qrow in range(Q_ROWS):
            r = Q_ROWS * t + qrow
            r0 = int(np.clip(r - kh // 2, 0, rows - kh))
            assert ks <= r0 and r0 + kh <= ks + NA_KEY_ROWS
            for j in range(KEY_PAIRS):
                key = []
                for i in (2 * j, 2 * j + 1):
                    kr = ks + i
                    key.append(kr - r + (NA_KH - 1) if r0 <= kr < r0 + kh else -1)
                key = tuple(key)
                if key not in lookup:
                    lookup[key] = len(entries)
                    entries.append(key)
                idx[t, qrow * KEY_PAIRS + j] = lookup[key]
    return entries, idx


def _build_na_bias_tiles(rel_ref, tbl_ref, entries):
    n_dr = 2 * NA_KH - 1
    c = lax.broadcasted_iota(jnp.int32, (GRID_W, LANES), 0)
    lane = lax.broadcasted_iota(jnp.int32, (GRID_W, LANES), 1)
    kc = lane & (GRID_W - 1)
    start = jnp.clip(c - NA_KW // 2, 0, GRID_W - NA_KW)
    in_win = (kc >= start) & (kc < start + NA_KW)
    low = lane < GRID_W
    masked = jnp.full((GRID_W, LANES), MASK_VALUE, jnp.float32)
    for h in range(NA_HEADS):
        halves = {}

        def half(dr, upper):
            if dr < 0:
                return masked
            if (dr, upper) not in halves:
                row = jnp.broadcast_to(rel_ref[pl.ds(h * n_dr + dr, 1), :] * LOG2E, (GRID_W, LANES))
                shift = (LANES - (NA_KW - 1) + (GRID_W if upper else 0)) % LANES
                halves[(dr, upper)] = pltpu.roll(row, shift, 1, stride=1, stride_axis=0)
            return halves[(dr, upper)]

        for e, (dr_a, dr_b) in enumerate(entries):
            val = jnp.where(low, half(dr_a, False), half(dr_b, True))
            tbl_ref[e, h] = jnp.where(in_win, val, masked)


def _swa_band_tiles():
    q = np.arange(TQ)[:, None]
    k = np.arange(SWA_KEYS)[None, :] - TQ
    band = np.abs(k - q) <= SWA_WINDOW
    left_ok = np.concatenate([np.zeros((1, TQ), bool), np.ones((1, 2 * TQ), bool)], axis=1)
    right_ok = np.concatenate([np.ones((1, 2 * TQ), bool), np.zeros((1, TQ), bool)], axis=1)
    variants = np.stack([band & left_ok, band, band & right_ok])
    return np.where(variants, 0.0, MASK_VALUE).astype(np.float32)


def _rope_tables(seq):
    inv_freq = ROPE_THETA ** (-jnp.arange(0, HEAD_DIM, 2, dtype=jnp.float32) / HEAD_DIM)
    ang = jnp.arange(seq, dtype=jnp.int32).astype(jnp.float32)[:, None] * inv_freq[None, :]
    cos = jnp.cos(ang)
    sin = jnp.sin(ang)
    cos_head = jnp.concatenate([cos, cos], axis=1)
    sin_head = jnp.concatenate([-sin, sin], axis=1)
    cos_tok = jnp.concatenate([cos_head, cos_head], axis=1) * Q_SCALE
    sin_tok = jnp.concatenate([sin_head, sin_head], axis=1) * Q_SCALE
    return cos_tok, sin_tok, cos_head.T, sin_head.T


def _build_proj_weights(win_ref, wtok_ref, wkt_ref):
    bf16 = jnp.bfloat16
    wtok_ref[:, _QA[0]:_QA[1]] = win_ref[:, _W_QA:_W_QA + NA_WIDTH].astype(bf16)
    wtok_ref[:, _VA[0]:_VA[1]] = win_ref[:, _W_VA:_W_VA + NA_WIDTH].astype(bf16)
    wtok_ref[:, _QB[0]:_QB[1]] = win_ref[:, _W_QB:_W_QB + SWA_WIDTH].astype(bf16)
    wtok_ref[:, _Z[0]:_Z[0] + NA_WIDTH] = win_ref[:, _W_ZA:_W_ZA + NA_WIDTH].astype(bf16)
    wtok_ref[:, _Z[0] + NA_WIDTH:_Z[1]] = win_ref[:, _W_ZB:_W_ZB + SWA_WIDTH].astype(bf16)
    vpair = win_ref[:, _W_VB:_W_VB + SWA_KV_WIDTH]
    swapped = pltpu.roll(vpair, HEAD_DIM, 1)
    low = lax.broadcasted_iota(jnp.int32, vpair.shape, 1) < HEAD_DIM
    wtok_ref[:, _VB[0]:_VB[0] + LANES] = jnp.where(low, vpair, swapped).astype(bf16)
    wtok_ref[:, _VB[0] + LANES:_VB[1]] = jnp.where(low, swapped, vpair).astype(bf16)
    for c in range(NA_WIDTH // LANES):
        wkt_ref[c * LANES:(c + 1) * LANES, :] = win_ref[:, _W_KA + c * LANES:_W_KA + (c + 1) * LANES].T.astype(bf16)
    kb_t = win_ref[:, _W_KB:_W_KB + SWA_KV_WIDTH].T.astype(bf16)
    for g in range(SWA_KV_HEADS):
        for rep in range(2):
            r0 = NA_WIDTH + (2 * g + rep) * HEAD_DIM
            wkt_ref[r0:r0 + HEAD_DIM, :] = kb_t[g * HEAD_DIM:(g + 1) * HEAD_DIM]


def _proj_kernel(x_ref, win_ref, cos_ref, sin_ref, cost_ref, sint_ref,
                 qa_ref, va_ref, qb_ref, vb_ref, gate_ref, kta_ref, ktb_ref, wtok_ref, wkt_ref):
    bf16 = jnp.bfloat16

    @pl.when(pl.program_id(0) == 0)
    def _():
        _build_proj_weights(win_ref, wtok_ref, wkt_ref)

    xb = x_ref[...].astype(bf16)

    def proj(span):
        return jnp.dot(xb, wtok_ref[:, span[0]:span[1]], preferred_element_type=jnp.float32)

    qa_ref[...] = (proj(_QA) * Q_SCALE).astype(bf16)
    va_ref[...] = proj(_VA).astype(bf16)
    vb_ref[...] = proj(_VB).astype(bf16)

    t = proj(_QB)
    cos = cos_ref[...]
    sin = sin_ref[...]
    lane = lax.broadcasted_iota(jnp.int32, (PROJ_TM, LANES), 1)
    low_half = (lane % HEAD_DIM) < (HEAD_DIM // 2)
    for c in range(SWA_WIDTH // LANES):
        tc = t[:, c * LANES:(c + 1) * LANES]
        partner = jnp.where(low_half, pltpu.roll(tc, LANES - HEAD_DIM // 2, 1), pltpu.roll(tc, HEAD_DIM // 2, 1))
        qb_ref[:, c * LANES:(c + 1) * LANES] = (tc * cos + partner * sin).astype(bf16)

    z = proj(_Z)
    gate_ref[...] = (z * jax.nn.sigmoid(z)).astype(bf16)

    kt = lax.dot_general(wkt_ref[...], xb, (((1,), (1,)), ((), ())), preferred_element_type=jnp.float32)
    for j in range(PROJ_TM // LANES):
        kta_ref[0, j] = kt[0:NA_WIDTH, j * LANES:(j + 1) * LANES].astype(bf16)
    cos_t = cost_ref[...]
    sin_t = sint_ref[...]
    half = HEAD_DIM // 2
    parts = []
    for u in range(2 * SWA_KV_HEADS):
        blk = kt[NA_WIDTH + u * HEAD_DIM:NA_WIDTH + (u + 1) * HEAD_DIM]
        partner = jnp.concatenate([blk[half:], blk[:half]], axis=0)
        parts.append(blk * cos_t + partner * sin_t)
    kb = jnp.concatenate(parts, axis=0).astype(bf16)
    for j in range(PROJ_TM // LANES):
        ktb_ref[0, j] = kb[:, j * LANES:(j + 1) * LANES]


def _attn_kernel(idx_ref, sink_ref,
                 qa_ref, kta_ref, va_ref, qb_ref, ktb_ref, vb_ref, gate_ref, x_ref, wout_ref,
                 rel_ref, band_ref, gain_ref, beta_ref, out_ref, woutb_ref, tbl_ref, *, rows, n_blocks, entries):
    f32, bf16 = jnp.float32, jnp.bfloat16
    t = pl.program_id(1)

    @pl.when((pl.program_id(0) == 0) & (t == 0))
    def _():
        woutb_ref[...] = wout_ref[...].astype(bf16)
        _build_na_bias_tiles(rel_ref, tbl_ref, entries)

    lane = lax.broadcasted_iota(jnp.int32, (TQ, LANES), 1)
    low = lane < HEAD_DIM

    ks = jnp.clip(Q_ROWS * t - min(NA_KH, rows) // 2, 0, rows - NA_KEY_ROWS)
    kblk = ks // 2
    key0 = pl.multiple_of(ks * GRID_W, LANES)
    jl = jnp.maximum(t - 1, 0)
    jr = jnp.minimum(t + 1, n_blocks - 1)
    variant = jnp.where(t == 0, 0, jnp.where(t == n_blocks - 1, 2, 1))

    na_s, na_v = [], []
    for p in range(NA_HEADS // 2):
        cols = slice(p * LANES, (p + 1) * LANES)
        qp = qa_ref[:, cols]
        zero = jnp.zeros_like(qp)
        kt = jnp.concatenate([kta_ref[0, kblk + j, cols, :] for j in range(KEY_PAIRS)], axis=1)
        na_v.append(va_ref[0, pl.ds(key0, NA_KEYS), cols])
        for par in range(2):
            qm = jnp.where(low, qp, zero) if par == 0 else jnp.where(low, zero, qp)
            na_s.append(jnp.dot(qm, kt, preferred_element_type=f32))
    sw_s, sw_v = [], []
    for g in range(SWA_KV_HEADS):
        gcols = slice(g * LANES, (g + 1) * LANES)
        kt = jnp.concatenate([ktb_ref[0, j, gcols, :] for j in (jl, t, jr)], axis=1)
        sw_v.append(jnp.concatenate(
            [vb_ref[0, pl.ds(pl.multiple_of(j * TQ, TQ), TQ), gcols] for j in (jl, t, jr)], axis=0))
        qs = []
        for pp in range(SWA_GROUP // 2):
            qp = qb_ref[:, (2 * g + pp) * LANES:(2 * g + pp + 1) * LANES]
            zero = jnp.zeros_like(qp)
            qs += [jnp.where(low, qp, zero), jnp.where(low, zero, qp)]
        q4 = jnp.concatenate(qs, axis=0)
        sw_s.append(jnp.dot(q4, kt, preferred_element_type=f32))

    na_e = []
    for h in range(NA_HEADS):
        bias = jnp.concatenate(
            [jnp.concatenate([tbl_ref[idx_ref[t * (Q_ROWS * KEY_PAIRS) + qrow * KEY_PAIRS + j], h]
                              for j in range(KEY_PAIRS)], axis=1)
             for qrow in range(Q_ROWS)], axis=0)
        s = na_s[h] + bias
        m = jnp.max(s, axis=-1, keepdims=True)
        na_e.append(jnp.exp2(s - m).astype(bf16))
    band = band_ref[variant]
    band4 = jnp.concatenate([band] * SWA_GROUP, axis=0)
    sw_e, sw_sink = [], []
    for g in range(SWA_KV_HEADS):
        s = sw_s[g] + band4
        sink = jnp.concatenate(
            [jnp.full((TQ, 1), sink_ref[SWA_GROUP * g + j] * LOG2E, f32) for j in range(SWA_GROUP)], axis=0)
        m = jnp.maximum(jnp.max(s, axis=-1, keepdims=True), sink)
        sw_e.append(jnp.exp2(s - m).astype(bf16))
        sw_sink.append(jnp.exp2(sink - m))

    mixed_cols = []
    for p in range(NA_HEADS // 2):
        v1 = jnp.concatenate([na_v[p], jnp.ones_like(na_v[p])], axis=1)
        outs = []
        for par in range(2):
            o = jnp.dot(na_e[2 * p + par], v1, preferred_element_type=f32)
            outs.append(o[:, :LANES] * (1.0 / o[:, LANES:]))
        mixed_cols.append(jnp.where(low, outs[0], outs[1]))
    yb_cols = [None] * (SWA_Q_HEADS // 2)
    for g in range(SWA_KV_HEADS):
        v1 = jnp.concatenate([sw_v[g], jnp.ones_like(sw_v[g])], axis=1)
        o = jnp.dot(sw_e[g], v1, preferred_element_type=f32)
        o = o[:, :LANES] * (1.0 / (o[:, LANES:] + sw_sink[g]))
        for pp in range(SWA_GROUP // 2):
            yb_cols[2 * g + pp] = jnp.where(low, o[(2 * pp) * TQ:(2 * pp + 1) * TQ],
                                            o[(2 * pp + 1) * TQ:(2 * pp + 2) * TQ])

    mixed = jnp.concatenate(mixed_cols + yb_cols, axis=1)
    mixed = (mixed * gate_ref[...].astype(f32)).astype(bf16)
    y = jnp.dot(mixed, woutb_ref[...], preferred_element_type=f32)
    r = DEEPNORM_ALPHA * x_ref[...] + y
    mu = jnp.mean(r, axis=-1, keepdims=True)
    d = r - mu
    var = jnp.mean(d * d, axis=-1, keepdims=True)
    out_ref[...] = d * lax.rsqrt(var + LN_EPS) * gain_ref[...] + beta_ref[...]


def kernel(x, w_in, rel_pos_bias, sink_logits, w_out, ln_gain, ln_bias):
    B, S, D = x.shape
    assert D == D_MODEL and S % PROJ_TM == 0 and S % GRID_W == 0
    rows = S // GRID_W
    n_blocks = S // TQ
    n_tok = B * S
    bf16 = jnp.bfloat16
    x2 = x.reshape(n_tok, D)

    cos_tok, sin_tok, cos_feat, sin_feat = _rope_tables(S)
    entries, idx_np = _na_tile_plan(rows)
    band = jnp.asarray(_swa_band_tiles())

    for layer in range(DEPTH):
        tiles_per_seq = S // PROJ_TM
        blk128 = PROJ_TM // LANES
        qa, va, qb, vb, gate, kta, ktb = pl.pallas_call(
            _proj_kernel,
            grid=(n_tok // PROJ_TM,),
            in_specs=[
                pl.BlockSpec((PROJ_TM, D), lambda i: (i, 0)),
                pl.BlockSpec((D, IN_WIDTH), lambda i: (0, 0), pipeline_mode=pl.Buffered(1)),
                pl.BlockSpec((PROJ_TM, LANES), lambda i: (i % tiles_per_seq, 0)),
                pl.BlockSpec((PROJ_TM, LANES), lambda i: (i % tiles_per_seq, 0)),
                pl.BlockSpec((HEAD_DIM, PROJ_TM), lambda i: (0, i % tiles_per_seq)),
                pl.BlockSpec((HEAD_DIM, PROJ_TM), lambda i: (0, i % tiles_per_seq)),
            ],
            out_specs=[
                pl.BlockSpec((PROJ_TM, NA_WIDTH), lambda i: (i, 0)),
                pl.BlockSpec((PROJ_TM, NA_WIDTH), lambda i: (i, 0)),
                pl.BlockSpec((PROJ_TM, SWA_WIDTH), lambda i: (i, 0)),
                pl.BlockSpec((PROJ_TM, 2 * SWA_KV_WIDTH), lambda i: (i, 0)),
                pl.BlockSpec((PROJ_TM, MIX_WIDTH), lambda i: (i, 0)),
                pl.BlockSpec((1, blk128, NA_WIDTH, LANES), lambda i: (i // tiles_per_seq, i % tiles_per_seq, 0, 0)),
                pl.BlockSpec((1, blk128, 2 * SWA_KV_WIDTH, LANES),
                             lambda i: (i // tiles_per_seq, i % tiles_per_seq, 0, 0)),
            ],
            out_shape=[
                jax.ShapeDtypeStruct((n_tok, NA_WIDTH), bf16),
                jax.ShapeDtypeStruct((n_tok, NA_WIDTH), bf16),
                jax.ShapeDtypeStruct((n_tok, SWA_WIDTH), bf16),
                jax.ShapeDtypeStruct((n_tok, 2 * SWA_KV_WIDTH), bf16),
                jax.ShapeDtypeStruct((n_tok, MIX_WIDTH), bf16),
                jax.ShapeDtypeStruct((B, S // LANES, NA_WIDTH, LANES), bf16),
                jax.ShapeDtypeStruct((B, S // LANES, 2 * SWA_KV_WIDTH, LANES), bf16),
            ],
            scratch_shapes=[pltpu.VMEM((D, TOK_WIDTH), bf16), pltpu.VMEM((KT_ROWS, D), bf16)],
            compiler_params=pltpu.CompilerParams(
                dimension_semantics=("arbitrary",), vmem_limit_bytes=56 * 1024 * 1024),
            name="in_proj",
        )(x2, w_in[layer], cos_tok, sin_tok, cos_feat, sin_feat)

        n_dr, n_dc = 2 * NA_KH - 1, 2 * NA_KW - 1
        rel_rows = jnp.pad(rel_pos_bias[layer].astype(jnp.float32).reshape(NA_HEADS * n_dr, n_dc),
                           ((0, 0), (0, LANES - n_dc)))
        idx = jnp.asarray(idx_np.reshape(-1))
        n_ent = len(entries)
        grid_spec = pltpu.PrefetchScalarGridSpec(
            num_scalar_prefetch=2,
            grid=(B, n_blocks),
            in_specs=[
                pl.BlockSpec((TQ, NA_WIDTH), lambda b, t, *_: (b * n_blocks + t, 0)),
                pl.BlockSpec((1, S // LANES, NA_WIDTH, LANES), lambda b, t, *_: (b, 0, 0, 0)),
                pl.BlockSpec((1, S, NA_WIDTH), lambda b, t, *_: (b, 0, 0)),
                pl.BlockSpec((TQ, SWA_WIDTH), lambda b, t, *_: (b * n_blocks + t, 0)),
                pl.BlockSpec((1, S // LANES, 2 * SWA_KV_WIDTH, LANES), lambda b, t, *_: (b, 0, 0, 0)),
                pl.BlockSpec((1, S, 2 * SWA_KV_WIDTH), lambda b, t, *_: (b, 0, 0)),
                pl.BlockSpec((TQ, MIX_WIDTH), lambda b, t, *_: (b * n_blocks + t, 0)),
                pl.BlockSpec((TQ, D), lambda b, t, *_: (b * n_blocks + t, 0)),
                pl.BlockSpec((MIX_WIDTH, D), lambda b, t, *_: (0, 0), pipeline_mode=pl.Buffered(1)),
                pl.BlockSpec((NA_HEADS * n_dr, LANES), lambda b, t, *_: (0, 0)),
                pl.BlockSpec((3, TQ, SWA_KEYS), lambda b, t, *_: (0, 0, 0)),
                pl.BlockSpec((1, D), lambda b, t, *_: (0, 0)),
                pl.BlockSpec((1, D), lambda b, t, *_: (0, 0)),
            ],
            out_specs=pl.BlockSpec((TQ, D), lambda b, t, *_: (b * n_blocks + t, 0)),
            scratch_shapes=[pltpu.VMEM((MIX_WIDTH, D), bf16),
                            pltpu.VMEM((n_ent, NA_HEADS, GRID_W, 2 * GRID_W), jnp.float32)],
        )
        x2 = pl.pallas_call(
            functools.partial(_attn_kernel, rows=rows, n_blocks=n_blocks, entries=tuple(entries)),
            grid_spec=grid_spec,
            out_shape=jax.ShapeDtypeStruct((n_tok, D), jnp.float32),
            compiler_params=pltpu.CompilerParams(
                dimension_semantics=("arbitrary", "arbitrary"), vmem_limit_bytes=56 * 1024 * 1024),
            name="attn_out",
        )(idx, sink_logits[layer].astype(jnp.float32),
          qa, kta, va.reshape(B, S, NA_WIDTH), qb, ktb, vb.reshape(B, S, 2 * SWA_KV_WIDTH), gate, x2,
          w_out[layer], rel_rows, band, ln_gain[layer].reshape(1, D), ln_bias[layer].reshape(1, D))
    return x2.reshape(B, S, D)
```

```python
import functools

import numpy as np
import jax
import jax.numpy as jnp
from jax import lax
from jax.experimental import pallas as pl
from jax.experimental.pallas import tpu as pltpu

D_MODEL = 1024
HEAD_DIM = 64
NA_HEADS = 8
NA_WIDTH = NA_HEADS * HEAD_DIM
NA_KH = 8
NA_KW = 16
GRID_W = 64
SWA_Q_HEADS = 8
SWA_KV_HEADS = 2
SWA_GROUP = SWA_Q_HEADS // SWA_KV_HEADS
SWA_WIDTH = SWA_Q_HEADS * HEAD_DIM
SWA_KV_WIDTH = SWA_KV_HEADS * HEAD_DIM
SWA_WINDOW = 128
ROPE_THETA = 10000.0
MIX_WIDTH = NA_WIDTH + SWA_WIDTH
LN_EPS = 1e-5
MASK_VALUE = -1e30
DEPTH = 1
DEEPNORM_ALPHA = (2.0 * DEPTH) ** 0.25
LOG2E = 1.4426950408889634
Q_SCALE = HEAD_DIM ** -0.5 * LOG2E
N_DR = 2 * NA_KH - 1
N_DC = 2 * NA_KW - 1

LANES = 128
PAIR = 2 * HEAD_DIM
assert PAIR == LANES and 2 * GRID_W == LANES

PROJ_TM = 512
TQ = 128
Q_ROWS = TQ // GRID_W
NA_KEYS = NA_KH * GRID_W
KEY_PAIRS = NA_KEYS // LANES
N_BIAS_TILES = N_DR - 1
SWA_KEYS = 3 * TQ

_QA = (0, NA_WIDTH)
_VA = (_QA[1], _QA[1] + NA_WIDTH)
_QB = (_VA[1], _VA[1] + SWA_WIDTH)
_VB = (_QB[1], _QB[1] + 2 * SWA_KV_WIDTH)
_Z = (_VB[1], _VB[1] + MIX_WIDTH)
TOK_WIDTH = _Z[1]
_W_QA = 0
_W_KA = _W_QA + NA_WIDTH
_W_VA = _W_KA + NA_WIDTH
_W_ZA = _W_VA + NA_WIDTH
_W_QB = _W_ZA + NA_WIDTH
_W_KB = _W_QB + SWA_WIDTH
_W_VB = _W_KB + SWA_KV_WIDTH
_W_ZB = _W_VB + SWA_KV_WIDTH
IN_WIDTH = _W_ZB + SWA_WIDTH
KT_ROWS = NA_WIDTH + SWA_KV_WIDTH


def _swa_band_tiles():
    q = np.arange(TQ)[:, None]
    k = np.arange(SWA_KEYS)[None, :] - TQ
    band = np.abs(k - q) <= SWA_WINDOW
    left_ok = np.concatenate([np.zeros((1, TQ), bool), np.ones((1, 2 * TQ), bool)], axis=1)
    right_ok = np.concatenate([np.ones((1, 2 * TQ), bool), np.zeros((1, TQ), bool)], axis=1)
    variants = np.stack([band & left_ok, band, band & right_ok])
    return np.where(variants, 0.0, MASK_VALUE).astype(np.float32)


def _rope_tables(seq):
    inv_freq = ROPE_THETA ** (-jnp.arange(0, HEAD_DIM, 2, dtype=jnp.float32) / HEAD_DIM)
    ang = jnp.arange(seq, dtype=jnp.int32).astype(jnp.float32)[:, None] * inv_freq[None, :]
    cos = jnp.cos(ang)
    sin = jnp.sin(ang)
    cos_head = jnp.concatenate([cos, cos], axis=1)
    sin_head = jnp.concatenate([-sin, sin], axis=1)
    cos_tok = jnp.concatenate([cos_head, cos_head], axis=1) * Q_SCALE
    sin_tok = jnp.concatenate([sin_head, sin_head], axis=1) * Q_SCALE
    return cos_tok, sin_tok, cos_head.T, sin_head.T


def _build_proj_weights(win_ref, wtok_ref, wkt_ref):
    bf16 = jnp.bfloat16
    wtok_ref[:, _QA[0]:_QA[1]] = win_ref[:, _W_QA:_W_QA + NA_WIDTH].astype(bf16)
    wtok_ref[:, _VA[0]:_VA[1]] = win_ref[:, _W_VA:_W_VA + NA_WIDTH].astype(bf16)
    wtok_ref[:, _QB[0]:_QB[1]] = win_ref[:, _W_QB:_W_QB + SWA_WIDTH].astype(bf16)
    wtok_ref[:, _Z[0]:_Z[0] + NA_WIDTH] = win_ref[:, _W_ZA:_W_ZA + NA_WIDTH].astype(bf16)
    wtok_ref[:, _Z[0] + NA_WIDTH:_Z[1]] = win_ref[:, _W_ZB:_W_ZB + SWA_WIDTH].astype(bf16)
    vpair = win_ref[:, _W_VB:_W_VB + SWA_KV_WIDTH]
    swapped = pltpu.roll(vpair, HEAD_DIM, 1)
    low = lax.broadcasted_iota(jnp.int32, vpair.shape, 1) < HEAD_DIM
    wtok_ref[:, _VB[0]:_VB[0] + LANES] = jnp.where(low, vpair, swapped).astype(bf16)
    wtok_ref[:, _VB[0] + LANES:_VB[1]] = jnp.where(low, swapped, vpair).astype(bf16)
    for c in range(NA_WIDTH // LANES):
        wkt_ref[c * LANES:(c + 1) * LANES, :] = win_ref[:, _W_KA + c * LANES:_W_KA + (c + 1) * LANES].T.astype(bf16)
    wkt_ref[NA_WIDTH:KT_ROWS, :] = win_ref[:, _W_KB:_W_KB + SWA_KV_WIDTH].T.astype(bf16)


def _proj_kernel(x_ref, win_ref, cos_ref, sin_ref, cost_ref, sint_ref,
                 qa_ref, va_ref, qb_ref, vb_ref, gate_ref, kta_ref, ktb_ref,
                 wtok_ref, wkt_ref, carry_ref, *, tiles_per_seq):
    bf16 = jnp.bfloat16
    i = pl.program_id(0)

    @pl.when(i == 0)
    def _():
        _build_proj_weights(win_ref, wtok_ref, wkt_ref)

    @pl.when(i % tiles_per_seq == 0)
    def _():
        carry_ref[...] = jnp.zeros_like(carry_ref)

    xb = x_ref[...].astype(bf16)

    def proj(span):
        return jnp.dot(xb, wtok_ref[:, span[0]:span[1]], preferred_element_type=jnp.float32)

    qa_ref[...] = (proj(_QA) * Q_SCALE).astype(bf16)
    va_ref[...] = proj(_VA).astype(bf16)
    vb_ref[...] = proj(_VB).astype(bf16)

    t = proj(_QB)
    cos = cos_ref[...]
    sin = sin_ref[...]
    lane = lax.broadcasted_iota(jnp.int32, (PROJ_TM, LANES), 1)
    low_half = (lane % HEAD_DIM) < (HEAD_DIM // 2)
    for c in range(SWA_WIDTH // LANES):
        tc = t[:, c * LANES:(c + 1) * LANES]
        partner = jnp.where(low_half, pltpu.roll(tc, LANES - HEAD_DIM // 2, 1), pltpu.roll(tc, HEAD_DIM // 2, 1))
        qb_ref[:, c * LANES:(c + 1) * LANES] = (tc * cos + partner * sin).astype(bf16)

    z = proj(_Z)
    gate_ref[...] = (z * jax.nn.sigmoid(z)).astype(bf16)

    kt = lax.dot_general(wkt_ref[...], xb, (((1,), (1,)), ((), ())), preferred_element_type=jnp.float32)
    n_blk = PROJ_TM // LANES
    cols = [carry_ref[...]] + [kt[0:NA_WIDTH, j * LANES:(j + 1) * LANES] for j in range(n_blk)]
    low_k = lax.broadcasted_iota(jnp.int32, (NA_WIDTH, LANES), 1) < GRID_W
    rolled = [pltpu.roll(c, GRID_W, 1) for c in cols]
    for j in range(n_blk):
        kta_ref[0, 0, j] = cols[j + 1].astype(bf16)
        kta_ref[0, 1, j] = jnp.where(low_k, rolled[j], rolled[j + 1]).astype(bf16)
    carry_ref[...] = cols[n_blk]

    cos_t = cost_ref[...]
    sin_t = sint_ref[...]
    half = HEAD_DIM // 2
    parts = []
    for g in range(SWA_KV_HEADS):
        blk = kt[NA_WIDTH + g * HEAD_DIM:NA_WIDTH + (g + 1) * HEAD_DIM]
        partner = jnp.concatenate([blk[half:], blk[:half]], axis=0)
        roped = (blk * cos_t + partner * sin_t).astype(bf16)
        parts += [roped, roped]
    kb = jnp.concatenate(parts, axis=0)
    for j in range(n_blk):
        ktb_ref[0, j] = kb[:, j * LANES:(j + 1) * LANES]


def _build_na_bias_tiles(rel_ref, tbl_ref):
    c = lax.broadcasted_iota(jnp.int32, (GRID_W, LANES), 0)
    lane = lax.broadcasted_iota(jnp.int32, (GRID_W, LANES), 1)
    kc = lane & (GRID_W - 1)
    start = jnp.clip(c - NA_KW // 2, 0, GRID_W - NA_KW)
    in_win = (kc >= start) & (kc < start + NA_KW)
    low = lane < GRID_W
    masked = jnp.full((GRID_W, LANES), MASK_VALUE, jnp.float32)
    for h in range(NA_HEADS):
        def half(dr, upper):
            row = jnp.broadcast_to(rel_ref[pl.ds(h * N_DR + dr, 1), :] * LOG2E, (GRID_W, LANES))
            shift = (LANES - (NA_KW - 1) + (GRID_W if upper else 0)) % LANES
            return pltpu.roll(row, shift, 1, stride=1, stride_axis=0)

        lower = [half(dr, False) for dr in range(N_DR - 1)]
        upper = [half(dr, True) for dr in range(1, N_DR)]
        for d in range(N_BIAS_TILES):
            tbl_ref[d, h] = jnp.where(in_win, jnp.where(low, lower[d], upper[d]), masked)


def _attn_kernel(sink_ref,
                 qa_ref, kta_ref, va_ref, qb_ref, ktb_ref, vb_ref, gate_ref, x_ref, wout_ref,
                 rel_ref, band_ref, gain_ref, beta_ref, out_ref, woutb_ref, tbl_ref, *, rows, n_blocks):
    f32, bf16 = jnp.float32, jnp.bfloat16
    t = pl.program_id(1)

    @pl.when((pl.program_id(0) == 0) & (t == 0))
    def _():
        woutb_ref[...] = wout_ref[...].astype(bf16)
        _build_na_bias_tiles(rel_ref, tbl_ref)

    kh = min(NA_KH, rows)
    lane = lax.broadcasted_iota(jnp.int32, (GRID_W, LANES), 1)
    low = lane < HEAD_DIM
    low_tq = lax.broadcasted_iota(jnp.int32, (TQ, LANES), 1) < HEAD_DIM
    jl = jnp.maximum(t - 1, 0)
    jr = jnp.minimum(t + 1, n_blocks - 1)
    variant = jnp.where(t == 0, 0, jnp.where(t == n_blocks - 1, 2, 1))

    na_s, na_v, na_d0 = [], [], []
    for qrow in range(Q_ROWS):
        r = Q_ROWS * t + qrow
        r0 = jnp.clip(r - kh // 2, 0, rows - kh)
        par = r0 & 1
        key0 = pl.multiple_of(r0 * GRID_W, GRID_W)
        na_d0.append(r0 - r + (NA_KH - 1))
        for p in range(NA_HEADS // 2):
            cols = slice(p * LANES, (p + 1) * LANES)
            qp = qa_ref[qrow * GRID_W:(qrow + 1) * GRID_W, cols]
            zero = jnp.zeros_like(qp)
            q2 = jnp.concatenate([jnp.where(low, qp, zero), jnp.where(low, zero, qp)], axis=0)
            kt = jnp.concatenate([kta_ref[0, par, (r0 + 2 * j + 1) // 2, cols, :] for j in range(KEY_PAIRS)],
                                 axis=1)
            na_v.append(va_ref[0, pl.ds(key0, NA_KEYS), cols])
            na_s.append(jnp.dot(q2, kt, preferred_element_type=f32))
    sw_s, sw_v = [], []
    for g in range(SWA_KV_HEADS):
        gcols = slice(g * LANES, (g + 1) * LANES)
        kt = jnp.concatenate([ktb_ref[0, j, gcols, :] for j in (jl, t, jr)], axis=1)
        sw_v.append(jnp.concatenate(
            [vb_ref[0, pl.ds(pl.multiple_of(j * TQ, TQ), TQ), gcols] for j in (jl, t, jr)], axis=0))
        qs = []
        for pp in range(SWA_GROUP // 2):
            qp = qb_ref[:, (2 * g + pp) * LANES:(2 * g + pp + 1) * LANES]
            zero = jnp.zeros_like(qp)
            qs += [jnp.where(low_tq, qp, zero), jnp.where(low_tq, zero, qp)]
        q4 = jnp.concatenate(qs, axis=0)
        sw_s.append(jnp.dot(q4, kt, preferred_element_type=f32))

    na_e = []
    for qrow in range(Q_ROWS):
        for p in range(NA_HEADS // 2):
            bias = jnp.concatenate(
                [jnp.concatenate([tbl_ref[na_d0[qrow] + 2 * j, 2 * p + par] for j in range(KEY_PAIRS)], axis=1)
                 for par in range(2)], axis=0)
            s = na_s[qrow * (NA_HEADS // 2) + p] + bias
            m = jnp.max(s, axis=-1, keepdims=True)
            na_e.append(jnp.exp2(s - m).astype(bf16))
    band = band_ref[variant]
    band4 = jnp.concatenate([band] * SWA_GROUP, axis=0)
    sw_e, sw_sink = [], []
    for g in range(SWA_KV_HEADS):
        s = sw_s[g] + band4
        sink = jnp.concatenate(
            [jnp.full((TQ, 1), sink_ref[SWA_GROUP * g + j] * LOG2E, f32) for j in range(SWA_GROUP)], axis=0)
        m = jnp.maximum(jnp.max(s, axis=-1, keepdims=True), sink)
        sw_e.append(jnp.exp2(s - m).astype(bf16))
        sw_sink.append(jnp.exp2(sink - m))

    na_out = []
    for u in range(Q_ROWS * (NA_HEADS // 2)):
        v1 = jnp.concatenate([na_v[u], jnp.ones_like(na_v[u])], axis=1)
        o = jnp.dot(na_e[u], v1, preferred_element_type=f32)
        o = o[:, :LANES] * (1.0 / o[:, LANES:])
        na_out.append(jnp.where(low, o[:GRID_W], o[GRID_W:]))
    mixed_cols = [jnp.concatenate([na_out[qrow * (NA_HEADS // 2) + p] for qrow in range(Q_ROWS)], axis=0)
                  for p in range(NA_HEADS // 2)]
    yb_cols = [None] * (SWA_Q_HEADS // 2)
    for g in range(SWA_KV_HEADS):
        v1 = jnp.concatenate([sw_v[g], jnp.ones_like(sw_v[g])], axis=1)
        o = jnp.dot(sw_e[g], v1, preferred_element_type=f32)
        o = o[:, :LANES] * (1.0 / (o[:, LANES:] + sw_sink[g]))
        for pp in range(SWA_GROUP // 2):
            yb_cols[2 * g + pp] = jnp.where(low_tq, o[(2 * pp) * TQ:(2 * pp + 1) * TQ],
                                            o[(2 * pp + 1) * TQ:(2 * pp + 2) * TQ])

    mixed = jnp.concatenate(mixed_cols + yb_cols, axis=1)
    mixed = (mixed * gate_ref[...].astype(f32)).astype(bf16)
    y = jnp.dot(mixed, woutb_ref[...], preferred_element_type=f32)
    r = DEEPNORM_ALPHA * x_ref[...] + y
    mu = jnp.mean(r, axis=-1, keepdims=True)
    d = r - mu
    var = jnp.mean(d * d, axis=-1, keepdims=True)
    out_ref[...] = d * lax.rsqrt(var + LN_EPS) * gain_ref[...] + beta_ref[...]


def kernel(x, w_in, rel_pos_bias, sink_logits, w_out, ln_gain, ln_bias):
    B, S, D = x.shape
    assert D == D_MODEL and S % PROJ_TM == 0 and S % GRID_W == 0
    rows = S // GRID_W
    assert rows >= NA_KH and rows % Q_ROWS == 0
    n_blocks = S // TQ
    n_tok = B * S
    bf16 = jnp.bfloat16
    x2 = x.reshape(n_tok, D)

    cos_tok, sin_tok, cos_feat, sin_feat = _rope_tables(S)
    band = jnp.asarray(_swa_band_tiles())

    for layer in range(DEPTH):
        tiles_per_seq = S // PROJ_TM
        blk128 = PROJ_TM // LANES
        qa, va, qb, vb, gate, kta, ktb = pl.pallas_call(
            functools.partial(_proj_kernel, tiles_per_seq=tiles_per_seq),
            grid=(n_tok // PROJ_TM,),
            in_specs=[
                pl.BlockSpec((PROJ_TM, D), lambda i: (i, 0)),
                pl.BlockSpec((D, IN_WIDTH), lambda i: (0, 0), pipeline_mode=pl.Buffered(1)),
                pl.BlockSpec((PROJ_TM, LANES), lambda i: (i % tiles_per_seq, 0)),
                pl.BlockSpec((PROJ_TM, LANES), lambda i: (i % tiles_per_seq, 0)),
                pl.BlockSpec((HEAD_DIM, PROJ_TM), lambda i: (0, i % tiles_per_seq)),
                pl.BlockSpec((HEAD_DIM, PROJ_TM), lambda i: (0, i % tiles_per_seq)),
            ],
            out_specs=[
                pl.BlockSpec((PROJ_TM, NA_WIDTH), lambda i: (i, 0)),
                pl.BlockSpec((PROJ_TM, NA_WIDTH), lambda i: (i, 0)),
                pl.BlockSpec((PROJ_TM, SWA_WIDTH), lambda i: (i, 0)),
                pl.BlockSpec((PROJ_TM, 2 * SWA_KV_WIDTH), lambda i: (i, 0)),
                pl.BlockSpec((PROJ_TM, MIX_WIDTH), lambda i: (i, 0)),
                pl.BlockSpec((1, 2, blk128, NA_WIDTH, LANES),
                             lambda i: (i // tiles_per_seq, 0, i % tiles_per_seq, 0, 0)),
                pl.BlockSpec((1, blk128, 2 * SWA_KV_WIDTH, LANES),
                             lambda i: (i // tiles_per_seq, i % tiles_per_seq, 0, 0)),
            ],
            out_shape=[
                jax.ShapeDtypeStruct((n_tok, NA_WIDTH), bf16),
                jax.ShapeDtypeStruct((n_tok, NA_WIDTH), bf16),
                jax.ShapeDtypeStruct((n_tok, SWA_WIDTH), bf16),
                jax.ShapeDtypeStruct((n_tok, 2 * SWA_KV_WIDTH), bf16),
                jax.ShapeDtypeStruct((n_tok, MIX_WIDTH), bf16),
                jax.ShapeDtypeStruct((B, 2, S // LANES, NA_WIDTH, LANES), bf16),
                jax.ShapeDtypeStruct((B, S // LANES, 2 * SWA_KV_WIDTH, LANES), bf16),
            ],
            scratch_shapes=[pltpu.VMEM((D, TOK_WIDTH), bf16), pltpu.VMEM((KT_ROWS, D), bf16),
                            pltpu.VMEM((NA_WIDTH, LANES), jnp.float32)],
            compiler_params=pltpu.CompilerParams(
                dimension_semantics=("arbitrary",), vmem_limit_bytes=56 * 1024 * 1024),
            name="in_proj",
        )(x2, w_in[layer], cos_tok, sin_tok, cos_feat, sin_feat)

        rel_rows = jnp.pad(rel_pos_bias[layer].astype(jnp.float32).reshape(NA_HEADS * N_DR, N_DC),
                           ((0, 0), (0, LANES - N_DC)))
        grid_spec = pltpu.PrefetchScalarGridSpec(
            num_scalar_prefetch=1,
            grid=(B, n_blocks),
            in_specs=[
                pl.BlockSpec((TQ, NA_WIDTH), lambda b, t, *_: (b * n_blocks + t, 0)),
                pl.BlockSpec((1, 2, S // LANES, NA_WIDTH, LANES), lambda b, t, *_: (b, 0, 0, 0, 0)),
                pl.BlockSpec((1, S, NA_WIDTH), lambda b, t, *_: (b, 0, 0)),
                pl.BlockSpec((TQ, SWA_WIDTH), lambda b, t, *_: (b * n_blocks + t, 0)),
                pl.BlockSpec((1, S // LANES, 2 * SWA_KV_WIDTH, LANES), lambda b, t, *_: (b, 0, 0, 0)),
                pl.BlockSpec((1, S, 2 * SWA_KV_WIDTH), lambda b, t, *_: (b, 0, 0)),
                pl.BlockSpec((TQ, MIX_WIDTH), lambda b, t, *_: (b * n_blocks + t, 0)),
                pl.BlockSpec((TQ, D), lambda b, t, *_: (b * n_blocks + t, 0)),
                pl.BlockSpec((MIX_WIDTH, D), lambda b, t, *_: (0, 0), pipeline_mode=pl.Buffered(1)),
                pl.BlockSpec((NA_HEADS * N_DR, LANES), lambda b, t, *_: (0, 0)),
                pl.BlockSpec((3, TQ, SWA_KEYS), lambda b, t, *_: (0, 0, 0)),
                pl.BlockSpec((1, D), lambda b, t, *_: (0, 0)),
                pl.BlockSpec((1, D), lambda b, t, *_: (0, 0)),
            ],
            out_specs=pl.BlockSpec((TQ, D), lambda b, t, *_: (b * n_blocks + t, 0)),
            scratch_shapes=[pltpu.VMEM((MIX_WIDTH, D), bf16),
                            pltpu.VMEM((N_BIAS_TILES, NA_HEADS, GRID_W, 2 * GRID_W), jnp.float32)],
        )
        x2 = pl.pallas_call(
            functools.partial(_attn_kernel, rows=rows, n_blocks=n_blocks),
            grid_spec=grid_spec,
            out_shape=jax.ShapeDtypeStruct((n_tok, D), jnp.float32),
            compiler_params=pltpu.CompilerParams(
                dimension_semantics=("arbitrary", "arbitrary"), vmem_limit_bytes=56 * 1024 * 1024),
            name="attn_out",
        )(sink_logits[layer].astype(jnp.float32),
          qa, kta, va.reshape(B, S, NA_WIDTH), qb, ktb, vb.reshape(B, S, 2 * SWA_KV_WIDTH), gate, x2,
          w_out[layer], rel_rows, band, ln_gain[layer].reshape(1, D), ln_bias[layer].reshape(1, D))
    return x2.reshape(B, S, D)
```

```python
import functools

import numpy as np
import jax
import jax.numpy as jnp
from jax import lax
from jax.experimental import pallas as pl
from jax.experimental.pallas import tpu as pltpu

D_MODEL = 1024
HEAD_DIM = 64
NA_HEADS = 8
NA_WIDTH = NA_HEADS * HEAD_DIM
NA_KH = 8
NA_KW = 16
GRID_W = 64
SWA_Q_HEADS = 8
SWA_KV_HEADS = 2
SWA_GROUP = SWA_Q_HEADS // SWA_KV_HEADS
SWA_WIDTH = SWA_Q_HEADS * HEAD_DIM
SWA_KV_WIDTH = SWA_KV_HEADS * HEAD_DIM
SWA_WINDOW = 128
ROPE_THETA = 10000.0
MIX_WIDTH = NA_WIDTH + SWA_WIDTH
LN_EPS = 1e-5
MASK_VALUE = -1e30
DEPTH = 1
DEEPNORM_ALPHA = (2.0 * DEPTH) ** 0.25
LOG2E = 1.4426950408889634
Q_SCALE = HEAD_DIM ** -0.5 * LOG2E
N_DR = 2 * NA_KH - 1
N_DC = 2 * NA_KW - 1

LANES = 128
PAIR = 2 * HEAD_DIM
assert PAIR == LANES and 2 * GRID_W == LANES

PROJ_TM = 512
TQ = 128
Q_ROWS = TQ // GRID_W
NA_KEYS = NA_KH * GRID_W
KEY_PAIRS = NA_KEYS // LANES
N_BIAS_TILES = N_DR - 1
SWA_KEYS = 3 * TQ

_QA = (0, NA_WIDTH)
_VA = (_QA[1], _QA[1] + NA_WIDTH)
_QB = (_VA[1], _VA[1] + SWA_WIDTH)
_VB = (_QB[1], _QB[1] + 2 * SWA_KV_WIDTH)
_Z = (_VB[1], _VB[1] + MIX_WIDTH)
TOK_WIDTH = _Z[1]
_W_QA = 0
_W_KA = _W_QA + NA_WIDTH
_W_VA = _W_KA + NA_WIDTH
_W_ZA = _W_VA + NA_WIDTH
_W_QB = _W_ZA + NA_WIDTH
_W_KB = _W_QB + SWA_WIDTH
_W_VB = _W_KB + SWA_KV_WIDTH
_W_ZB = _W_VB + SWA_KV_WIDTH
IN_WIDTH = _W_ZB + SWA_WIDTH
KT_ROWS = NA_WIDTH + SWA_KV_WIDTH


def _swa_band_tiles():
    q = np.arange(TQ)[:, None]
    k = np.arange(SWA_KEYS)[None, :] - TQ
    band = np.abs(k - q) <= SWA_WINDOW
    left_ok = np.concatenate([np.zeros((1, TQ), bool), np.ones((1, 2 * TQ), bool)], axis=1)
    right_ok = np.concatenate([np.ones((1, 2 * TQ), bool), np.zeros((1, TQ), bool)], axis=1)
    variants = np.stack([band & left_ok, band, band & right_ok])
    return np.where(variants, 0.0, MASK_VALUE).astype(np.float32)


def _rope_tables(seq):
    inv_freq = ROPE_THETA ** (-jnp.arange(0, HEAD_DIM, 2, dtype=jnp.float32) / HEAD_DIM)
    ang = jnp.arange(seq, dtype=jnp.int32).astype(jnp.float32)[:, None] * inv_freq[None, :]
    cos = jnp.cos(ang)
    sin = jnp.sin(ang)
    cos_head = jnp.concatenate([cos, cos], axis=1)
    sin_head = jnp.concatenate([-sin, sin], axis=1)
    cos_tok = jnp.concatenate([cos_head, cos_head], axis=1) * Q_SCALE
    sin_tok = jnp.concatenate([sin_head, sin_head], axis=1) * Q_SCALE
    return cos_tok, sin_tok, cos_head.T, sin_head.T


def _build_proj_weights(win_ref, wtok_ref, wkt_ref):
    bf16 = jnp.bfloat16
    wtok_ref[:, _QA[0]:_QA[1]] = win_ref[:, _W_QA:_W_QA + NA_WIDTH].astype(bf16)
    wtok_ref[:, _VA[0]:_VA[1]] = win_ref[:, _W_VA:_W_VA + NA_WIDTH].astype(bf16)
    wtok_ref[:, _QB[0]:_QB[1]] = win_ref[:, _W_QB:_W_QB + SWA_WIDTH].astype(bf16)
    wtok_ref[:, _Z[0]:_Z[0] + NA_WIDTH] = win_ref[:, _W_ZA:_W_ZA + NA_WIDTH].astype(bf16)
    wtok_ref[:, _Z[0] + NA_WIDTH:_Z[1]] = win_ref[:, _W_ZB:_W_ZB + SWA_WIDTH].astype(bf16)
    vpair = win_ref[:, _W_VB:_W_VB + SWA_KV_WIDTH]
    swapped = pltpu.roll(vpair, HEAD_DIM, 1)
    low = lax.broadcasted_iota(jnp.int32, vpair.shape, 1) < HEAD_DIM
    wtok_ref[:, _VB[0]:_VB[0] + LANES] = jnp.where(low, vpair, swapped).astype(bf16)
    wtok_ref[:, _VB[0] + LANES:_VB[1]] = jnp.where(low, swapped, vpair).astype(bf16)
    for c in range(NA_WIDTH // LANES):
        wkt_ref[c * LANES:(c + 1) * LANES, :] = win_ref[:, _W_KA + c * LANES:_W_KA + (c + 1) * LANES].T.astype(bf16)
    wkt_ref[NA_WIDTH:KT_ROWS, :] = win_ref[:, _W_KB:_W_KB + SWA_KV_WIDTH].T.astype(bf16)


def _proj_kernel(x_ref, win_ref, cos_ref, sin_ref, cost_ref, sint_ref,
                 qa_ref, va_ref, qb_ref, vb_ref, gate_ref, kta_ref, ktb_ref,
                 wtok_ref, wkt_ref, carry_ref, *, tiles_per_seq):
    bf16 = jnp.bfloat16
    i = pl.program_id(0)

    @pl.when(i == 0)
    def _():
        _build_proj_weights(win_ref, wtok_ref, wkt_ref)

    @pl.when(i % tiles_per_seq == 0)
    def _():
        carry_ref[...] = jnp.zeros_like(carry_ref)

    xb = x_ref[...].astype(bf16)

    def proj(span):
        return jnp.dot(xb, wtok_ref[:, span[0]:span[1]], preferred_element_type=jnp.float32)

    qa_ref[...] = (proj(_QA) * Q_SCALE).astype(bf16)
    va_ref[...] = proj(_VA).astype(bf16)
    vb_ref[...] = proj(_VB).astype(bf16)

    t = proj(_QB)
    cos = cos_ref[...]
    sin = sin_ref[...]
    lane = lax.broadcasted_iota(jnp.int32, (PROJ_TM, LANES), 1)
    low_half = (lane % HEAD_DIM) < (HEAD_DIM // 2)
    for c in range(SWA_WIDTH // LANES):
        tc = t[:, c * LANES:(c + 1) * LANES]
        partner = jnp.where(low_half, pltpu.roll(tc, LANES - HEAD_DIM // 2, 1), pltpu.roll(tc, HEAD_DIM // 2, 1))
        qb_ref[:, c * LANES:(c + 1) * LANES] = (tc * cos + partner * sin).astype(bf16)

    z = proj(_Z)
    gate_ref[...] = (z * jax.nn.sigmoid(z)).astype(bf16)

    kt = lax.dot_general(wkt_ref[...], xb, (((1,), (1,)), ((), ())), preferred_element_type=jnp.float32)
    n_blk = PROJ_TM // LANES
    cols = [carry_ref[...]] + [kt[0:NA_WIDTH, j * LANES:(j + 1) * LANES] for j in range(n_blk)]
    low_k = lax.broadcasted_iota(jnp.int32, (NA_WIDTH, LANES), 1) < GRID_W
    rolled = [pltpu.roll(c, GRID_W, 1) for c in cols]
    for j in range(n_blk):
        kta_ref[0, 0, j] = cols[j + 1].astype(bf16)
        kta_ref[0, 1, j] = jnp.where(low_k, rolled[j], rolled[j + 1]).astype(bf16)
    carry_ref[...] = cols[n_blk]

    cos_t = cost_ref[...]
    sin_t = sint_ref[...]
    half = HEAD_DIM // 2
    parts = []
    for g in range(SWA_KV_HEADS):
        blk = kt[NA_WIDTH + g * HEAD_DIM:NA_WIDTH + (g + 1) * HEAD_DIM]
        partner = jnp.concatenate([blk[half:], blk[:half]], axis=0)
        roped = (blk * cos_t + partner * sin_t).astype(bf16)
        parts += [roped, roped]
    kb = jnp.concatenate(parts, axis=0)
    for j in range(n_blk):
        ktb_ref[0, j] = kb[:, j * LANES:(j + 1) * LANES]


def _build_na_bias_tiles(rel_ref, tbl_ref):
    c = lax.broadcasted_iota(jnp.int32, (GRID_W, LANES), 0)
    lane = lax.broadcasted_iota(jnp.int32, (GRID_W, LANES), 1)
    kc = lane & (GRID_W - 1)
    start = jnp.clip(c - NA_KW // 2, 0, GRID_W - NA_KW)
    in_win = (kc >= start) & (kc < start + NA_KW)
    low = lane < GRID_W
    masked = jnp.full((GRID_W, LANES), MASK_VALUE, jnp.float32)
    for h in range(NA_HEADS):
        def half(dr, upper):
            row = jnp.broadcast_to(rel_ref[pl.ds(h * N_DR + dr, 1), :] * LOG2E, (GRID_W, LANES))
            shift = (LANES - (NA_KW - 1) + (GRID_W if upper else 0)) % LANES
            return pltpu.roll(row, shift, 1, stride=1, stride_axis=0)

        lower = [half(dr, False) for dr in range(N_DR - 1)]
        upper = [half(dr, True) for dr in range(1, N_DR)]
        for d in range(N_BIAS_TILES):
            tbl_ref[d, h] = jnp.where(in_win, jnp.where(low, lower[d], upper[d]), masked)


def _attn_kernel(sink_ref,
                 qa_ref, kta_ref, va_ref, qb_ref, ktb_ref, vb_ref, gate_ref, x_ref, wout_ref,
                 rel_ref, band_ref, gain_ref, beta_ref, out_ref, woutb_ref, tbl_ref, sna_ref, ssw_ref,
                 *, rows, n_blocks, n_total):
    i = pl.program_id(0)

    @pl.when(i == 0)
    def _():
        woutb_ref[...] = wout_ref[...].astype(jnp.bfloat16)
        _build_na_bias_tiles(rel_ref, tbl_ref)
        sna_ref[0] = jnp.zeros(sna_ref.shape[1:], sna_ref.dtype)
        ssw_ref[0] = jnp.zeros(ssw_ref.shape[1:], ssw_ref.dtype)

    refs = (sink_ref, qa_ref, kta_ref, va_ref, qb_ref, ktb_ref, vb_ref, gate_ref, x_ref,
            band_ref, gain_ref, beta_ref, out_ref, woutb_ref, tbl_ref)
    for parity in range(2):
        @pl.when(i % 2 == parity)
        def _():
            _attn_step(i, *refs, sna_ref.at[parity], ssw_ref.at[parity], sna_ref.at[1 - parity],
                       ssw_ref.at[1 - parity], rows=rows, n_blocks=n_blocks, n_total=n_total)


def _attn_step(i, sink_ref, qa_ref, kta_ref, va_ref, qb_ref, ktb_ref, vb_ref, gate_ref, x_ref,
               band_ref, gain_ref, beta_ref, out_ref, woutb_ref, tbl_ref,
               sna_rd, ssw_rd, sna_wr, ssw_wr, *, rows, n_blocks, n_total):
    f32, bf16 = jnp.float32, jnp.bfloat16
    kh = min(NA_KH, rows)
    n_pairs = NA_HEADS // 2
    lane = lax.broadcasted_iota(jnp.int32, (GRID_W, LANES), 1)
    low = lane < HEAD_DIM
    low_tq = lax.broadcasted_iota(jnp.int32, (TQ, LANES), 1) < HEAD_DIM

    def window_start(t, qrow):
        r = Q_ROWS * t + qrow
        return r, jnp.clip(r - kh // 2, 0, rows - kh)

    t = jnp.maximum(i - 1, 0) % n_blocks
    jl = jnp.maximum(t - 1, 0)
    jr = jnp.minimum(t + 1, n_blocks - 1)

    na_e, na_v = [], []
    for qrow in range(Q_ROWS):
        _, r0 = window_start(t, qrow)
        key0 = pl.multiple_of(r0 * GRID_W, GRID_W)
        for p in range(n_pairs):
            s_ref = sna_rd.at[qrow * n_pairs + p]
            m = jnp.max(s_ref[...], axis=-1, keepdims=True)
            na_e.append(jnp.exp2(s_ref[...] - m).astype(bf16))
            na_v.append(va_ref[0, pl.ds(key0, NA_KEYS), p * LANES:(p + 1) * LANES])
    sw_e, sw_sink, sw_v = [], [], []
    for g in range(SWA_KV_HEADS):
        s = ssw_rd[g]
        sink = jnp.concatenate(
            [jnp.full((TQ, 1), sink_ref[SWA_GROUP * g + j] * LOG2E, f32) for j in range(SWA_GROUP)], axis=0)
        m = jnp.maximum(jnp.max(s, axis=-1, keepdims=True), sink)
        sw_e.append(jnp.exp2(s - m).astype(bf16))
        sw_sink.append(jnp.exp2(sink - m))
        sw_v.append(jnp.concatenate(
            [vb_ref[0, pl.ds(pl.multiple_of(j * TQ, TQ), TQ), g * LANES:(g + 1) * LANES] for j in (jl, t, jr)],
            axis=0))

    na_out = []
    for u in range(Q_ROWS * n_pairs):
        v1 = jnp.concatenate([na_v[u], jnp.ones_like(na_v[u])], axis=1)
        o = jnp.dot(na_e[u], v1, preferred_element_type=f32)
        o = o[:, :LANES] * (1.0 / o[:, LANES:])
        na_out.append(jnp.where(low, o[:GRID_W], o[GRID_W:]))
    mixed_cols = [jnp.concatenate([na_out[qrow * n_pairs + p] for qrow in range(Q_ROWS)], axis=0)
                  for p in range(n_pairs)]
    yb_cols = [None] * (SWA_Q_HEADS // 2)
    for g in range(SWA_KV_HEADS):
        v1 = jnp.concatenate([sw_v[g], jnp.ones_like(sw_v[g])], axis=1)
        o = jnp.dot(sw_e[g], v1, preferred_element_type=f32)
        o = o[:, :LANES] * (1.0 / (o[:, LANES:] + sw_sink[g]))
        for pp in range(SWA_GROUP // 2):
            yb_cols[2 * g + pp] = jnp.where(low_tq, o[(2 * pp) * TQ:(2 * pp + 1) * TQ],
                                            o[(2 * pp + 1) * TQ:(2 * pp + 2) * TQ])

    mixed = jnp.concatenate(mixed_cols + yb_cols, axis=1)
    mixed = (mixed * gate_ref[...].astype(f32)).astype(bf16)
    y = jnp.dot(mixed, woutb_ref[...], preferred_element_type=f32)
    r = DEEPNORM_ALPHA * x_ref[...] + y
    mu = jnp.mean(r, axis=-1, keepdims=True)
    d = r - mu
    var = jnp.mean(d * d, axis=-1, keepdims=True)
    out_ref[...] = d * lax.rsqrt(var + LN_EPS) * gain_ref[...] + beta_ref[...]

    t1 = jnp.minimum(i, n_total - 1) % n_blocks
    for qrow in range(Q_ROWS):
        r, r0 = window_start(t1, qrow)
        par = r0 & 1
        d0 = r0 - r + (NA_KH - 1)
        for p in range(n_pairs):
            cols = slice(p * LANES, (p + 1) * LANES)
            qp = qa_ref[qrow * GRID_W:(qrow + 1) * GRID_W, cols]
            zero = jnp.zeros_like(qp)
            q2 = jnp.concatenate([jnp.where(low, qp, zero), jnp.where(low, zero, qp)], axis=0)
            kt = jnp.concatenate([kta_ref[0, par, (r0 + 2 * j + 1) // 2, cols, :] for j in range(KEY_PAIRS)],
                                 axis=1)
            bias = jnp.concatenate(
                [jnp.concatenate([tbl_ref[d0 + 2 * j, 2 * p + hp] for j in range(KEY_PAIRS)], axis=1)
                 for hp in range(2)], axis=0)
            sna_wr[qrow * n_pairs + p] = jnp.dot(q2, kt, preferred_element_type=f32) + bias
    jl1 = jnp.maximum(t1 - 1, 0)
    jr1 = jnp.minimum(t1 + 1, n_blocks - 1)
    variant = jnp.where(t1 == 0, 0, jnp.where(t1 == n_blocks - 1, 2, 1))
    band = band_ref[variant]
    band4 = jnp.concatenate([band] * SWA_GROUP, axis=0)
    for g in range(SWA_KV_HEADS):
        kt = jnp.concatenate([ktb_ref[0, j, g * LANES:(g + 1) * LANES, :] for j in (jl1, t1, jr1)], axis=1)
        qs = []
        for pp in range(SWA_GROUP // 2):
            qp = qb_ref[:, (2 * g + pp) * LANES:(2 * g + pp + 1) * LANES]
            zero = jnp.zeros_like(qp)
            qs += [jnp.where(low_tq, qp, zero), jnp.where(low_tq, zero, qp)]
        q4 = jnp.concatenate(qs, axis=0)
        ssw_wr[g] = jnp.dot(q4, kt, preferred_element_type=f32) + band4


def kernel(x, w_in, rel_pos_bias, sink_logits, w_out, ln_gain, ln_bias):
    B, S, D = x.shape
    assert D == D_MODEL and S % PROJ_TM == 0 and S % GRID_W == 0
    rows = S // GRID_W
    assert rows >= NA_KH and rows % Q_ROWS == 0
    n_blocks = S // TQ
    n_tok = B * S
    bf16 = jnp.bfloat16
    x2 = x.reshape(n_tok, D)

    cos_tok, sin_tok, cos_feat, sin_feat = _rope_tables(S)
    band = jnp.asarray(_swa_band_tiles())

    for layer in range(DEPTH):
        tiles_per_seq = S // PROJ_TM
        blk128 = PROJ_TM // LANES
        qa, va, qb, vb, gate, kta, ktb = pl.pallas_call(
            functools.partial(_proj_kernel, tiles_per_seq=tiles_per_seq),
            grid=(n_tok // PROJ_TM,),
            in_specs=[
                pl.BlockSpec((PROJ_TM, D), lambda i: (i, 0)),
                pl.BlockSpec((D, IN_WIDTH), lambda i: (0, 0), pipeline_mode=pl.Buffered(1)),
                pl.BlockSpec((PROJ_TM, LANES), lambda i: (i % tiles_per_seq, 0)),
                pl.BlockSpec((PROJ_TM, LANES), lambda i: (i % tiles_per_seq, 0)),
                pl.BlockSpec((HEAD_DIM, PROJ_TM), lambda i: (0, i % tiles_per_seq)),
                pl.BlockSpec((HEAD_DIM, PROJ_TM), lambda i: (0, i % tiles_per_seq)),
            ],
            out_specs=[
                pl.BlockSpec((PROJ_TM, NA_WIDTH), lambda i: (i, 0)),
                pl.BlockSpec((PROJ_TM, NA_WIDTH), lambda i: (i, 0)),
                pl.BlockSpec((PROJ_TM, SWA_WIDTH), lambda i: (i, 0)),
                pl.BlockSpec((PROJ_TM, 2 * SWA_KV_WIDTH), lambda i: (i, 0)),
                pl.BlockSpec((PROJ_TM, MIX_WIDTH), lambda i: (i, 0)),
                pl.BlockSpec((1, 2, blk128, NA_WIDTH, LANES),
                             lambda i: (i // tiles_per_seq, 0, i % tiles_per_seq, 0, 0)),
                pl.BlockSpec((1, blk128, 2 * SWA_KV_WIDTH, LANES),
                             lambda i: (i // tiles_per_seq, i % tiles_per_seq, 0, 0)),
            ],
            out_shape=[
                jax.ShapeDtypeStruct((n_tok, NA_WIDTH), bf16),
                jax.ShapeDtypeStruct((n_tok, NA_WIDTH), bf16),
                jax.ShapeDtypeStruct((n_tok, SWA_WIDTH), bf16),
                jax.ShapeDtypeStruct((n_tok, 2 * SWA_KV_WIDTH), bf16),
                jax.ShapeDtypeStruct((n_tok, MIX_WIDTH), bf16),
                jax.ShapeDtypeStruct((B, 2, S // LANES, NA_WIDTH, LANES), bf16),
                jax.ShapeDtypeStruct((B, S // LANES, 2 * SWA_KV_WIDTH, LANES), bf16),
            ],
            scratch_shapes=[pltpu.VMEM((D, TOK_WIDTH), bf16), pltpu.VMEM((KT_ROWS, D), bf16),
                            pltpu.VMEM((NA_WIDTH, LANES), jnp.float32)],
            compiler_params=pltpu.CompilerParams(
                dimension_semantics=("arbitrary",), vmem_limit_bytes=56 * 1024 * 1024),
            name="in_proj",
        )(x2, w_in[layer], cos_tok, sin_tok, cos_feat, sin_feat)

        rel_rows = jnp.pad(rel_pos_bias[layer].astype(jnp.float32).reshape(NA_HEADS * N_DR, N_DC),
                           ((0, 0), (0, LANES - N_DC)))
        n_total = B * n_blocks
        cur = lambda i: jnp.minimum(i, n_total - 1)
        prv = lambda i: jnp.maximum(i - 1, 0)
        grid_spec = pltpu.PrefetchScalarGridSpec(
            num_scalar_prefetch=1,
            grid=(n_total + 1,),
            in_specs=[
                pl.BlockSpec((TQ, NA_WIDTH), lambda i, *_: (cur(i), 0)),
                pl.BlockSpec((1, 2, S // LANES, NA_WIDTH, LANES), lambda i, *_: (cur(i) // n_blocks, 0, 0, 0, 0)),
                pl.BlockSpec((1, S, NA_WIDTH), lambda i, *_: (prv(i) // n_blocks, 0, 0)),
                pl.BlockSpec((TQ, SWA_WIDTH), lambda i, *_: (cur(i), 0)),
                pl.BlockSpec((1, S // LANES, 2 * SWA_KV_WIDTH, LANES), lambda i, *_: (cur(i) // n_blocks, 0, 0, 0)),
                pl.BlockSpec((1, S, 2 * SWA_KV_WIDTH), lambda i, *_: (prv(i) // n_blocks, 0, 0)),
                pl.BlockSpec((TQ, MIX_WIDTH), lambda i, *_: (prv(i), 0)),
                pl.BlockSpec((TQ, D), lambda i, *_: (prv(i), 0)),
                pl.BlockSpec((MIX_WIDTH, D), lambda i, *_: (0, 0), pipeline_mode=pl.Buffered(1)),
                pl.BlockSpec((NA_HEADS * N_DR, LANES), lambda i, *_: (0, 0)),
                pl.BlockSpec((3, TQ, SWA_KEYS), lambda i, *_: (0, 0, 0)),
                pl.BlockSpec((1, D), lambda i, *_: (0, 0)),
                pl.BlockSpec((1, D), lambda i, *_: (0, 0)),
            ],
            out_specs=pl.BlockSpec((TQ, D), lambda i, *_: (prv(i), 0)),
            scratch_shapes=[pltpu.VMEM((MIX_WIDTH, D), bf16),
                            pltpu.VMEM((N_BIAS_TILES, NA_HEADS, GRID_W, 2 * GRID_W), jnp.float32),
                            pltpu.VMEM((2, Q_ROWS * NA_HEADS // 2, TQ, NA_KEYS), jnp.float32),
                            pltpu.VMEM((2, SWA_KV_HEADS, SWA_GROUP * TQ, SWA_KEYS), jnp.float32)],
        )
        x2 = pl.pallas_call(
            functools.partial(_attn_kernel, rows=rows, n_blocks=n_blocks, n_total=n_total),
            grid_spec=grid_spec,
            out_shape=jax.ShapeDtypeStruct((n_tok, D), jnp.float32),
            compiler_params=pltpu.CompilerParams(
                dimension_semantics=("arbitrary",), vmem_limit_bytes=56 * 1024 * 1024),
            name="attn_out",
        )(sink_logits[layer].astype(jnp.float32),
          qa, kta, va.reshape(B, S, NA_WIDTH), qb, ktb, vb.reshape(B, S, 2 * SWA_KV_WIDTH), gate, x2,
          w_out[layer], rel_rows, band, ln_gain[layer].reshape(1, D), ln_bias[layer].reshape(1, D))
    return x2.reshape(B, S, D)
```

```python
import functools

import numpy as np
import jax
import jax.numpy as jnp
from jax import lax
from jax.experimental import pallas as pl
from jax.experimental.pallas import tpu as pltpu

D_MODEL = 1024
HEAD_DIM = 64
NA_HEADS = 8
NA_WIDTH = NA_HEADS * HEAD_DIM
NA_KH = 8
NA_KW = 16
GRID_W = 64
SWA_Q_HEADS = 8
SWA_KV_HEADS = 2
SWA_GROUP = SWA_Q_HEADS // SWA_KV_HEADS
SWA_WIDTH = SWA_Q_HEADS * HEAD_DIM
SWA_KV_WIDTH = SWA_KV_HEADS * HEAD_DIM
SWA_WINDOW = 128
ROPE_THETA = 10000.0
MIX_WIDTH = NA_WIDTH + SWA_WIDTH
LN_EPS = 1e-5
MASK_VALUE = -1e30
DEPTH = 1
DEEPNORM_ALPHA = (2.0 * DEPTH) ** 0.25
LOG2E = 1.4426950408889634
Q_SCALE = HEAD_DIM ** -0.5 * LOG2E
N_DR = 2 * NA_KH - 1
N_DC = 2 * NA_KW - 1

LANES = 128
PAIR = 2 * HEAD_DIM
assert PAIR == LANES and 2 * GRID_W == LANES

PROJ_TM = 1024
TQ = 128
Q_ROWS = TQ // GRID_W
NA_KEYS = NA_KH * GRID_W
KEY_PAIRS = NA_KEYS // LANES
N_BIAS_TILES = N_DR - 1
SWA_KEYS = 3 * TQ

_QA = (0, NA_WIDTH)
_VA = (_QA[1], _QA[1] + NA_WIDTH)
_QB = (_VA[1], _VA[1] + SWA_WIDTH)
_VB = (_QB[1], _QB[1] + 2 * SWA_KV_WIDTH)
_Z = (_VB[1], _VB[1] + MIX_WIDTH)
TOK_WIDTH = _Z[1]
_W_QA = 0
_W_KA = _W_QA + NA_WIDTH
_W_VA = _W_KA + NA_WIDTH
_W_ZA = _W_VA + NA_WIDTH
_W_QB = _W_ZA + NA_WIDTH
_W_KB = _W_QB + SWA_WIDTH
_W_VB = _W_KB + SWA_KV_WIDTH
_W_ZB = _W_VB + SWA_KV_WIDTH
IN_WIDTH = _W_ZB + SWA_WIDTH
KT_ROWS = NA_WIDTH + SWA_KV_WIDTH


def _swa_band_tiles():
    q = np.arange(TQ)[:, None]
    k = np.arange(SWA_KEYS)[None, :] - TQ
    band = np.abs(k - q) <= SWA_WINDOW
    left_ok = np.concatenate([np.zeros((1, TQ), bool), np.ones((1, 2 * TQ), bool)], axis=1)
    right_ok = np.concatenate([np.ones((1, 2 * TQ), bool), np.zeros((1, TQ), bool)], axis=1)
    variants = np.stack([band & left_ok, band, band & right_ok])
    return np.where(variants, 0.0, MASK_VALUE).astype(np.float32)


def _rope_tables(seq):
    inv_freq = ROPE_THETA ** (-jnp.arange(0, HEAD_DIM, 2, dtype=jnp.float32) / HEAD_DIM)
    ang = jnp.arange(seq, dtype=jnp.int32).astype(jnp.float32)[:, None] * inv_freq[None, :]
    cos = jnp.cos(ang)
    sin = jnp.sin(ang)
    cos_head = jnp.concatenate([cos, cos], axis=1)
    sin_head = jnp.concatenate([-sin, sin], axis=1)
    cos_tok = jnp.concatenate([cos_head, cos_head], axis=1) * Q_SCALE
    sin_tok = jnp.concatenate([sin_head, sin_head], axis=1) * Q_SCALE
    return cos_tok, sin_tok, cos_head.T, sin_head.T


def _build_proj_weights(win_ref, wtok_ref, wkt_ref):
    bf16 = jnp.bfloat16
    wtok_ref[:, _QA[0]:_QA[1]] = win_ref[:, _W_QA:_W_QA + NA_WIDTH].astype(bf16)
    wtok_ref[:, _VA[0]:_VA[1]] = win_ref[:, _W_VA:_W_VA + NA_WIDTH].astype(bf16)
    wtok_ref[:, _QB[0]:_QB[1]] = win_ref[:, _W_QB:_W_QB + SWA_WIDTH].astype(bf16)
    wtok_ref[:, _Z[0]:_Z[0] + NA_WIDTH] = win_ref[:, _W_ZA:_W_ZA + NA_WIDTH].astype(bf16)
    wtok_ref[:, _Z[0] + NA_WIDTH:_Z[1]] = win_ref[:, _W_ZB:_W_ZB + SWA_WIDTH].astype(bf16)
    vpair = win_ref[:, _W_VB:_W_VB + SWA_KV_WIDTH]
    swapped = pltpu.roll(vpair, HEAD_DIM, 1)
    low = lax.broadcasted_iota(jnp.int32, vpair.shape, 1) < HEAD_DIM
    wtok_ref[:, _VB[0]:_VB[0] + LANES] = jnp.where(low, vpair, swapped).astype(bf16)
    wtok_ref[:, _VB[0] + LANES:_VB[1]] = jnp.where(low, swapped, vpair).astype(bf16)
    for c in range(NA_WIDTH // LANES):
        wkt_ref[c * LANES:(c + 1) * LANES, :] = win_ref[:, _W_KA + c * LANES:_W_KA + (c + 1) * LANES].T.astype(bf16)
    wkt_ref[NA_WIDTH:KT_ROWS, :] = win_ref[:, _W_KB:_W_KB + SWA_KV_WIDTH].T.astype(bf16)


def _proj_kernel(x_ref, win_ref, cos_ref, sin_ref, cost_ref, sint_ref,
                 qa_ref, va_ref, qb_ref, vb_ref, gate_ref, kta_ref, ktb_ref,
                 wtok_ref, wkt_ref, carry_ref, *, tiles_per_seq):
    bf16 = jnp.bfloat16
    i = pl.program_id(0)

    @pl.when(i == 0)
    def _():
        _build_proj_weights(win_ref, wtok_ref, wkt_ref)

    @pl.when(i % tiles_per_seq == 0)
    def _():
        carry_ref[...] = jnp.zeros_like(carry_ref)

    xb = x_ref[...].astype(bf16)

    def proj(span):
        return jnp.dot(xb, wtok_ref[:, span[0]:span[1]], preferred_element_type=jnp.float32)

    qa_ref[...] = (proj(_QA) * Q_SCALE).astype(bf16)
    va_ref[...] = proj(_VA).astype(bf16)
    vb_ref[...] = proj(_VB).astype(bf16)

    t = proj(_QB)
    cos = cos_ref[...]
    sin = sin_ref[...]
    lane = lax.broadcasted_iota(jnp.int32, (PROJ_TM, LANES), 1)
    low_half = (lane % HEAD_DIM) < (HEAD_DIM // 2)
    for c in range(SWA_WIDTH // LANES):
        tc = t[:, c * LANES:(c + 1) * LANES]
        partner = jnp.where(low_half, pltpu.roll(tc, LANES - HEAD_DIM // 2, 1), pltpu.roll(tc, HEAD_DIM // 2, 1))
        qb_ref[:, c * LANES:(c + 1) * LANES] = (tc * cos + partner * sin).astype(bf16)

    z = proj(_Z)
    gate_ref[...] = (z * jax.nn.sigmoid(z)).astype(bf16)

    kt = lax.dot_general(wkt_ref[...], xb, (((1,), (1,)), ((), ())), preferred_element_type=jnp.float32)
    n_blk = PROJ_TM // LANES
    cols = [carry_ref[...]] + [kt[0:NA_WIDTH, j * LANES:(j + 1) * LANES] for j in range(n_blk)]
    low_k = lax.broadcasted_iota(jnp.int32, (NA_WIDTH, LANES), 1) < GRID_W
    rolled = [pltpu.roll(c, GRID_W, 1) for c in cols]
    for j in range(n_blk):
        kta_ref[0, 0, j] = cols[j + 1].astype(bf16)
        kta_ref[0, 1, j] = jnp.where(low_k, rolled[j], rolled[j + 1]).astype(bf16)
    carry_ref[...] = cols[n_blk]

    cos_t = cost_ref[...]
    sin_t = sint_ref[...]
    half = HEAD_DIM // 2
    parts = []
    for g in range(SWA_KV_HEADS):
        blk = kt[NA_WIDTH + g * HEAD_DIM:NA_WIDTH + (g + 1) * HEAD_DIM]
        partner = jnp.concatenate([blk[half:], blk[:half]], axis=0)
        roped = (blk * cos_t + partner * sin_t).astype(bf16)
        parts += [roped, roped]
    kb = jnp.concatenate(parts, axis=0)
    for j in range(n_blk):
        ktb_ref[0, j] = kb[:, j * LANES:(j + 1) * LANES]


def _build_na_bias_tiles(rel_ref, tbl_ref):
    c = lax.broadcasted_iota(jnp.int32, (GRID_W, LANES), 0)
    lane = lax.broadcasted_iota(jnp.int32, (GRID_W, LANES), 1)
    kc = lane & (GRID_W - 1)
    start = jnp.clip(c - NA_KW // 2, 0, GRID_W - NA_KW)
    in_win = (kc >= start) & (kc < start + NA_KW)
    low = lane < GRID_W
    masked = jnp.full((GRID_W, LANES), MASK_VALUE, jnp.float32)
    for h in range(NA_HEADS):
        def half(dr, upper):
            row = jnp.broadcast_to(rel_ref[pl.ds(h * N_DR + dr, 1), :] * LOG2E, (GRID_W, LANES))
            shift = (LANES - (NA_KW - 1) + (GRID_W if upper else 0)) % LANES
            return pltpu.roll(row, shift, 1, stride=1, stride_axis=0)

        lower = [half(dr, False) for dr in range(N_DR - 1)]
        upper = [half(dr, True) for dr in range(1, N_DR)]
        for d in range(N_BIAS_TILES):
            tbl_ref[d, h] = jnp.where(in_win, jnp.where(low, lower[d], upper[d]), masked)


def _attn_kernel(sink_ref,
                 qa_ref, kta_ref, va_ref, qb_ref, ktb_ref, vb_ref, gate_ref, x_ref, wout_ref,
                 rel_ref, band_ref, gain_ref, beta_ref, out_ref, woutb_ref, tbl_ref, sna_ref, ssw_ref,
                 *, rows, n_blocks, n_total):
    i = pl.program_id(0)

    @pl.when(i == 0)
    def _():
        woutb_ref[...] = wout_ref[...].astype(jnp.bfloat16)
        _build_na_bias_tiles(rel_ref, tbl_ref)
        sna_ref[0] = jnp.zeros(sna_ref.shape[1:], sna_ref.dtype)
        ssw_ref[0] = jnp.zeros(ssw_ref.shape[1:], ssw_ref.dtype)

    refs = (sink_ref, qa_ref, kta_ref, va_ref, qb_ref, ktb_ref, vb_ref, gate_ref, x_ref,
            band_ref, gain_ref, beta_ref, out_ref, woutb_ref, tbl_ref)
    for parity in range(2):
        @pl.when(i % 2 == parity)
        def _():
            _attn_step(i, *refs, sna_ref.at[parity], ssw_ref.at[parity], sna_ref.at[1 - parity],
                       ssw_ref.at[1 - parity], rows=rows, n_blocks=n_blocks, n_total=n_total)


def _attn_step(i, sink_ref, qa_ref, kta_ref, va_ref, qb_ref, ktb_ref, vb_ref, gate_ref, x_ref,
               band_ref, gain_ref, beta_ref, out_ref, woutb_ref, tbl_ref,
               sna_rd, ssw_rd, sna_wr, ssw_wr, *, rows, n_blocks, n_total):
    f32, bf16 = jnp.float32, jnp.bfloat16
    kh = min(NA_KH, rows)
    n_pairs = NA_HEADS // 2
    lane = lax.broadcasted_iota(jnp.int32, (GRID_W, LANES), 1)
    low = lane < HEAD_DIM
    low_tq = lax.broadcasted_iota(jnp.int32, (TQ, LANES), 1) < HEAD_DIM

    def window_start(t, qrow):
        r = Q_ROWS * t + qrow
        return r, jnp.clip(r - kh // 2, 0, rows - kh)

    t = jnp.maximum(i - 1, 0) % n_blocks
    jl = jnp.maximum(t - 1, 0)
    jr = jnp.minimum(t + 1, n_blocks - 1)

    na_e, na_v = [], []
    for qrow in range(Q_ROWS):
        _, r0 = window_start(t, qrow)
        key0 = pl.multiple_of(r0 * GRID_W, GRID_W)
        for p in range(n_pairs):
            s_ref = sna_rd.at[qrow * n_pairs + p]
            m = jnp.max(s_ref[...], axis=-1, keepdims=True)
            na_e.append(jnp.exp2(s_ref[...] - m).astype(bf16))
            na_v.append(va_ref[0, pl.ds(key0, NA_KEYS), p * LANES:(p + 1) * LANES])
    sw_e, sw_sink, sw_v = [], [], []
    for g in range(SWA_KV_HEADS):
        s = ssw_rd[g]
        sink = jnp.concatenate(
            [jnp.full((TQ, 1), sink_ref[SWA_GROUP * g + j] * LOG2E, f32) for j in range(SWA_GROUP)], axis=0)
        m = jnp.maximum(jnp.max(s, axis=-1, keepdims=True), sink)
        sw_e.append(jnp.exp2(s - m).astype(bf16))
        sw_sink.append(jnp.exp2(sink - m))
        sw_v.append(jnp.concatenate(
            [vb_ref[0, pl.ds(pl.multiple_of(j * TQ, TQ), TQ), g * LANES:(g + 1) * LANES] for j in (jl, t, jr)],
            axis=0))

    na_out = []
    for u in range(Q_ROWS * n_pairs):
        v1 = jnp.concatenate([na_v[u], jnp.ones_like(na_v[u])], axis=1)
        o = jnp.dot(na_e[u], v1, preferred_element_type=f32)
        o = o[:, :LANES] * (1.0 / o[:, LANES:])
        na_out.append(jnp.where(low, o[:GRID_W], o[GRID_W:]))
    mixed_cols = [jnp.concatenate([na_out[qrow * n_pairs + p] for qrow in range(Q_ROWS)], axis=0)
                  for p in range(n_pairs)]
    yb_cols = [None] * (SWA_Q_HEADS // 2)
    for g in range(SWA_KV_HEADS):
        v1 = jnp.concatenate([sw_v[g], jnp.ones_like(sw_v[g])], axis=1)
        o = jnp.dot(sw_e[g], v1, preferred_element_type=f32)
        o = o[:, :LANES] * (1.0 / (o[:, LANES:] + sw_sink[g]))
        for pp in range(SWA_GROUP // 2):
            yb_cols[2 * g + pp] = jnp.where(low_tq, o[(2 * pp) * TQ:(2 * pp + 1) * TQ],
                                            o[(2 * pp + 1) * TQ:(2 * pp + 2) * TQ])

    mixed = jnp.concatenate(mixed_cols + yb_cols, axis=1)
    mixed = (mixed * gate_ref[...].astype(f32)).astype(bf16)
    y = jnp.dot(mixed, woutb_ref[...], preferred_element_type=f32)
    r = DEEPNORM_ALPHA * x_ref[...] + y
    mu = jnp.mean(r, axis=-1, keepdims=True)
    d = r - mu
    var = jnp.mean(d * d, axis=-1, keepdims=True)
    out_ref[...] = d * lax.rsqrt(var + LN_EPS) * gain_ref[...] + beta_ref[...]

    t1 = jnp.minimum(i, n_total - 1) % n_blocks
    for qrow in range(Q_ROWS):
        r, r0 = window_start(t1, qrow)
        par = r0 & 1
        d0 = r0 - r + (NA_KH - 1)
        for p in range(n_pairs):
            cols = slice(p * LANES, (p + 1) * LANES)
            qp = qa_ref[qrow * GRID_W:(qrow + 1) * GRID_W, cols]
            zero = jnp.zeros_like(qp)
            q2 = jnp.concatenate([jnp.where(low, qp, zero), jnp.where(low, zero, qp)], axis=0)
            kt = jnp.concatenate([kta_ref[0, par, (r0 + 2 * j + 1) // 2, cols, :] for j in range(KEY_PAIRS)],
                                 axis=1)
            bias = jnp.concatenate(
                [jnp.concatenate([tbl_ref[d0 + 2 * j, 2 * p + hp] for j in range(KEY_PAIRS)], axis=1)
                 for hp in range(2)], axis=0)
            sna_wr[qrow * n_pairs + p] = jnp.dot(q2, kt, preferred_element_type=f32) + bias
    jl1 = jnp.maximum(t1 - 1, 0)
    jr1 = jnp.minimum(t1 + 1, n_blocks - 1)
    variant = jnp.where(t1 == 0, 0, jnp.where(t1 == n_blocks - 1, 2, 1))
    band = band_ref[variant]
    band4 = jnp.concatenate([band] * SWA_GROUP, axis=0)
    for g in range(SWA_KV_HEADS):
        kt = jnp.concatenate([ktb_ref[0, j, g * LANES:(g + 1) * LANES, :] for j in (jl1, t1, jr1)], axis=1)
        qs = []
        for pp in range(SWA_GROUP // 2):
            qp = qb_ref[:, (2 * g + pp) * LANES:(2 * g + pp + 1) * LANES]
            zero = jnp.zeros_like(qp)
            qs += [jnp.where(low_tq, qp, zero), jnp.where(low_tq, zero, qp)]
        q4 = jnp.concatenate(qs, axis=0)
        ssw_wr[g] = jnp.dot(q4, kt, preferred_element_type=f32) + band4


def kernel(x, w_in, rel_pos_bias, sink_logits, w_out, ln_gain, ln_bias):
    B, S, D = x.shape
    assert D == D_MODEL and S % PROJ_TM == 0 and S % GRID_W == 0
    rows = S // GRID_W
    assert rows >= NA_KH and rows % Q_ROWS == 0
    n_blocks = S // TQ
    n_tok = B * S
    bf16 = jnp.bfloat16
    x2 = x.reshape(n_tok, D)

    cos_tok, sin_tok, cos_feat, sin_feat = _rope_tables(S)
    band = jnp.asarray(_swa_band_tiles())

    for layer in range(DEPTH):
        tiles_per_seq = S // PROJ_TM
        blk128 = PROJ_TM // LANES
        qa, va, qb, vb, gate, kta, ktb = pl.pallas_call(
            functools.partial(_proj_kernel, tiles_per_seq=tiles_per_seq),
            grid=(n_tok // PROJ_TM,),
            in_specs=[
                pl.BlockSpec((PROJ_TM, D), lambda i: (i, 0)),
                pl.BlockSpec((D, IN_WIDTH), lambda i: (0, 0), pipeline_mode=pl.Buffered(1)),
                pl.BlockSpec((PROJ_TM, LANES), lambda i: (i % tiles_per_seq, 0)),
                pl.BlockSpec((PROJ_TM, LANES), lambda i: (i % tiles_per_seq, 0)),
                pl.BlockSpec((HEAD_DIM, PROJ_TM), lambda i: (0, i % tiles_per_seq)),
                pl.BlockSpec((HEAD_DIM, PROJ_TM), lambda i: (0, i % tiles_per_seq)),
            ],
            out_specs=[
                pl.BlockSpec((PROJ_TM, NA_WIDTH), lambda i: (i, 0)),
                pl.BlockSpec((PROJ_TM, NA_WIDTH), lambda i: (i, 0)),
                pl.BlockSpec((PROJ_TM, SWA_WIDTH), lambda i: (i, 0)),
                pl.BlockSpec((PROJ_TM, 2 * SWA_KV_WIDTH), lambda i: (i, 0)),
                pl.BlockSpec((PROJ_TM, MIX_WIDTH), lambda i: (i, 0)),
                pl.BlockSpec((1, 2, blk128, NA_WIDTH, LANES),
                             lambda i: (i // tiles_per_seq, 0, i % tiles_per_seq, 0, 0)),
                pl.BlockSpec((1, blk128, 2 * SWA_KV_WIDTH, LANES),
                             lambda i: (i // tiles_per_seq, i % tiles_per_seq, 0, 0)),
            ],
            out_shape=[
                jax.ShapeDtypeStruct((n_tok, NA_WIDTH), bf16),
                jax.ShapeDtypeStruct((n_tok, NA_WIDTH), bf16),
                jax.ShapeDtypeStruct((n_tok, SWA_WIDTH), bf16),
                jax.ShapeDtypeStruct((n_tok, 2 * SWA_KV_WIDTH), bf16),
                jax.ShapeDtypeStruct((n_tok, MIX_WIDTH), bf16),
                jax.ShapeDtypeStruct((B, 2, S // LANES, NA_WIDTH, LANES), bf16),
                jax.ShapeDtypeStruct((B, S // LANES, 2 * SWA_KV_WIDTH, LANES), bf16),
            ],
            scratch_shapes=[pltpu.VMEM((D, TOK_WIDTH), bf16), pltpu.VMEM((KT_ROWS, D), bf16),
                            pltpu.VMEM((NA_WIDTH, LANES), jnp.float32)],
            compiler_params=pltpu.CompilerParams(
                dimension_semantics=("arbitrary",), vmem_limit_bytes=56 * 1024 * 1024),
            name="in_proj",
        )(x2, w_in[layer], cos_tok, sin_tok, cos_feat, sin_feat)

        rel_rows = jnp.pad(rel_pos_bias[layer].astype(jnp.float32).reshape(NA_HEADS * N_DR, N_DC),
                           ((0, 0), (0, LANES - N_DC)))
        n_total = B * n_blocks
        cur = lambda i: jnp.minimum(i, n_total - 1)
        prv = lambda i: jnp.maximum(i - 1, 0)
        grid_spec = pltpu.PrefetchScalarGridSpec(
            num_scalar_prefetch=1,
            grid=(n_total + 1,),
            in_specs=[
                pl.BlockSpec((TQ, NA_WIDTH), lambda i, *_: (cur(i), 0)),
                pl.BlockSpec((1, 2, S // LANES, NA_WIDTH, LANES), lambda i, *_: (cur(i) // n_blocks, 0, 0, 0, 0)),
                pl.BlockSpec((1, S, NA_WIDTH), lambda i, *_: (prv(i) // n_blocks, 0, 0)),
                pl.BlockSpec((TQ, SWA_WIDTH), lambda i, *_: (cur(i), 0)),
                pl.BlockSpec((1, S // LANES, 2 * SWA_KV_WIDTH, LANES), lambda i, *_: (cur(i) // n_blocks, 0, 0, 0)),
                pl.BlockSpec((1, S, 2 * SWA_KV_WIDTH), lambda i, *_: (prv(i) // n_blocks, 0, 0)),
                pl.BlockSpec((TQ, MIX_WIDTH), lambda i, *_: (prv(i), 0)),
                pl.BlockSpec((TQ, D), lambda i, *_: (prv(i), 0)),
                pl.BlockSpec((MIX_WIDTH, D), lambda i, *_: (0, 0), pipeline_mode=pl.Buffered(1)),
                pl.BlockSpec((NA_HEADS * N_DR, LANES), lambda i, *_: (0, 0)),
                pl.BlockSpec((3, TQ, SWA_KEYS), lambda i, *_: (0, 0, 0)),
                pl.BlockSpec((1, D), lambda i, *_: (0, 0)),
                pl.BlockSpec((1, D), lambda i, *_: (0, 0)),
            ],
            out_specs=pl.BlockSpec((TQ, D), lambda i, *_: (prv(i), 0)),
            scratch_shapes=[pltpu.VMEM((MIX_WIDTH, D), bf16),
                            pltpu.VMEM((N_BIAS_TILES, NA_HEADS, GRID_W, 2 * GRID_W), jnp.float32),
                            pltpu.VMEM((2, Q_ROWS * NA_HEADS // 2, TQ, NA_KEYS), jnp.float32),
                            pltpu.VMEM((2, SWA_KV_HEADS, SWA_GROUP * TQ, SWA_KEYS), jnp.float32)],
        )
        x2 = pl.pallas_call(
            functools.partial(_attn_kernel, rows=rows, n_blocks=n_blocks, n_total=n_total),
            grid_spec=grid_spec,
            out_shape=jax.ShapeDtypeStruct((n_tok, D), jnp.float32),
            compiler_params=pltpu.CompilerParams(
                dimension_semantics=("arbitrary",), vmem_limit_bytes=56 * 1024 * 1024),
            name="attn_out",
        )(sink_logits[layer].astype(jnp.float32),
          qa, kta, va.reshape(B, S, NA_WIDTH), qb, ktb, vb.reshape(B, S, 2 * SWA_KV_WIDTH), gate, x2,
          w_out[layer], rel_rows, band, ln_gain[layer].reshape(1, D), ln_bias[layer].reshape(1, D))
    return x2.reshape(B, S, D)
```

```python
import functools

import numpy as np
import jax
import jax.numpy as jnp
from jax import lax
from jax.experimental import pallas as pl
from jax.experimental.pallas import tpu as pltpu

D_MODEL = 1024
HEAD_DIM = 64
NA_HEADS = 8
NA_WIDTH = NA_HEADS * HEAD_DIM
NA_KH = 8
NA_KW = 16
GRID_W = 64
SWA_Q_HEADS = 8
SWA_KV_HEADS = 2
SWA_GROUP = SWA_Q_HEADS // SWA_KV_HEADS
SWA_WIDTH = SWA_Q_HEADS * HEAD_DIM
SWA_KV_WIDTH = SWA_KV_HEADS * HEAD_DIM
SWA_WINDOW = 128
ROPE_THETA = 10000.0
MIX_WIDTH = NA_WIDTH + SWA_WIDTH
LN_EPS = 1e-5
MASK_VALUE = -1e30
DEPTH = 1
DEEPNORM_ALPHA = (2.0 * DEPTH) ** 0.25
LOG2E = 1.4426950408889634
Q_SCALE = HEAD_DIM ** -0.5 * LOG2E
N_DR = 2 * NA_KH - 1
N_DC = 2 * NA_KW - 1

LANES = 128
BF16_ROWS = 16
PAIR = 2 * HEAD_DIM
assert PAIR == LANES and 2 * GRID_W == LANES and SWA_KV_WIDTH == LANES

PROJ_TM = 1024
TQ = 128
Q_ROWS = TQ // GRID_W
NA_KEYS = NA_KH * GRID_W
KEY_PAIRS = NA_KEYS // LANES
N_BIAS_TILES = N_DR - 1
SWA_KEYS = 3 * TQ

_QA = (0, NA_WIDTH)
_VA = (_QA[1], _QA[1] + NA_WIDTH)
_KB = (_VA[1], _VA[1] + SWA_KV_WIDTH)
_Z = (_KB[1], _KB[1] + MIX_WIDTH)
TOK_WIDTH = _Z[1]
_T_KA = (0, NA_WIDTH)
_T_QB = (_T_KA[1], _T_KA[1] + SWA_WIDTH)
_T_VB = (_T_QB[1], _T_QB[1] + SWA_KV_WIDTH)
FEAT_ROWS = _T_VB[1]
_W_QA = 0
_W_KA = _W_QA + NA_WIDTH
_W_VA = _W_KA + NA_WIDTH
_W_ZA = _W_VA + NA_WIDTH
_W_QB = _W_ZA + NA_WIDTH
_W_KB = _W_QB + SWA_WIDTH
_W_VB = _W_KB + SWA_KV_WIDTH
_W_ZB = _W_VB + SWA_KV_WIDTH
IN_WIDTH = _W_ZB + SWA_WIDTH


def _swa_band_tiles():
    q = np.arange(TQ)[None, :]
    k = np.arange(SWA_KEYS)[:, None] - TQ
    band = np.abs(k - q) <= SWA_WINDOW
    left_ok = np.concatenate([np.zeros((TQ, 1), bool), np.ones((2 * TQ, 1), bool)], axis=0)
    right_ok = np.concatenate([np.ones((2 * TQ, 1), bool), np.zeros((TQ, 1), bool)], axis=0)
    variants = np.stack([band & left_ok, band, band & right_ok])
    return np.where(variants, 0.0, MASK_VALUE).astype(np.float32)


def _rope_tables(seq):
    inv_freq = ROPE_THETA ** (-jnp.arange(0, HEAD_DIM, 2, dtype=jnp.float32) / HEAD_DIM)
    ang = jnp.arange(seq, dtype=jnp.int32).astype(jnp.float32)[:, None] * inv_freq[None, :]
    cos = jnp.cos(ang)
    sin = jnp.sin(ang)
    cos_head = jnp.concatenate([cos, cos], axis=1)
    sin_head = jnp.concatenate([-sin, sin], axis=1)
    cos_tok = jnp.concatenate([cos_head, cos_head], axis=1)
    sin_tok = jnp.concatenate([sin_head, sin_head], axis=1)
    return cos_tok, sin_tok, cos_head.T * Q_SCALE, sin_head.T * Q_SCALE


def _build_proj_weights(win_ref, wtok_ref, wft_ref):
    bf16 = jnp.bfloat16
    wtok_ref[:, _QA[0]:_QA[1]] = win_ref[:, _W_QA:_W_QA + NA_WIDTH].astype(bf16)
    wtok_ref[:, _VA[0]:_VA[1]] = win_ref[:, _W_VA:_W_VA + NA_WIDTH].astype(bf16)
    wtok_ref[:, _KB[0]:_KB[1]] = win_ref[:, _W_KB:_W_KB + SWA_KV_WIDTH].astype(bf16)
    wtok_ref[:, _Z[0]:_Z[0] + NA_WIDTH] = win_ref[:, _W_ZA:_W_ZA + NA_WIDTH].astype(bf16)
    wtok_ref[:, _Z[0] + NA_WIDTH:_Z[1]] = win_ref[:, _W_ZB:_W_ZB + SWA_WIDTH].astype(bf16)
    for dst, src, width in ((_T_KA[0], _W_KA, NA_WIDTH), (_T_QB[0], _W_QB, SWA_WIDTH), (_T_VB[0], _W_VB, SWA_KV_WIDTH)):
        for c in range(width // LANES):
            wft_ref[dst + c * LANES:dst + (c + 1) * LANES, :] = \
                win_ref[:, src + c * LANES:src + (c + 1) * LANES].T.astype(bf16)


def _proj_kernel(x_ref, win_ref, cos_ref, sin_ref, cost_ref, sint_ref,
                 qa_ref, va_ref, kb_ref, gate_ref, kta_ref, qbt_ref, vbt_ref,
                 wtok_ref, wft_ref, carry_ref, *, tiles_per_seq):
    bf16 = jnp.bfloat16
    i = pl.program_id(0)
    half = HEAD_DIM // 2
    n_blk = PROJ_TM // LANES

    @pl.when(i == 0)
    def _():
        _build_proj_weights(win_ref, wtok_ref, wft_ref)

    @pl.when(i % tiles_per_seq == 0)
    def _():
        carry_ref[...] = jnp.zeros_like(carry_ref)

    xb = x_ref[...].astype(bf16)

    def proj(span):
        return jnp.dot(xb, wtok_ref[:, span[0]:span[1]], preferred_element_type=jnp.float32)

    qa_ref[...] = (proj(_QA) * Q_SCALE).astype(bf16)
    va_ref[...] = proj(_VA).astype(bf16)

    kb = proj(_KB)
    lane = lax.broadcasted_iota(jnp.int32, (PROJ_TM, LANES), 1)
    low_half = (lane % HEAD_DIM) < half
    partner = jnp.where(low_half, pltpu.roll(kb, LANES - half, 1), pltpu.roll(kb, half, 1))
    kb_ref[...] = (kb * cos_ref[...] + partner * sin_ref[...]).astype(bf16)

    z = proj(_Z)
    gate_ref[...] = (z * jax.nn.sigmoid(z)).astype(bf16)

    ft = lax.dot_general(wft_ref[...], xb, (((1,), (1,)), ((), ())), preferred_element_type=jnp.float32)
    cols = [carry_ref[...]] + [ft[_T_KA[0]:_T_KA[1], j * LANES:(j + 1) * LANES] for j in range(n_blk)]
    low_k = lax.broadcasted_iota(jnp.int32, (NA_WIDTH, LANES), 1) < GRID_W
    rolled = [pltpu.roll(c, GRID_W, 1) for c in cols]
    for j in range(n_blk):
        kta_ref[0, 0, j] = cols[j + 1].astype(bf16)
        kta_ref[0, 1, j] = jnp.where(low_k, rolled[j], rolled[j + 1]).astype(bf16)
    carry_ref[...] = cols[n_blk]

    cos_t = cost_ref[...]
    sin_t = sint_ref[...]
    for h in range(SWA_Q_HEADS):
        blk = ft[_T_QB[0] + h * HEAD_DIM:_T_QB[0] + (h + 1) * HEAD_DIM]
        partner_t = jnp.concatenate([blk[half:], blk[:half]], axis=0)
        qbt_ref[0, h * HEAD_DIM:(h + 1) * HEAD_DIM, :] = (blk * cos_t + partner_t * sin_t).astype(bf16)

    vt = ft[_T_VB[0]:_T_VB[1]].astype(bf16)
    for j in range(n_blk):
        vbt_ref[0, j] = vt[:, j * LANES:(j + 1) * LANES]


def _build_na_bias_tiles(rel_ref, tbl_ref):
    c = lax.broadcasted_iota(jnp.int32, (GRID_W, LANES), 0)
    lane = lax.broadcasted_iota(jnp.int32, (GRID_W, LANES), 1)
    kc = lane & (GRID_W - 1)
    start = jnp.clip(c - NA_KW // 2, 0, GRID_W - NA_KW)
    in_win = (kc >= start) & (kc < start + NA_KW)
    low = lane < GRID_W
    masked = jnp.full((GRID_W, LANES), MASK_VALUE, jnp.float32)
    for h in range(NA_HEADS):
        def half(dr, upper):
            row = jnp.broadcast_to(rel_ref[pl.ds(h * N_DR + dr, 1), :] * LOG2E, (GRID_W, LANES))
            shift = (LANES - (NA_KW - 1) + (GRID_W if upper else 0)) % LANES
            return pltpu.roll(row, shift, 1, stride=1, stride_axis=0)

        lower = [half(dr, False) for dr in range(N_DR - 1)]
        upper = [half(dr, True) for dr in range(1, N_DR)]
        for d in range(N_BIAS_TILES):
            tbl_ref[d, h] = jnp.where(in_win, jnp.where(low, lower[d], upper[d]), masked)


def _attn_kernel(sink_ref,
                 qa_ref, kta_ref, va_ref, qbt_ref, kb_ref, vbt_ref, gate_ref, x_ref, wout_ref,
                 rel_ref, band_ref, gain_ref, beta_ref, out_ref, woutb_ref, tbl_ref, sna_ref, ssw_ref,
                 *, rows, n_blocks, n_total):
    i = pl.program_id(0)

    @pl.when(i == 0)
    def _():
        woutb_ref[...] = wout_ref[...].astype(jnp.bfloat16)
        _build_na_bias_tiles(rel_ref, tbl_ref)
        sna_ref[0] = jnp.zeros(sna_ref.shape[1:], sna_ref.dtype)
        ssw_ref[0] = jnp.zeros(ssw_ref.shape[1:], ssw_ref.dtype)

    refs = (sink_ref, qa_ref, kta_ref, va_ref, qbt_ref, kb_ref, vbt_ref, gate_ref, x_ref,
            band_ref, gain_ref, beta_ref, out_ref, woutb_ref, tbl_ref)
    for parity in range(2):
        @pl.when(i % 2 == parity)
        def _():
            _attn_step(i, *refs, sna_ref.at[parity], ssw_ref.at[parity], sna_ref.at[1 - parity],
                       ssw_ref.at[1 - parity], rows=rows, n_blocks=n_blocks, n_total=n_total)


def _attn_step(i, sink_ref, qa_ref, kta_ref, va_ref, qbt_ref, kb_ref, vbt_ref, gate_ref, x_ref,
               band_ref, gain_ref, beta_ref, out_ref, woutb_ref, tbl_ref,
               sna_rd, ssw_rd, sna_wr, ssw_wr, *, rows, n_blocks, n_total):
    f32, bf16 = jnp.float32, jnp.bfloat16
    kh = min(NA_KH, rows)
    n_pairs = NA_HEADS // 2
    lane = lax.broadcasted_iota(jnp.int32, (GRID_W, LANES), 1)
    low = lane < HEAD_DIM

    def window_start(t, qrow):
        r = Q_ROWS * t + qrow
        return r, jnp.clip(r - kh // 2, 0, rows - kh)

    t = jnp.maximum(i - 1, 0) % n_blocks
    jl = jnp.maximum(t - 1, 0)
    jr = jnp.minimum(t + 1, n_blocks - 1)

    na_e, na_v = [], []
    for qrow in range(Q_ROWS):
        _, r0 = window_start(t, qrow)
        key0 = pl.multiple_of(r0 * GRID_W, GRID_W)
        for p in range(n_pairs):
            s_ref = sna_rd.at[qrow * n_pairs + p]
            m = jnp.max(s_ref[...], axis=-1, keepdims=True)
            na_e.append(jnp.exp2(s_ref[...] - m).astype(bf16))
            na_v.append(va_ref[0, pl.ds(key0, NA_KEYS), p * LANES:(p + 1) * LANES])
    sw_e, sw_sink, sw_v = [], [], []
    for g in range(SWA_KV_HEADS):
        s_ref = ssw_rd.at[g]
        sink = jnp.concatenate(
            [jnp.full((1, TQ), sink_ref[SWA_GROUP * g + j] * LOG2E, f32) for j in range(SWA_GROUP)], axis=1)
        m = jnp.maximum(jnp.max(s_ref[...], axis=0, keepdims=True), sink)
        sw_e.append(jnp.exp2(s_ref[...] - m).astype(bf16))
        sw_sink.append(jnp.exp2(sink - m))
        sw_v.append(jnp.concatenate(
            [vbt_ref[0, j, g * HEAD_DIM:(g + 1) * HEAD_DIM, :] for j in (jl, t, jr)], axis=1))

    na_out = []
    for u in range(Q_ROWS * n_pairs):
        v1 = jnp.concatenate([na_v[u], jnp.ones_like(na_v[u])], axis=1)
        o = jnp.dot(na_e[u], v1, preferred_element_type=f32)
        o = o[:, :LANES] * (1.0 / o[:, LANES:])
        na_out.append(jnp.where(low, o[:GRID_W], o[GRID_W:]))
    mixed_cols = [jnp.concatenate([na_out[qrow * n_pairs + p] for qrow in range(Q_ROWS)], axis=0)
                  for p in range(n_pairs)]
    for g in range(SWA_KV_HEADS):
        v1 = jnp.concatenate([sw_v[g], jnp.ones((BF16_ROWS, SWA_KEYS), bf16)], axis=0)
        o = jnp.dot(v1, sw_e[g], preferred_element_type=f32)
        o = o[:HEAD_DIM] * (1.0 / (o[HEAD_DIM:HEAD_DIM + 1] + sw_sink[g]))
        for pp in range(SWA_GROUP // 2):
            both = jnp.concatenate([o[:, (2 * pp) * TQ:(2 * pp + 1) * TQ],
                                    o[:, (2 * pp + 1) * TQ:(2 * pp + 2) * TQ]], axis=0)
            mixed_cols.append(both.T)

    mixed = jnp.concatenate(mixed_cols, axis=1)
    mixed = (mixed * gate_ref[...].astype(f32)).astype(bf16)
    y = jnp.dot(mixed, woutb_ref[...], preferred_element_type=f32)
    r = DEEPNORM_ALPHA * x_ref[...] + y
    mu = jnp.mean(r, axis=-1, keepdims=True)
    d = r - mu
    var = jnp.mean(d * d, axis=-1, keepdims=True)
    out_ref[...] = d * lax.rsqrt(var + LN_EPS) * gain_ref[...] + beta_ref[...]

    t1 = jnp.minimum(i, n_total - 1) % n_blocks
    for qrow in range(Q_ROWS):
        r, r0 = window_start(t1, qrow)
        par = r0 & 1
        d0 = r0 - r + (NA_KH - 1)
        for p in range(n_pairs):
            cols = slice(p * LANES, (p + 1) * LANES)
            qp = qa_ref[qrow * GRID_W:(qrow + 1) * GRID_W, cols]
            zero = jnp.zeros_like(qp)
            q2 = jnp.concatenate([jnp.where(low, qp, zero), jnp.where(low, zero, qp)], axis=0)
            kt = jnp.concatenate([kta_ref[0, par, (r0 + 2 * j + 1) // 2, cols, :] for j in range(KEY_PAIRS)],
                                 axis=1)
            bias = jnp.concatenate(
                [jnp.concatenate([tbl_ref[d0 + 2 * j, 2 * p + hp] for j in range(KEY_PAIRS)], axis=1)
                 for hp in range(2)], axis=0)
            sna_wr[qrow * n_pairs + p] = jnp.dot(q2, kt, preferred_element_type=f32) + bias
    jl1 = jnp.maximum(t1 - 1, 0)
    jr1 = jnp.minimum(t1 + 1, n_blocks - 1)
    variant = jnp.where(t1 == 0, 0, jnp.where(t1 == n_blocks - 1, 2, 1))
    band = band_ref[variant]
    band4 = jnp.concatenate([band] * SWA_GROUP, axis=1)
    k3 = jnp.concatenate([kb_ref[0, pl.ds(pl.multiple_of(j * TQ, TQ), TQ), :] for j in (jl1, t1, jr1)],
                         axis=0)
    for g in range(SWA_KV_HEADS):
        qt = jnp.concatenate(
            [qbt_ref[0, (SWA_GROUP * g + j) * HEAD_DIM:(SWA_GROUP * g + j + 1) * HEAD_DIM, :]
             for j in range(SWA_GROUP)], axis=1)
        parts = [jnp.zeros_like(qt)] * SWA_KV_HEADS
        parts[g] = qt
        ssw_wr[g] = jnp.dot(k3, jnp.concatenate(parts, axis=0), preferred_element_type=f32) + band4


def kernel(x, w_in, rel_pos_bias, sink_logits, w_out, ln_gain, ln_bias):
    B, S, D = x.shape
    assert D == D_MODEL and S % PROJ_TM == 0 and S % GRID_W == 0
    rows = S // GRID_W
    assert rows >= NA_KH and rows % Q_ROWS == 0
    n_blocks = S // TQ
    n_tok = B * S
    bf16 = jnp.bfloat16
    x2 = x.reshape(n_tok, D)

    cos_tok, sin_tok, cos_feat, sin_feat = _rope_tables(S)
    band = jnp.asarray(_swa_band_tiles())

    for layer in range(DEPTH):
        tiles_per_seq = S // PROJ_TM
        blk128 = PROJ_TM // LANES
        seq_tile = lambda i: (i // tiles_per_seq, i % tiles_per_seq)
        qa, va, kb, gate, kta, qbt, vbt = pl.pallas_call(
            functools.partial(_proj_kernel, tiles_per_seq=tiles_per_seq),
            grid=(n_tok // PROJ_TM,),
            in_specs=[
                pl.BlockSpec((PROJ_TM, D), lambda i: (i, 0)),
                pl.BlockSpec((D, IN_WIDTH), lambda i: (0, 0), pipeline_mode=pl.Buffered(1)),
                pl.BlockSpec((PROJ_TM, LANES), lambda i: (i % tiles_per_seq, 0)),
                pl.BlockSpec((PROJ_TM, LANES), lambda i: (i % tiles_per_seq, 0)),
                pl.BlockSpec((HEAD_DIM, PROJ_TM), lambda i: (0, i % tiles_per_seq)),
                pl.BlockSpec((HEAD_DIM, PROJ_TM), lambda i: (0, i % tiles_per_seq)),
            ],
            out_specs=[
                pl.BlockSpec((PROJ_TM, NA_WIDTH), lambda i: (i, 0)),
                pl.BlockSpec((PROJ_TM, NA_WIDTH), lambda i: (i, 0)),
                pl.BlockSpec((PROJ_TM, SWA_KV_WIDTH), lambda i: (i, 0)),
                pl.BlockSpec((PROJ_TM, MIX_WIDTH), lambda i: (i, 0)),
                pl.BlockSpec((1, 2, blk128, NA_WIDTH, LANES), lambda i: (seq_tile(i)[0], 0, seq_tile(i)[1], 0, 0)),
                pl.BlockSpec((1, SWA_WIDTH, PROJ_TM), lambda i: (seq_tile(i)[0], 0, seq_tile(i)[1])),
                pl.BlockSpec((1, blk128, SWA_KV_WIDTH, LANES), lambda i: (seq_tile(i)[0], seq_tile(i)[1], 0, 0)),
            ],
            out_shape=[
                jax.ShapeDtypeStruct((n_tok, NA_WIDTH), bf16),
                jax.ShapeDtypeStruct((n_tok, NA_WIDTH), bf16),
                jax.ShapeDtypeStruct((n_tok, SWA_KV_WIDTH), bf16),
                jax.ShapeDtypeStruct((n_tok, MIX_WIDTH), bf16),
                jax.ShapeDtypeStruct((B, 2, S // LANES, NA_WIDTH, LANES), bf16),
                jax.ShapeDtypeStruct((B, SWA_WIDTH, S), bf16),
                jax.ShapeDtypeStruct((B, S // LANES, SWA_KV_WIDTH, LANES), bf16),
            ],
            scratch_shapes=[pltpu.VMEM((D, TOK_WIDTH), bf16), pltpu.VMEM((FEAT_ROWS, D), bf16),
                            pltpu.VMEM((NA_WIDTH, LANES), jnp.float32)],
            compiler_params=pltpu.CompilerParams(
                dimension_semantics=("arbitrary",), vmem_limit_bytes=56 * 1024 * 1024),
            name="in_proj",
        )(x2, w_in[layer], cos_tok, sin_tok, cos_feat, sin_feat)

        rel_rows = jnp.pad(rel_pos_bias[layer].astype(jnp.float32).reshape(NA_HEADS * N_DR, N_DC),
                           ((0, 0), (0, LANES - N_DC)))
        n_total = B * n_blocks
        cur = lambda i: jnp.minimum(i, n_total - 1)
        prv = lambda i: jnp.maximum(i - 1, 0)
        grid_spec = pltpu.PrefetchScalarGridSpec(
            num_scalar_prefetch=1,
            grid=(n_total + 1,),
            in_specs=[
                pl.BlockSpec((TQ, NA_WIDTH), lambda i, *_: (cur(i), 0)),
                pl.BlockSpec((1, 2, S // LANES, NA_WIDTH, LANES), lambda i, *_: (cur(i) // n_blocks, 0, 0, 0, 0)),
                pl.BlockSpec((1, S, NA_WIDTH), lambda i, *_: (prv(i) // n_blocks, 0, 0)),
                pl.BlockSpec((1, SWA_WIDTH, TQ), lambda i, *_: (cur(i) // n_blocks, 0, cur(i) % n_blocks)),
                pl.BlockSpec((1, S, SWA_KV_WIDTH), lambda i, *_: (cur(i) // n_blocks, 0, 0)),
                pl.BlockSpec((1, S // LANES, SWA_KV_WIDTH, LANES), lambda i, *_: (prv(i) // n_blocks, 0, 0, 0)),
                pl.BlockSpec((TQ, MIX_WIDTH), lambda i, *_: (prv(i), 0)),
                pl.BlockSpec((TQ, D), lambda i, *_: (prv(i), 0)),
                pl.BlockSpec((MIX_WIDTH, D), lambda i, *_: (0, 0), pipeline_mode=pl.Buffered(1)),
                pl.BlockSpec((NA_HEADS * N_DR, LANES), lambda i, *_: (0, 0)),
                pl.BlockSpec((3, SWA_KEYS, TQ), lambda i, *_: (0, 0, 0)),
                pl.BlockSpec((1, D), lambda i, *_: (0, 0)),
                pl.BlockSpec((1, D), lambda i, *_: (0, 0)),
            ],
            out_specs=pl.BlockSpec((TQ, D), lambda i, *_: (prv(i), 0)),
            scratch_shapes=[pltpu.VMEM((MIX_WIDTH, D), bf16),
                            pltpu.VMEM((N_BIAS_TILES, NA_HEADS, GRID_W, 2 * GRID_W), jnp.float32),
                            pltpu.VMEM((2, Q_ROWS * NA_HEADS // 2, TQ, NA_KEYS), jnp.float32),
                            pltpu.VMEM((2, SWA_KV_HEADS, SWA_KEYS, SWA_GROUP * TQ), jnp.float32)],
        )
        x2 = pl.pallas_call(
            functools.partial(_attn_kernel, rows=rows, n_blocks=n_blocks, n_total=n_total),
            grid_spec=grid_spec,
            out_shape=jax.ShapeDtypeStruct((n_tok, D), jnp.float32),
            compiler_params=pltpu.CompilerParams(
                dimension_semantics=("arbitrary",), vmem_limit_bytes=56 * 1024 * 1024),
            name="attn_out",
        )(sink_logits[layer].astype(jnp.float32),
          qa, kta, va.reshape(B, S, NA_WIDTH), qbt, kb.reshape(B, S, SWA_KV_WIDTH), vbt, gate, x2,
          w_out[layer], rel_rows, band, ln_gain[layer].reshape(1, D), ln_bias[layer].reshape(1, D))
    return x2.reshape(B, S, D)
```

```python
import functools

import numpy as np
import jax
import jax.numpy as jnp
from jax import lax
from jax.experimental import pallas as pl
from jax.experimental.pallas import tpu as pltpu

D_MODEL = 1024
HEAD_DIM = 64
NA_HEADS = 8
NA_WIDTH = NA_HEADS * HEAD_DIM
NA_KH = 8
NA_KW = 16
GRID_W = 64
SWA_Q_HEADS = 8
SWA_KV_HEADS = 2
SWA_GROUP = SWA_Q_HEADS // SWA_KV_HEADS
SWA_WIDTH = SWA_Q_HEADS * HEAD_DIM
SWA_KV_WIDTH = SWA_KV_HEADS * HEAD_DIM
SWA_WINDOW = 128
ROPE_THETA = 10000.0
MIX_WIDTH = NA_WIDTH + SWA_WIDTH
LN_EPS = 1e-5
MASK_VALUE = -1e30
DEPTH = 1
DEEPNORM_ALPHA = (2.0 * DEPTH) ** 0.25
LOG2E = 1.4426950408889634
Q_SCALE = HEAD_DIM ** -0.5 * LOG2E
N_DR = 2 * NA_KH - 1
N_DC = 2 * NA_KW - 1

LANES = 128
BF16_ROWS = 16
PAIR = 2 * HEAD_DIM
assert PAIR == LANES and 2 * GRID_W == LANES and SWA_KV_WIDTH == LANES

PROJ_TM = 1024
TQ = 128
Q_ROWS = TQ // GRID_W
NA_KEYS = NA_KH * GRID_W
KEY_PAIRS = NA_KEYS // LANES
N_BIAS_TILES = N_DR - 1
SWA_KEYS = 3 * TQ

_QA = (0, NA_WIDTH)
_VA = (_QA[1], _QA[1] + NA_WIDTH)
_KB = (_VA[1], _VA[1] + SWA_KV_WIDTH)
TOK_WIDTH = _KB[1]
GATE_BLOCK = 256
_T_KA = (0, NA_WIDTH)
_T_QB = (_T_KA[1], _T_KA[1] + SWA_WIDTH)
_T_VB = (_T_QB[1], _T_QB[1] + SWA_KV_WIDTH)
FEAT_ROWS = _T_VB[1]
_W_QA = 0
_W_KA = _W_QA + NA_WIDTH
_W_VA = _W_KA + NA_WIDTH
_W_ZA = _W_VA + NA_WIDTH
_W_QB = _W_ZA + NA_WIDTH
_W_KB = _W_QB + SWA_WIDTH
_W_VB = _W_KB + SWA_KV_WIDTH
_W_ZB = _W_VB + SWA_KV_WIDTH
IN_WIDTH = _W_ZB + SWA_WIDTH


def _swa_band_tiles():
    q = np.arange(TQ)[None, :]
    k = np.arange(SWA_KEYS)[:, None] - TQ
    band = np.abs(k - q) <= SWA_WINDOW
    left_ok = np.concatenate([np.zeros((TQ, 1), bool), np.ones((2 * TQ, 1), bool)], axis=0)
    right_ok = np.concatenate([np.ones((2 * TQ, 1), bool), np.zeros((TQ, 1), bool)], axis=0)
    variants = np.stack([band & left_ok, band, band & right_ok])
    return np.where(variants, 0.0, MASK_VALUE).astype(np.float32)


def _rope_tables(seq):
    inv_freq = ROPE_THETA ** (-jnp.arange(0, HEAD_DIM, 2, dtype=jnp.float32) / HEAD_DIM)
    ang = jnp.arange(seq, dtype=jnp.int32).astype(jnp.float32)[:, None] * inv_freq[None, :]
    cos = jnp.cos(ang)
    sin = jnp.sin(ang)
    cos_head = jnp.concatenate([cos, cos], axis=1)
    sin_head = jnp.concatenate([-sin, sin], axis=1)
    cos_tok = jnp.concatenate([cos_head, cos_head], axis=1)
    sin_tok = jnp.concatenate([sin_head, sin_head], axis=1)
    return cos_tok, sin_tok, cos_head.T * Q_SCALE, sin_head.T * Q_SCALE


def _build_proj_weights(win_ref, wtok_ref, wft_ref):
    bf16 = jnp.bfloat16
    wtok_ref[:, _QA[0]:_QA[1]] = win_ref[:, _W_QA:_W_QA + NA_WIDTH].astype(bf16)
    wtok_ref[:, _VA[0]:_VA[1]] = win_ref[:, _W_VA:_W_VA + NA_WIDTH].astype(bf16)
    wtok_ref[:, _KB[0]:_KB[1]] = win_ref[:, _W_KB:_W_KB + SWA_KV_WIDTH].astype(bf16)
    for dst, src, width in ((_T_KA[0], _W_KA, NA_WIDTH), (_T_QB[0], _W_QB, SWA_WIDTH), (_T_VB[0], _W_VB, SWA_KV_WIDTH)):
        for c in range(width // LANES):
            wft_ref[dst + c * LANES:dst + (c + 1) * LANES, :] = \
                win_ref[:, src + c * LANES:src + (c + 1) * LANES].T.astype(bf16)


def _proj_kernel(x_ref, win_ref, cos_ref, sin_ref, cost_ref, sint_ref,
                 qa_ref, va_ref, kb_ref, kta_ref, qbt_ref, vbt_ref,
                 wtok_ref, wft_ref, carry_ref, *, tiles_per_seq):
    bf16 = jnp.bfloat16
    i = pl.program_id(0)
    half = HEAD_DIM // 2
    n_blk = PROJ_TM // LANES

    @pl.when(i == 0)
    def _():
        _build_proj_weights(win_ref, wtok_ref, wft_ref)

    @pl.when(i % tiles_per_seq == 0)
    def _():
        carry_ref[...] = jnp.zeros_like(carry_ref)

    xb = x_ref[...].astype(bf16)

    def proj(span):
        return jnp.dot(xb, wtok_ref[:, span[0]:span[1]], preferred_element_type=jnp.float32)

    qa_ref[...] = (proj(_QA) * Q_SCALE).astype(bf16)
    va_ref[...] = proj(_VA).astype(bf16)

    kb = proj(_KB)
    lane = lax.broadcasted_iota(jnp.int32, (PROJ_TM, LANES), 1)
    low_half = (lane % HEAD_DIM) < half
    partner = jnp.where(low_half, pltpu.roll(kb, LANES - half, 1), pltpu.roll(kb, half, 1))
    kb_ref[...] = (kb * cos_ref[...] + partner * sin_ref[...]).astype(bf16)

    ft = lax.dot_general(wft_ref[...], xb, (((1,), (1,)), ((), ())), preferred_element_type=jnp.float32)
    cols = [carry_ref[...]] + [ft[_T_KA[0]:_T_KA[1], j * LANES:(j + 1) * LANES] for j in range(n_blk)]
    low_k = lax.broadcasted_iota(jnp.int32, (NA_WIDTH, LANES), 1) < GRID_W
    rolled = [pltpu.roll(c, GRID_W, 1) for c in cols]
    for j in range(n_blk):
        kta_ref[0, 0, j] = cols[j + 1].astype(bf16)
        kta_ref[0, 1, j] = jnp.where(low_k, rolled[j], rolled[j + 1]).astype(bf16)
    carry_ref[...] = cols[n_blk]

    cos_t = cost_ref[...]
    sin_t = sint_ref[...]
    for h in range(SWA_Q_HEADS):
        blk = ft[_T_QB[0] + h * HEAD_DIM:_T_QB[0] + (h + 1) * HEAD_DIM]
        partner_t = jnp.concatenate([blk[half:], blk[:half]], axis=0)
        qbt_ref[0, h * HEAD_DIM:(h + 1) * HEAD_DIM, :] = (blk * cos_t + partner_t * sin_t).astype(bf16)

    vt = ft[_T_VB[0]:_T_VB[1]].astype(bf16)
    for j in range(n_blk):
        vbt_ref[0, j] = vt[:, j * LANES:(j + 1) * LANES]


def _build_na_bias_tiles(rel_ref, tbl_ref):
    c = lax.broadcasted_iota(jnp.int32, (GRID_W, LANES), 0)
    lane = lax.broadcasted_iota(jnp.int32, (GRID_W, LANES), 1)
    kc = lane & (GRID_W - 1)
    start = jnp.clip(c - NA_KW // 2, 0, GRID_W - NA_KW)
    in_win = (kc >= start) & (kc < start + NA_KW)
    low = lane < GRID_W
    masked = jnp.full((GRID_W, LANES), MASK_VALUE, jnp.float32)
    for h in range(NA_HEADS):
        def half(dr, upper):
            row = jnp.broadcast_to(rel_ref[pl.ds(h * N_DR + dr, 1), :] * LOG2E, (GRID_W, LANES))
            shift = (LANES - (NA_KW - 1) + (GRID_W if upper else 0)) % LANES
            return pltpu.roll(row, shift, 1, stride=1, stride_axis=0)

        lower = [half(dr, False) for dr in range(N_DR - 1)]
        upper = [half(dr, True) for dr in range(1, N_DR)]
        for d in range(N_BIAS_TILES):
            tbl_ref[d, h] = jnp.where(in_win, jnp.where(low, lower[d], upper[d]), masked)


def _attn_kernel(sink_ref,
                 qa_ref, kta_ref, va_ref, qbt_ref, kb_ref, vbt_ref, x_ref, wout_ref, *rest,
                 rows, n_blocks, n_total):
    n_gate_blocks = MIX_WIDTH // GATE_BLOCK
    wz_refs = rest[:n_gate_blocks]
    rel_ref, band_ref, gain_ref, beta_ref, out_ref, woutb_ref, wzb_ref, tbl_ref, sna_ref, ssw_ref = rest[n_gate_blocks:]
    _attn_body(sink_ref, qa_ref, kta_ref, va_ref, qbt_ref, kb_ref, vbt_ref, x_ref, wout_ref, wz_refs,
               rel_ref, band_ref, gain_ref, beta_ref, out_ref, woutb_ref, wzb_ref, tbl_ref, sna_ref, ssw_ref,
               rows=rows, n_blocks=n_blocks, n_total=n_total)


def _attn_body(sink_ref, qa_ref, kta_ref, va_ref, qbt_ref, kb_ref, vbt_ref, x_ref, wout_ref, wz_refs,
               rel_ref, band_ref, gain_ref, beta_ref, out_ref, woutb_ref, wzb_ref, tbl_ref, sna_ref, ssw_ref,
               *, rows, n_blocks, n_total):
    i = pl.program_id(0)

    @pl.when(i == 0)
    def _():
        woutb_ref[...] = wout_ref[...].astype(jnp.bfloat16)
        for c, wz_ref in enumerate(wz_refs):
            wzb_ref[:, c * GATE_BLOCK:(c + 1) * GATE_BLOCK] = wz_ref[...].astype(jnp.bfloat16)
        _build_na_bias_tiles(rel_ref, tbl_ref)
        sna_ref[0] = jnp.zeros(sna_ref.shape[1:], sna_ref.dtype)
        ssw_ref[0] = jnp.zeros(ssw_ref.shape[1:], ssw_ref.dtype)

    refs = (sink_ref, qa_ref, kta_ref, va_ref, qbt_ref, kb_ref, vbt_ref, wzb_ref, x_ref,
            band_ref, gain_ref, beta_ref, out_ref, woutb_ref, tbl_ref)
    for parity in range(2):
        @pl.when(i % 2 == parity)
        def _():
            _attn_step(i, *refs, sna_ref.at[parity], ssw_ref.at[parity], sna_ref.at[1 - parity],
                       ssw_ref.at[1 - parity], rows=rows, n_blocks=n_blocks, n_total=n_total)


def _attn_step(i, sink_ref, qa_ref, kta_ref, va_ref, qbt_ref, kb_ref, vbt_ref, wzb_ref, x_ref,
               band_ref, gain_ref, beta_ref, out_ref, woutb_ref, tbl_ref,
               sna_rd, ssw_rd, sna_wr, ssw_wr, *, rows, n_blocks, n_total):
    f32, bf16 = jnp.float32, jnp.bfloat16
    kh = min(NA_KH, rows)
    n_pairs = NA_HEADS // 2
    lane = lax.broadcasted_iota(jnp.int32, (GRID_W, LANES), 1)
    low = lane < HEAD_DIM

    def window_start(t, qrow):
        r = Q_ROWS * t + qrow
        return r, jnp.clip(r - kh // 2, 0, rows - kh)

    t = jnp.maximum(i - 1, 0) % n_blocks
    jl = jnp.maximum(t - 1, 0)
    jr = jnp.minimum(t + 1, n_blocks - 1)

    na_e, na_v = [], []
    for qrow in range(Q_ROWS):
        _, r0 = window_start(t, qrow)
        key0 = pl.multiple_of(r0 * GRID_W, GRID_W)
        for p in range(n_pairs):
            s_ref = sna_rd.at[qrow * n_pairs + p]
            m = jnp.max(s_ref[...], axis=-1, keepdims=True)
            na_e.append(jnp.exp2(s_ref[...] - m).astype(bf16))
            na_v.append(va_ref[0, pl.ds(key0, NA_KEYS), p * LANES:(p + 1) * LANES])
    sw_e, sw_sink, sw_v = [], [], []
    for g in range(SWA_KV_HEADS):
        s_ref = ssw_rd.at[g]
        sink = jnp.concatenate(
            [jnp.full((1, TQ), sink_ref[SWA_GROUP * g + j] * LOG2E, f32) for j in range(SWA_GROUP)], axis=1)
        m = jnp.maximum(jnp.max(s_ref[...], axis=0, keepdims=True), sink)
        sw_e.append(jnp.exp2(s_ref[...] - m).astype(bf16))
        sw_sink.append(jnp.exp2(sink - m))
        sw_v.append(jnp.concatenate(
            [vbt_ref[0, j, g * HEAD_DIM:(g + 1) * HEAD_DIM, :] for j in (jl, t, jr)], axis=1))

    na_out = []
    for u in range(Q_ROWS * n_pairs):
        v1 = jnp.concatenate([na_v[u], jnp.ones_like(na_v[u])], axis=1)
        o = jnp.dot(na_e[u], v1, preferred_element_type=f32)
        o = o[:, :LANES] * (1.0 / o[:, LANES:])
        na_out.append(jnp.where(low, o[:GRID_W], o[GRID_W:]))
    mixed_cols = [jnp.concatenate([na_out[qrow * n_pairs + p] for qrow in range(Q_ROWS)], axis=0)
                  for p in range(n_pairs)]
    for g in range(SWA_KV_HEADS):
        v1 = jnp.concatenate([sw_v[g], jnp.ones((BF16_ROWS, SWA_KEYS), bf16)], axis=0)
        o = jnp.dot(v1, sw_e[g], preferred_element_type=f32)
        o = o[:HEAD_DIM] * (1.0 / (o[HEAD_DIM:HEAD_DIM + 1] + sw_sink[g]))
        for pp in range(SWA_GROUP // 2):
            both = jnp.concatenate([o[:, (2 * pp) * TQ:(2 * pp + 1) * TQ],
                                    o[:, (2 * pp + 1) * TQ:(2 * pp + 2) * TQ]], axis=0)
            mixed_cols.append(both.T)

    mixed = jnp.concatenate(mixed_cols, axis=1)
    x_blk = x_ref[...]
    z = jnp.dot(x_blk.astype(bf16), wzb_ref[...], preferred_element_type=f32)
    mixed = (mixed * (z * jax.nn.sigmoid(z))).astype(bf16)
    y = jnp.dot(mixed, woutb_ref[...], preferred_element_type=f32)
    r = DEEPNORM_ALPHA * x_blk + y
    mu = jnp.mean(r, axis=-1, keepdims=True)
    d = r - mu
    var = jnp.mean(d * d, axis=-1, keepdims=True)
    out_ref[...] = d * lax.rsqrt(var + LN_EPS) * gain_ref[...] + beta_ref[...]

    t1 = jnp.minimum(i, n_total - 1) % n_blocks
    for qrow in range(Q_ROWS):
        r, r0 = window_start(t1, qrow)
        par = r0 & 1
        d0 = r0 - r + (NA_KH - 1)
        for p in range(n_pairs):
            cols = slice(p * LANES, (p + 1) * LANES)
            qp = qa_ref[qrow * GRID_W:(qrow + 1) * GRID_W, cols]
            zero = jnp.zeros_like(qp)
            q2 = jnp.concatenate([jnp.where(low, qp, zero), jnp.where(low, zero, qp)], axis=0)
            kt = jnp.concatenate([kta_ref[0, par, (r0 + 2 * j + 1) // 2, cols, :] for j in range(KEY_PAIRS)],
                                 axis=1)
            bias = jnp.concatenate(
                [jnp.concatenate([tbl_ref[d0 + 2 * j, 2 * p + hp] for j in range(KEY_PAIRS)], axis=1)
                 for hp in range(2)], axis=0)
            sna_wr[qrow * n_pairs + p] = jnp.dot(q2, kt, preferred_element_type=f32) + bias
    jl1 = jnp.maximum(t1 - 1, 0)
    jr1 = jnp.minimum(t1 + 1, n_blocks - 1)
    variant = jnp.where(t1 == 0, 0, jnp.where(t1 == n_blocks - 1, 2, 1))
    band = band_ref[variant]
    band4 = jnp.concatenate([band] * SWA_GROUP, axis=1)
    k3 = jnp.concatenate([kb_ref[0, pl.ds(pl.multiple_of(j * TQ, TQ), TQ), :] for j in (jl1, t1, jr1)],
                         axis=0)
    for g in range(SWA_KV_HEADS):
        qt = jnp.concatenate(
            [qbt_ref[0, (SWA_GROUP * g + j) * HEAD_DIM:(SWA_GROUP * g + j + 1) * HEAD_DIM, :]
             for j in range(SWA_GROUP)], axis=1)
        parts = [jnp.zeros_like(qt)] * SWA_KV_HEADS
        parts[g] = qt
        ssw_wr[g] = jnp.dot(k3, jnp.concatenate(parts, axis=0), preferred_element_type=f32) + band4


def kernel(x, w_in, rel_pos_bias, sink_logits, w_out, ln_gain, ln_bias):
    B, S, D = x.shape
    assert D == D_MODEL and S % PROJ_TM == 0 and S % GRID_W == 0
    rows = S // GRID_W
    assert rows >= NA_KH and rows % Q_ROWS == 0
    n_blocks = S // TQ
    n_tok = B * S
    bf16 = jnp.bfloat16
    x2 = x.reshape(n_tok, D)

    cos_tok, sin_tok, cos_feat, sin_feat = _rope_tables(S)
    band = jnp.asarray(_swa_band_tiles())

    for layer in range(DEPTH):
        tiles_per_seq = S // PROJ_TM
        blk128 = PROJ_TM // LANES
        seq_tile = lambda i: (i // tiles_per_seq, i % tiles_per_seq)
        qa, va, kb, kta, qbt, vbt = pl.pallas_call(
            functools.partial(_proj_kernel, tiles_per_seq=tiles_per_seq),
            grid=(n_tok // PROJ_TM,),
            in_specs=[
                pl.BlockSpec((PROJ_TM, D), lambda i: (i, 0)),
                pl.BlockSpec((D, IN_WIDTH), lambda i: (0, 0), pipeline_mode=pl.Buffered(1)),
                pl.BlockSpec((PROJ_TM, LANES), lambda i: (i % tiles_per_seq, 0)),
                pl.BlockSpec((PROJ_TM, LANES), lambda i: (i % tiles_per_seq, 0)),
                pl.BlockSpec((HEAD_DIM, PROJ_TM), lambda i: (0, i % tiles_per_seq)),
                pl.BlockSpec((HEAD_DIM, PROJ_TM), lambda i: (0, i % tiles_per_seq)),
            ],
            out_specs=[
                pl.BlockSpec((PROJ_TM, NA_WIDTH), lambda i: (i, 0)),
                pl.BlockSpec((PROJ_TM, NA_WIDTH), lambda i: (i, 0)),
                pl.BlockSpec((PROJ_TM, SWA_KV_WIDTH), lambda i: (i, 0)),
                pl.BlockSpec((1, 2, blk128, NA_WIDTH, LANES), lambda i: (seq_tile(i)[0], 0, seq_tile(i)[1], 0, 0)),
                pl.BlockSpec((1, SWA_WIDTH, PROJ_TM), lambda i: (seq_tile(i)[0], 0, seq_tile(i)[1])),
                pl.BlockSpec((1, blk128, SWA_KV_WIDTH, LANES), lambda i: (seq_tile(i)[0], seq_tile(i)[1], 0, 0)),
            ],
            out_shape=[
                jax.ShapeDtypeStruct((n_tok, NA_WIDTH), bf16),
                jax.ShapeDtypeStruct((n_tok, NA_WIDTH), bf16),
                jax.ShapeDtypeStruct((n_tok, SWA_KV_WIDTH), bf16),
                jax.ShapeDtypeStruct((B, 2, S // LANES, NA_WIDTH, LANES), bf16),
                jax.ShapeDtypeStruct((B, SWA_WIDTH, S), bf16),
                jax.ShapeDtypeStruct((B, S // LANES, SWA_KV_WIDTH, LANES), bf16),
            ],
            scratch_shapes=[pltpu.VMEM((D, TOK_WIDTH), bf16), pltpu.VMEM((FEAT_ROWS, D), bf16),
                            pltpu.VMEM((NA_WIDTH, LANES), jnp.float32)],
            compiler_params=pltpu.CompilerParams(
                dimension_semantics=("arbitrary",), vmem_limit_bytes=56 * 1024 * 1024),
            name="in_proj",
        )(x2, w_in[layer], cos_tok, sin_tok, cos_feat, sin_feat)

        rel_rows = jnp.pad(rel_pos_bias[layer].astype(jnp.float32).reshape(NA_HEADS * N_DR, N_DC),
                           ((0, 0), (0, LANES - N_DC)))
        n_total = B * n_blocks
        gate_blocks = [(_W_ZA + c * GATE_BLOCK) // GATE_BLOCK for c in range(NA_WIDTH // GATE_BLOCK)] + \
                      [(_W_ZB + c * GATE_BLOCK) // GATE_BLOCK for c in range(SWA_WIDTH // GATE_BLOCK)]
        assert _W_ZA % GATE_BLOCK == 0 and _W_ZB % GATE_BLOCK == 0
        cur = lambda i: jnp.minimum(i, n_total - 1)
        prv = lambda i: jnp.maximum(i - 1, 0)
        grid_spec = pltpu.PrefetchScalarGridSpec(
            num_scalar_prefetch=1,
            grid=(n_total + 1,),
            in_specs=[
                pl.BlockSpec((TQ, NA_WIDTH), lambda i, *_: (cur(i), 0)),
                pl.BlockSpec((1, 2, S // LANES, NA_WIDTH, LANES), lambda i, *_: (cur(i) // n_blocks, 0, 0, 0, 0)),
                pl.BlockSpec((1, S, NA_WIDTH), lambda i, *_: (prv(i) // n_blocks, 0, 0)),
                pl.BlockSpec((1, SWA_WIDTH, TQ), lambda i, *_: (cur(i) // n_blocks, 0, cur(i) % n_blocks)),
                pl.BlockSpec((1, S, SWA_KV_WIDTH), lambda i, *_: (cur(i) // n_blocks, 0, 0)),
                pl.BlockSpec((1, S // LANES, SWA_KV_WIDTH, LANES), lambda i, *_: (prv(i) // n_blocks, 0, 0, 0)),
                pl.BlockSpec((TQ, D), lambda i, *_: (prv(i), 0)),
                pl.BlockSpec((MIX_WIDTH, D), lambda i, *_: (0, 0), pipeline_mode=pl.Buffered(1)),
                *[pl.BlockSpec((D, GATE_BLOCK), functools.partial(lambda c, i, *_: (0, c), c),
                               pipeline_mode=pl.Buffered(1)) for c in gate_blocks],
                pl.BlockSpec((NA_HEADS * N_DR, LANES), lambda i, *_: (0, 0)),
                pl.BlockSpec((3, SWA_KEYS, TQ), lambda i, *_: (0, 0, 0)),
                pl.BlockSpec((1, D), lambda i, *_: (0, 0)),
                pl.BlockSpec((1, D), lambda i, *_: (0, 0)),
            ],
            out_specs=pl.BlockSpec((TQ, D), lambda i, *_: (prv(i), 0)),
            scratch_shapes=[pltpu.VMEM((MIX_WIDTH, D), bf16),
                            pltpu.VMEM((D, MIX_WIDTH), bf16),
                            pltpu.VMEM((N_BIAS_TILES, NA_HEADS, GRID_W, 2 * GRID_W), jnp.float32),
                            pltpu.VMEM((2, Q_ROWS * NA_HEADS // 2, TQ, NA_KEYS), jnp.float32),
                            pltpu.VMEM((2, SWA_KV_HEADS, SWA_KEYS, SWA_GROUP * TQ), jnp.float32)],
        )
        x2 = pl.pallas_call(
            functools.partial(_attn_kernel, rows=rows, n_blocks=n_blocks, n_total=n_total),
            grid_spec=grid_spec,
            out_shape=jax.ShapeDtypeStruct((n_tok, D), jnp.float32),
            compiler_params=pltpu.CompilerParams(
                dimension_semantics=("arbitrary",), vmem_limit_bytes=56 * 1024 * 1024),
            name="attn_out",
        )(sink_logits[layer].astype(jnp.float32),
          qa, kta, va.reshape(B, S, NA_WIDTH), qbt, kb.reshape(B, S, SWA_KV_WIDTH), vbt, x2,
          w_out[layer], *([w_in[layer]] * len(gate_blocks)), rel_rows, band, ln_gain[layer].reshape(1, D), ln_bias[layer].reshape(1, D))
    return x2.reshape(B, S, D)
```

```python
import functools

import numpy as np
import jax
import jax.numpy as jnp
from jax import lax
from jax.experimental import pallas as pl
from jax.experimental.pallas import tpu as pltpu

D_MODEL = 1024
HEAD_DIM = 64
NA_HEADS = 8
NA_WIDTH = NA_HEADS * HEAD_DIM
NA_KH = 8
NA_KW = 16
GRID_W = 64
SWA_Q_HEADS = 8
SWA_KV_HEADS = 2
SWA_GROUP = SWA_Q_HEADS // SWA_KV_HEADS
SWA_WIDTH = SWA_Q_HEADS * HEAD_DIM
SWA_KV_WIDTH = SWA_KV_HEADS * HEAD_DIM
SWA_WINDOW = 128
ROPE_THETA = 10000.0
MIX_WIDTH = NA_WIDTH + SWA_WIDTH
LN_EPS = 1e-5
MASK_VALUE = -1e30
DEPTH = 1
DEEPNORM_ALPHA = (2.0 * DEPTH) ** 0.25
LOG2E = 1.4426950408889634
Q_SCALE = HEAD_DIM ** -0.5 * LOG2E
N_DR = 2 * NA_KH - 1
N_DC = 2 * NA_KW - 1

LANES = 128
BF16_ROWS = 16
PAIR = 2 * HEAD_DIM
assert PAIR == LANES and 2 * GRID_W == LANES and SWA_KV_WIDTH == LANES

PROJ_TM = 1024
TQ = 128
Q_ROWS = TQ // GRID_W
NA_KEYS = NA_KH * GRID_W
KEY_PAIRS = NA_KEYS // LANES
N_BIAS_TILES = N_DR - 1
SWA_KEYS = 3 * TQ

_QA = (0, NA_WIDTH)
_VA = (_QA[1], _QA[1] + NA_WIDTH)
_KB = (_VA[1], _VA[1] + SWA_KV_WIDTH)
TOK_WIDTH = _KB[1]
GATE_BLOCK = 256
_T_KA = (0, NA_WIDTH)
_T_QB = (_T_KA[1], _T_KA[1] + SWA_WIDTH)
_T_VB = (_T_QB[1], _T_QB[1] + SWA_KV_WIDTH)
FEAT_ROWS = _T_VB[1]
_W_QA = 0
_W_KA = _W_QA + NA_WIDTH
_W_VA = _W_KA + NA_WIDTH
_W_ZA = _W_VA + NA_WIDTH
_W_QB = _W_ZA + NA_WIDTH
_W_KB = _W_QB + SWA_WIDTH
_W_VB = _W_KB + SWA_KV_WIDTH
_W_ZB = _W_VB + SWA_KV_WIDTH
IN_WIDTH = _W_ZB + SWA_WIDTH


def _swa_band_tiles():
    q = np.arange(TQ)[None, :]
    k = np.arange(SWA_KEYS)[:, None] - TQ
    band = np.abs(k - q) <= SWA_WINDOW
    left_ok = np.concatenate([np.zeros((TQ, 1), bool), np.ones((2 * TQ, 1), bool)], axis=0)
    right_ok = np.concatenate([np.ones((2 * TQ, 1), bool), np.zeros((TQ, 1), bool)], axis=0)
    variants = np.stack([band & left_ok, band, band & right_ok])
    return np.where(variants, 0.0, MASK_VALUE).astype(np.float32)


def _rope_tables(seq):
    inv_freq = ROPE_THETA ** (-jnp.arange(0, HEAD_DIM, 2, dtype=jnp.float32) / HEAD_DIM)
    ang = jnp.arange(seq, dtype=jnp.int32).astype(jnp.float32)[:, None] * inv_freq[None, :]
    cos = jnp.cos(ang)
    sin = jnp.sin(ang)
    cos_head = jnp.concatenate([cos, cos], axis=1)
    sin_head = jnp.concatenate([-sin, sin], axis=1)
    cos_tok = jnp.concatenate([cos_head, cos_head], axis=1)
    sin_tok = jnp.concatenate([sin_head, sin_head], axis=1)
    return cos_tok, sin_tok, cos_head.T * Q_SCALE, sin_head.T * Q_SCALE


def _build_proj_weights(win_ref, wtok_ref, wft_ref):
    bf16 = jnp.bfloat16
    wtok_ref[:, _QA[0]:_QA[1]] = win_ref[:, _W_QA:_W_QA + NA_WIDTH].astype(bf16)
    wtok_ref[:, _VA[0]:_VA[1]] = win_ref[:, _W_VA:_W_VA + NA_WIDTH].astype(bf16)
    wtok_ref[:, _KB[0]:_KB[1]] = win_ref[:, _W_KB:_W_KB + SWA_KV_WIDTH].astype(bf16)
    for dst, src, width in ((_T_KA[0], _W_KA, NA_WIDTH), (_T_QB[0], _W_QB, SWA_WIDTH), (_T_VB[0], _W_VB, SWA_KV_WIDTH)):
        for c in range(width // LANES):
            wft_ref[dst + c * LANES:dst + (c + 1) * LANES, :] = \
                win_ref[:, src + c * LANES:src + (c + 1) * LANES].T.astype(bf16)


def _proj_kernel(x_ref, win_ref, cos_ref, sin_ref, cost_ref, sint_ref,
                 qa_ref, va_ref, kb_ref, kta_ref, qbt_ref, vbt_ref,
                 wtok_ref, wft_ref, carry_ref, *, tiles_per_seq):
    bf16 = jnp.bfloat16
    i = pl.program_id(0)
    half = HEAD_DIM // 2
    n_blk = PROJ_TM // LANES

    @pl.when(i == 0)
    def _():
        _build_proj_weights(win_ref, wtok_ref, wft_ref)

    @pl.when(i % tiles_per_seq == 0)
    def _():
        carry_ref[...] = jnp.zeros_like(carry_ref)

    xb = x_ref[...].astype(bf16)

    def proj(span):
        return jnp.dot(xb, wtok_ref[:, span[0]:span[1]], preferred_element_type=jnp.float32)

    qa_ref[...] = (proj(_QA) * Q_SCALE).astype(bf16)
    va_ref[...] = proj(_VA).astype(bf16)

    kb = proj(_KB)
    lane = lax.broadcasted_iota(jnp.int32, (PROJ_TM, LANES), 1)
    low_half = (lane % HEAD_DIM) < half
    partner = jnp.where(low_half, pltpu.roll(kb, LANES - half, 1), pltpu.roll(kb, half, 1))
    kb_ref[...] = (kb * cos_ref[...] + partner * sin_ref[...]).astype(bf16)

    ft = lax.dot_general(wft_ref[...], xb, (((1,), (1,)), ((), ())), preferred_element_type=jnp.float32)
    cols = [carry_ref[...]] + [ft[_T_KA[0]:_T_KA[1], j * LANES:(j + 1) * LANES] for j in range(n_blk)]
    low_k = lax.broadcasted_iota(jnp.int32, (NA_WIDTH, LANES), 1) < GRID_W
    rolled = [pltpu.roll(c, GRID_W, 1) for c in cols]
    for j in range(n_blk):
        kta_ref[0, 0, j] = cols[j + 1].astype(bf16)
        kta_ref[0, 1, j] = jnp.where(low_k, rolled[j], rolled[j + 1]).astype(bf16)
    carry_ref[...] = cols[n_blk]

    cos_t = cost_ref[...]
    sin_t = sint_ref[...]
    for h in range(SWA_Q_HEADS):
        blk = ft[_T_QB[0] + h * HEAD_DIM:_T_QB[0] + (h + 1) * HEAD_DIM]
        partner_t = jnp.concatenate([blk[half:], blk[:half]], axis=0)
        qbt_ref[0, h * HEAD_DIM:(h + 1) * HEAD_DIM, :] = (blk * cos_t + partner_t * sin_t).astype(bf16)

    vt = ft[_T_VB[0]:_T_VB[1]].astype(bf16)
    for j in range(n_blk):
        vbt_ref[0, j] = vt[:, j * LANES:(j + 1) * LANES]


def _build_na_bias_tiles(rel_ref, tbl_ref):
    c = lax.broadcasted_iota(jnp.int32, (GRID_W, LANES), 0)
    lane = lax.broadcasted_iota(jnp.int32, (GRID_W, LANES), 1)
    kc = lane & (GRID_W - 1)
    start = jnp.clip(c - NA_KW // 2, 0, GRID_W - NA_KW)
    in_win = (kc >= start) & (kc < start + NA_KW)
    low = lane < GRID_W
    masked = jnp.full((GRID_W, LANES), MASK_VALUE, jnp.float32)
    for h in range(NA_HEADS):
        def half(dr, upper):
            row = jnp.broadcast_to(rel_ref[pl.ds(h * N_DR + dr, 1), :] * LOG2E, (GRID_W, LANES))
            shift = (LANES - (NA_KW - 1) + (GRID_W if upper else 0)) % LANES
            return pltpu.roll(row, shift, 1, stride=1, stride_axis=0)

        lower = [half(dr, False) for dr in range(N_DR - 1)]
        upper = [half(dr, True) for dr in range(1, N_DR)]
        for d in range(N_BIAS_TILES):
            tbl_ref[d, h] = jnp.where(in_win, jnp.where(low, lower[d], upper[d]), masked)


def _attn_kernel(sink_ref, qa_lo, qa_hi, kta_lo, kta_hi, va_ref, qbt_lo, qbt_hi, kb_lo, kb_hi, vbt_ref,
                 x_ref, wout_ref, *rest, rows, n_blocks, n_total):
    n_gate_blocks = MIX_WIDTH // GATE_BLOCK
    wz_refs = rest[:n_gate_blocks]
    rel_ref, band_ref, gain_ref, beta_ref, out_ref, woutb_ref, wzb_ref, tbl_ref, sna_ref, ssw_ref = rest[n_gate_blocks:]
    j = pl.program_id(0)

    @pl.when(j == 0)
    def _():
        woutb_ref[...] = wout_ref[...].astype(jnp.bfloat16)
        for c, wz_ref in enumerate(wz_refs):
            wzb_ref[:, c * GATE_BLOCK:(c + 1) * GATE_BLOCK] = wz_ref[...].astype(jnp.bfloat16)
        _build_na_bias_tiles(rel_ref, tbl_ref)
        sna_ref[0] = jnp.zeros(sna_ref.shape[1:], sna_ref.dtype)
        ssw_ref[0] = jnp.zeros(ssw_ref.shape[1:], ssw_ref.dtype)

    for h, (qa_ref, kta_ref, qbt_ref, kb_ref) in enumerate(((qa_lo, kta_lo, qbt_lo, kb_lo),
                                                            (qa_hi, kta_hi, qbt_hi, kb_hi))):
        rows_h = pl.ds(h * TQ, TQ)
        _attn_step(2 * j - 1 + h, sink_ref, qa_ref, kta_ref, va_ref, qbt_ref, kb_ref, vbt_ref, wzb_ref,
                   x_ref.at[rows_h], band_ref, gain_ref, beta_ref, out_ref.at[rows_h], woutb_ref, tbl_ref,
                   sna_ref.at[h], ssw_ref.at[h], sna_ref.at[1 - h], ssw_ref.at[1 - h],
                   rows=rows, n_blocks=n_blocks, n_total=n_total)


def _attn_step(i, sink_ref, qa_ref, kta_ref, va_ref, qbt_ref, kb_ref, vbt_ref, wzb_ref, x_ref,
               band_ref, gain_ref, beta_ref, out_ref, woutb_ref, tbl_ref,
               sna_rd, ssw_rd, sna_wr, ssw_wr, *, rows, n_blocks, n_total):
    f32, bf16 = jnp.float32, jnp.bfloat16
    kh = min(NA_KH, rows)
    n_pairs = NA_HEADS // 2
    lane = lax.broadcasted_iota(jnp.int32, (GRID_W, LANES), 1)
    low = lane < HEAD_DIM

    def window_start(t, qrow):
        r = Q_ROWS * t + qrow
        return r, jnp.clip(r - kh // 2, 0, rows - kh)

    t = jnp.maximum(i - 1, 0) % n_blocks
    jl = jnp.maximum(t - 1, 0)
    jr = jnp.minimum(t + 1, n_blocks - 1)

    na_e, na_v = [], []
    for qrow in range(Q_ROWS):
        _, r0 = window_start(t, qrow)
        key0 = pl.multiple_of(r0 * GRID_W, GRID_W)
        for p in range(n_pairs):
            s_ref = sna_rd.at[qrow * n_pairs + p]
            m = jnp.max(s_ref[...], axis=-1, keepdims=True)
            na_e.append(jnp.exp2(s_ref[...] - m).astype(bf16))
            na_v.append(va_ref[0, pl.ds(key0, NA_KEYS), p * LANES:(p + 1) * LANES])
    sw_e, sw_sink, sw_v = [], [], []
    for g in range(SWA_KV_HEADS):
        s_ref = ssw_rd.at[g]
        sink = jnp.concatenate(
            [jnp.full((1, TQ), sink_ref[SWA_GROUP * g + j] * LOG2E, f32) for j in range(SWA_GROUP)], axis=1)
        m = jnp.maximum(jnp.max(s_ref[...], axis=0, keepdims=True), sink)
        sw_e.append(jnp.exp2(s_ref[...] - m).astype(bf16))
        sw_sink.append(jnp.exp2(sink - m))
        sw_v.append(jnp.concatenate(
            [vbt_ref[0, j, g * HEAD_DIM:(g + 1) * HEAD_DIM, :] for j in (jl, t, jr)], axis=1))

    na_out = []
    for u in range(Q_ROWS * n_pairs):
        v1 = jnp.concatenate([na_v[u], jnp.ones_like(na_v[u])], axis=1)
        o = jnp.dot(na_e[u], v1, preferred_element_type=f32)
        o = o[:, :LANES] * (1.0 / o[:, LANES:])
        na_out.append(jnp.where(low, o[:GRID_W], o[GRID_W:]))
    mixed_cols = [jnp.concatenate([na_out[qrow * n_pairs + p] for qrow in range(Q_ROWS)], axis=0)
                  for p in range(n_pairs)]
    for g in range(SWA_KV_HEADS):
        v1 = jnp.concatenate([sw_v[g], jnp.ones((BF16_ROWS, SWA_KEYS), bf16)], axis=0)
        o = jnp.dot(v1, sw_e[g], preferred_element_type=f32)
        o = o[:HEAD_DIM] * (1.0 / (o[HEAD_DIM:HEAD_DIM + 1] + sw_sink[g]))
        for pp in range(SWA_GROUP // 2):
            both = jnp.concatenate([o[:, (2 * pp) * TQ:(2 * pp + 1) * TQ],
                                    o[:, (2 * pp + 1) * TQ:(2 * pp + 2) * TQ]], axis=0)
            mixed_cols.append(both.T)

    mixed = jnp.concatenate(mixed_cols, axis=1)
    x_blk = x_ref[...]
    z = jnp.dot(x_blk.astype(bf16), wzb_ref[...], preferred_element_type=f32)
    mixed = (mixed * (z * jax.nn.sigmoid(z))).astype(bf16)
    y = jnp.dot(mixed, woutb_ref[...], preferred_element_type=f32)
    r = DEEPNORM_ALPHA * x_blk + y
    mu = jnp.mean(r, axis=-1, keepdims=True)
    d = r - mu
    var = jnp.mean(d * d, axis=-1, keepdims=True)
    out_ref[...] = d * lax.rsqrt(var + LN_EPS) * gain_ref[...] + beta_ref[...]

    t1 = jnp.clip(i, 0, n_total - 1) % n_blocks
    for qrow in range(Q_ROWS):
        r, r0 = window_start(t1, qrow)
        par = r0 & 1
        d0 = r0 - r + (NA_KH - 1)
        for p in range(n_pairs):
            cols = slice(p * LANES, (p + 1) * LANES)
            qp = qa_ref[qrow * GRID_W:(qrow + 1) * GRID_W, cols]
            zero = jnp.zeros_like(qp)
            q2 = jnp.concatenate([jnp.where(low, qp, zero), jnp.where(low, zero, qp)], axis=0)
            kt = jnp.concatenate([kta_ref[0, par, (r0 + 2 * j + 1) // 2, cols, :] for j in range(KEY_PAIRS)],
                                 axis=1)
            bias = jnp.concatenate(
                [jnp.concatenate([tbl_ref[d0 + 2 * j, 2 * p + hp] for j in range(KEY_PAIRS)], axis=1)
                 for hp in range(2)], axis=0)
            sna_wr[qrow * n_pairs + p] = jnp.dot(q2, kt, preferred_element_type=f32) + bias
    jl1 = jnp.maximum(t1 - 1, 0)
    jr1 = jnp.minimum(t1 + 1, n_blocks - 1)
    variant = jnp.where(t1 == 0, 0, jnp.where(t1 == n_blocks - 1, 2, 1))
    band = band_ref[variant]
    band4 = jnp.concatenate([band] * SWA_GROUP, axis=1)
    k3 = jnp.concatenate([kb_ref[0, pl.ds(pl.multiple_of(j * TQ, TQ), TQ), :] for j in (jl1, t1, jr1)],
                         axis=0)
    for g in range(SWA_KV_HEADS):
        qt = jnp.concatenate(
            [qbt_ref[0, (SWA_GROUP * g + j) * HEAD_DIM:(SWA_GROUP * g + j + 1) * HEAD_DIM, :]
             for j in range(SWA_GROUP)], axis=1)
        parts = [jnp.zeros_like(qt)] * SWA_KV_HEADS
        parts[g] = qt
        ssw_wr[g] = jnp.dot(k3, jnp.concatenate(parts, axis=0), preferred_element_type=f32) + band4


def kernel(x, w_in, rel_pos_bias, sink_logits, w_out, ln_gain, ln_bias):
    B, S, D = x.shape
    assert D == D_MODEL and S % PROJ_TM == 0 and S % GRID_W == 0
    rows = S // GRID_W
    assert rows >= NA_KH and rows % Q_ROWS == 0
    n_blocks = S // TQ
    n_tok = B * S
    bf16 = jnp.bfloat16
    x2 = x.reshape(n_tok, D)

    cos_tok, sin_tok, cos_feat, sin_feat = _rope_tables(S)
    band = jnp.asarray(_swa_band_tiles())

    for layer in range(DEPTH):
        tiles_per_seq = S // PROJ_TM
        blk128 = PROJ_TM // LANES
        seq_tile = lambda i: (i // tiles_per_seq, i % tiles_per_seq)
        qa, va, kb, kta, qbt, vbt = pl.pallas_call(
            functools.partial(_proj_kernel, tiles_per_seq=tiles_per_seq),
            grid=(n_tok // PROJ_TM,),
            in_specs=[
                pl.BlockSpec((PROJ_TM, D), lambda i: (i, 0)),
                pl.BlockSpec((D, IN_WIDTH), lambda i: (0, 0), pipeline_mode=pl.Buffered(1)),
                pl.BlockSpec((PROJ_TM, LANES), lambda i: (i % tiles_per_seq, 0)),
                pl.BlockSpec((PROJ_TM, LANES), lambda i: (i % tiles_per_seq, 0)),
                pl.BlockSpec((HEAD_DIM, PROJ_TM), lambda i: (0, i % tiles_per_seq)),
                pl.BlockSpec((HEAD_DIM, PROJ_TM), lambda i: (0, i % tiles_per_seq)),
            ],
            out_specs=[
                pl.BlockSpec((PROJ_TM, NA_WIDTH), lambda i: (i, 0)),
                pl.BlockSpec((PROJ_TM, NA_WIDTH), lambda i: (i, 0)),
                pl.BlockSpec((PROJ_TM, SWA_KV_WIDTH), lambda i: (i, 0)),
                pl.BlockSpec((1, 2, blk128, NA_WIDTH, LANES), lambda i: (seq_tile(i)[0], 0, seq_tile(i)[1], 0, 0)),
                pl.BlockSpec((1, SWA_WIDTH, PROJ_TM), lambda i: (seq_tile(i)[0], 0, seq_tile(i)[1])),
                pl.BlockSpec((1, blk128, SWA_KV_WIDTH, LANES), lambda i: (seq_tile(i)[0], seq_tile(i)[1], 0, 0)),
            ],
            out_shape=[
                jax.ShapeDtypeStruct((n_tok, NA_WIDTH), bf16),
                jax.ShapeDtypeStruct((n_tok, NA_WIDTH), bf16),
                jax.ShapeDtypeStruct((n_tok, SWA_KV_WIDTH), bf16),
                jax.ShapeDtypeStruct((B, 2, S // LANES, NA_WIDTH, LANES), bf16),
                jax.ShapeDtypeStruct((B, SWA_WIDTH, S), bf16),
                jax.ShapeDtypeStruct((B, S // LANES, SWA_KV_WIDTH, LANES), bf16),
            ],
            scratch_shapes=[pltpu.VMEM((D, TOK_WIDTH), bf16), pltpu.VMEM((FEAT_ROWS, D), bf16),
                            pltpu.VMEM((NA_WIDTH, LANES), jnp.float32)],
            compiler_params=pltpu.CompilerParams(
                dimension_semantics=("arbitrary",), vmem_limit_bytes=56 * 1024 * 1024),
            name="in_proj",
        )(x2, w_in[layer], cos_tok, sin_tok, cos_feat, sin_feat)

        rel_rows = jnp.pad(rel_pos_bias[layer].astype(jnp.float32).reshape(NA_HEADS * N_DR, N_DC),
                           ((0, 0), (0, LANES - N_DC)))
        n_total = B * n_blocks
        gate_blocks = [(_W_ZA + c * GATE_BLOCK) // GATE_BLOCK for c in range(NA_WIDTH // GATE_BLOCK)] + \
                      [(_W_ZB + c * GATE_BLOCK) // GATE_BLOCK for c in range(SWA_WIDTH // GATE_BLOCK)]
        assert _W_ZA % GATE_BLOCK == 0 and _W_ZB % GATE_BLOCK == 0
        assert n_blocks % 2 == 0
        lo = lambda j: jnp.clip(2 * j - 1, 0, n_total - 1)
        hi = lambda j: jnp.clip(2 * j, 0, n_total - 1)
        fin = lambda j: jnp.maximum(j - 1, 0)
        q_spec = lambda blk: pl.BlockSpec((TQ, NA_WIDTH), lambda j, *_: (blk(j), 0))
        kta_spec = lambda blk: pl.BlockSpec((1, 2, S // LANES, NA_WIDTH, LANES),
                                            lambda j, *_: (blk(j) // n_blocks, 0, 0, 0, 0))
        qbt_spec = lambda blk: pl.BlockSpec((1, SWA_WIDTH, TQ),
                                            lambda j, *_: (blk(j) // n_blocks, 0, blk(j) % n_blocks))
        kb_spec = lambda blk: pl.BlockSpec((1, S, SWA_KV_WIDTH), lambda j, *_: (blk(j) // n_blocks, 0, 0))
        fin_batch = lambda j: (2 * fin(j)) // n_blocks
        grid_spec = pltpu.PrefetchScalarGridSpec(
            num_scalar_prefetch=1,
            grid=(n_total // 2 + 1,),
            in_specs=[
                q_spec(lo), q_spec(hi),
                kta_spec(lo), kta_spec(hi),
                pl.BlockSpec((1, S, NA_WIDTH), lambda j, *_: (fin_batch(j), 0, 0)),
                qbt_spec(lo), qbt_spec(hi),
                kb_spec(lo), kb_spec(hi),
                pl.BlockSpec((1, S // LANES, SWA_KV_WIDTH, LANES), lambda j, *_: (fin_batch(j), 0, 0, 0)),
                pl.BlockSpec((2 * TQ, D), lambda j, *_: (fin(j), 0)),
                pl.BlockSpec((MIX_WIDTH, D), lambda j, *_: (0, 0), pipeline_mode=pl.Buffered(1)),
                *[pl.BlockSpec((D, GATE_BLOCK), functools.partial(lambda c, j, *_: (0, c), c),
                               pipeline_mode=pl.Buffered(1)) for c in gate_blocks],
                pl.BlockSpec((NA_HEADS * N_DR, LANES), lambda j, *_: (0, 0)),
                pl.BlockSpec((3, SWA_KEYS, TQ), lambda j, *_: (0, 0, 0)),
                pl.BlockSpec((1, D), lambda j, *_: (0, 0)),
                pl.BlockSpec((1, D), lambda j, *_: (0, 0)),
            ],
            out_specs=pl.BlockSpec((2 * TQ, D), lambda j, *_: (fin(j), 0)),
            scratch_shapes=[pltpu.VMEM((MIX_WIDTH, D), bf16),
                            pltpu.VMEM((D, MIX_WIDTH), bf16),
                            pltpu.VMEM((N_BIAS_TILES, NA_HEADS, GRID_W, 2 * GRID_W), jnp.float32),
                            pltpu.VMEM((2, Q_ROWS * NA_HEADS // 2, TQ, NA_KEYS), jnp.float32),
                            pltpu.VMEM((2, SWA_KV_HEADS, SWA_KEYS, SWA_GROUP * TQ), jnp.float32)],
        )
        x2 = pl.pallas_call(
            functools.partial(_attn_kernel, rows=rows, n_blocks=n_blocks, n_total=n_total),
            grid_spec=grid_spec,
            out_shape=jax.ShapeDtypeStruct((n_tok, D), jnp.float32),
            compiler_params=pltpu.CompilerParams(
                dimension_semantics=("arbitrary",), vmem_limit_bytes=56 * 1024 * 1024),
            name="attn_out",
        )(sink_logits[layer].astype(jnp.float32),
          qa, qa, kta, kta, va.reshape(B, S, NA_WIDTH), qbt, qbt,
          kb.reshape(B, S, SWA_KV_WIDTH), kb.reshape(B, S, SWA_KV_WIDTH), vbt, x2,
          w_out[layer], *([w_in[layer]] * len(gate_blocks)), rel_rows, band, ln_gain[layer].reshape(1, D), ln_bias[layer].reshape(1, D))
    return x2.reshape(B, S, D)
```

```python
import functools

import numpy as np
import jax
import jax.numpy as jnp
from jax import lax
from jax.experimental import pallas as pl
from jax.experimental.pallas import tpu as pltpu

D_MODEL = 1024
HEAD_DIM = 64
NA_HEADS = 8
NA_WIDTH = NA_HEADS * HEAD_DIM
NA_KH = 8
NA_KW = 16
GRID_W = 64
SWA_Q_HEADS = 8
SWA_KV_HEADS = 2
SWA_GROUP = SWA_Q_HEADS // SWA_KV_HEADS
SWA_WIDTH = SWA_Q_HEADS * HEAD_DIM
SWA_KV_WIDTH = SWA_KV_HEADS * HEAD_DIM
SWA_WINDOW = 128
ROPE_THETA = 10000.0
MIX_WIDTH = NA_WIDTH + SWA_WIDTH
LN_EPS = 1e-5
MASK_VALUE = -1e30
DEPTH = 1
DEEPNORM_ALPHA = (2.0 * DEPTH) ** 0.25
LOG2E = 1.4426950408889634
Q_SCALE = HEAD_DIM ** -0.5 * LOG2E
N_DR = 2 * NA_KH - 1
N_DC = 2 * NA_KW - 1

LANES = 128
BF16_ROWS = 16
PAIR = 2 * HEAD_DIM
assert PAIR == LANES and 2 * GRID_W == LANES and SWA_KV_WIDTH == LANES

PROJ_TM = 1024
TQ = 128
TICKS = 4
ATTN_VMEM_LIMIT = 60 * 1024 * 1024
Q_ROWS = TQ // GRID_W
NA_KEYS = NA_KH * GRID_W
KEY_PAIRS = NA_KEYS // LANES
N_BIAS_TILES = N_DR - 1
SWA_KEYS = 3 * TQ

_QA = (0, NA_WIDTH)
_VA = (_QA[1], _QA[1] + NA_WIDTH)
_KB = (_VA[1], _VA[1] + SWA_KV_WIDTH)
TOK_WIDTH = _KB[1]
GATE_BLOCK = 256
_T_KA = (0, NA_WIDTH)
_T_QB = (_T_KA[1], _T_KA[1] + SWA_WIDTH)
_T_VB = (_T_QB[1], _T_QB[1] + SWA_KV_WIDTH)
FEAT_ROWS = _T_VB[1]
_W_QA = 0
_W_KA = _W_QA + NA_WIDTH
_W_VA = _W_KA + NA_WIDTH
_W_ZA = _W_VA + NA_WIDTH
_W_QB = _W_ZA + NA_WIDTH
_W_KB = _W_QB + SWA_WIDTH
_W_VB = _W_KB + SWA_KV_WIDTH
_W_ZB = _W_VB + SWA_KV_WIDTH
IN_WIDTH = _W_ZB + SWA_WIDTH


def _swa_band_tiles():
    q = np.arange(TQ)[None, :]
    k = np.arange(SWA_KEYS)[:, None] - TQ
    band = np.abs(k - q) <= SWA_WINDOW
    left_ok = np.concatenate([np.zeros((TQ, 1), bool), np.ones((2 * TQ, 1), bool)], axis=0)
    right_ok = np.concatenate([np.ones((2 * TQ, 1), bool), np.zeros((TQ, 1), bool)], axis=0)
    variants = np.stack([band & left_ok, band, band & right_ok])
    return np.where(variants, 0.0, MASK_VALUE).astype(np.float32)


def _rope_tables(seq):
    inv_freq = ROPE_THETA ** (-jnp.arange(0, HEAD_DIM, 2, dtype=jnp.float32) / HEAD_DIM)
    ang = jnp.arange(seq, dtype=jnp.int32).astype(jnp.float32)[:, None] * inv_freq[None, :]
    cos = jnp.cos(ang)
    sin = jnp.sin(ang)
    cos_head = jnp.concatenate([cos, cos], axis=1)
    sin_head = jnp.concatenate([-sin, sin], axis=1)
    cos_tok = jnp.concatenate([cos_head, cos_head], axis=1)
    sin_tok = jnp.concatenate([sin_head, sin_head], axis=1)
    return cos_tok, sin_tok, cos_head.T * Q_SCALE, sin_head.T * Q_SCALE


def _build_proj_weights(win_ref, wtok_ref, wft_ref):
    bf16 = jnp.bfloat16
    wtok_ref[:, _QA[0]:_QA[1]] = win_ref[:, _W_QA:_W_QA + NA_WIDTH].astype(bf16)
    wtok_ref[:, _VA[0]:_VA[1]] = win_ref[:, _W_VA:_W_VA + NA_WIDTH].astype(bf16)
    wtok_ref[:, _KB[0]:_KB[1]] = win_ref[:, _W_KB:_W_KB + SWA_KV_WIDTH].astype(bf16)
    for dst, src, width in ((_T_KA[0], _W_KA, NA_WIDTH), (_T_QB[0], _W_QB, SWA_WIDTH), (_T_VB[0], _W_VB, SWA_KV_WIDTH)):
        for c in range(width // LANES):
            wft_ref[dst + c * LANES:dst + (c + 1) * LANES, :] = \
                win_ref[:, src + c * LANES:src + (c + 1) * LANES].T.astype(bf16)


def _proj_kernel(x_ref, win_ref, cos_ref, sin_ref, cost_ref, sint_ref,
                 qa_ref, va_ref, kb_ref, kta_ref, qbt_ref, vbt_ref,
                 wtok_ref, wft_ref, carry_ref, *, tiles_per_seq):
    bf16 = jnp.bfloat16
    i = pl.program_id(0)
    half = HEAD_DIM // 2
    n_blk = PROJ_TM // LANES

    @pl.when(i == 0)
    def _():
        _build_proj_weights(win_ref, wtok_ref, wft_ref)

    @pl.when(i % tiles_per_seq == 0)
    def _():
        carry_ref[...] = jnp.zeros_like(carry_ref)

    xb = x_ref[...].astype(bf16)

    def proj(span):
        return jnp.dot(xb, wtok_ref[:, span[0]:span[1]], preferred_element_type=jnp.float32)

    qa_ref[...] = (proj(_QA) * Q_SCALE).astype(bf16)
    va_ref[...] = proj(_VA).astype(bf16)

    kb = proj(_KB)
    lane = lax.broadcasted_iota(jnp.int32, (PROJ_TM, LANES), 1)
    low_half = (lane % HEAD_DIM) < half
    partner = jnp.where(low_half, pltpu.roll(kb, LANES - half, 1), pltpu.roll(kb, half, 1))
    kb_ref[...] = (kb * cos_ref[...] + partner * sin_ref[...]).astype(bf16)

    ft = lax.dot_general(wft_ref[...], xb, (((1,), (1,)), ((), ())), preferred_element_type=jnp.float32)
    cols = [carry_ref[...]] + [ft[_T_KA[0]:_T_KA[1], j * LANES:(j + 1) * LANES] for j in range(n_blk)]
    low_k = lax.broadcasted_iota(jnp.int32, (NA_WIDTH, LANES), 1) < GRID_W
    rolled = [pltpu.roll(c, GRID_W, 1) for c in cols]
    for j in range(n_blk):
        kta_ref[0, 0, j] = cols[j + 1].astype(bf16)
        kta_ref[0, 1, j] = jnp.where(low_k, rolled[j], rolled[j + 1]).astype(bf16)
    carry_ref[...] = cols[n_blk]

    cos_t = cost_ref[...]
    sin_t = sint_ref[...]
    for h in range(SWA_Q_HEADS):
        blk = ft[_T_QB[0] + h * HEAD_DIM:_T_QB[0] + (h + 1) * HEAD_DIM]
        partner_t = jnp.concatenate([blk[half:], blk[:half]], axis=0)
        qbt_ref[0, h * HEAD_DIM:(h + 1) * HEAD_DIM, :] = (blk * cos_t + partner_t * sin_t).astype(bf16)

    vt = ft[_T_VB[0]:_T_VB[1]].astype(bf16)
    for j in range(n_blk):
        vbt_ref[0, j] = vt[:, j * LANES:(j + 1) * LANES]


def _build_na_bias_tiles(rel_ref, tbl_ref):
    c = lax.broadcasted_iota(jnp.int32, (GRID_W, LANES), 0)
    lane = lax.broadcasted_iota(jnp.int32, (GRID_W, LANES), 1)
    kc = lane & (GRID_W - 1)
    start = jnp.clip(c - NA_KW // 2, 0, GRID_W - NA_KW)
    in_win = (kc >= start) & (kc < start + NA_KW)
    low = lane < GRID_W
    masked = jnp.full((GRID_W, LANES), MASK_VALUE, jnp.float32)
    for h in range(NA_HEADS):
        def half(dr, upper):
            row = jnp.broadcast_to(rel_ref[pl.ds(h * N_DR + dr, 1), :] * LOG2E, (GRID_W, LANES))
            shift = (LANES - (NA_KW - 1) + (GRID_W if upper else 0)) % LANES
            return pltpu.roll(row, shift, 1, stride=1, stride_axis=0)

        lower = [half(dr, False) for dr in range(N_DR - 1)]
        upper = [half(dr, True) for dr in range(1, N_DR)]
        for d in range(N_BIAS_TILES):
            tbl_ref[d, h] = jnp.where(in_win, jnp.where(low, lower[d], upper[d]), masked)


def _attn_kernel(sink_ref, *refs, rows, n_blocks, n_total):
    refs = list(refs)
    take = lambda n: [refs.pop(0) for _ in range(n)]
    qa_refs = take(TICKS)
    kta_lo, kta_hi, va_ref = take(3)
    qbt_refs = take(TICKS)
    kb_lo, kb_hi, vbt_ref, x_ref, wout_ref = take(5)
    wz_refs = take(MIX_WIDTH // GATE_BLOCK)
    rel_ref, band_ref, gain_ref, beta_ref, out_ref, woutb_ref, wzb_ref, tbl_ref, sna_ref, ssw_ref = refs
    j = pl.program_id(0)

    @pl.when(j == 0)
    def _():
        woutb_ref[...] = wout_ref[...].astype(jnp.bfloat16)
        for c, wz_ref in enumerate(wz_refs):
            wzb_ref[:, c * GATE_BLOCK:(c + 1) * GATE_BLOCK] = wz_ref[...].astype(jnp.bfloat16)
        _build_na_bias_tiles(rel_ref, tbl_ref)
        sna_ref[0] = jnp.zeros(sna_ref.shape[1:], sna_ref.dtype)
        ssw_ref[0] = jnp.zeros(ssw_ref.shape[1:], ssw_ref.dtype)

    for h in range(TICKS):
        last = h == TICKS - 1
        rows_h = pl.ds(h * TQ, TQ)
        rd, wr = h % 2, 1 - h % 2
        _attn_step(TICKS * (j - 1) + 1 + h, sink_ref, qa_refs[h], kta_hi if last else kta_lo, va_ref, qbt_refs[h],
                   kb_hi if last else kb_lo, vbt_ref, wzb_ref,
                   x_ref.at[rows_h], band_ref, gain_ref, beta_ref, out_ref.at[rows_h], woutb_ref, tbl_ref,
                   sna_ref.at[rd], ssw_ref.at[rd], sna_ref.at[wr], ssw_ref.at[wr],
                   rows=rows, n_blocks=n_blocks, n_total=n_total)


def _attn_step(i, sink_ref, qa_ref, kta_ref, va_ref, qbt_ref, kb_ref, vbt_ref, wzb_ref, x_ref,
               band_ref, gain_ref, beta_ref, out_ref, woutb_ref, tbl_ref,
               sna_rd, ssw_rd, sna_wr, ssw_wr, *, rows, n_blocks, n_total):
    f32, bf16 = jnp.float32, jnp.bfloat16
    kh = min(NA_KH, rows)
    n_pairs = NA_HEADS // 2
    lane = lax.broadcasted_iota(jnp.int32, (GRID_W, LANES), 1)
    low = lane < HEAD_DIM

    def window_start(t, qrow):
        r = Q_ROWS * t + qrow
        return r, jnp.clip(r - kh // 2, 0, rows - kh)

    t = jnp.maximum(i - 1, 0) % n_blocks
    jl = jnp.maximum(t - 1, 0)
    jr = jnp.minimum(t + 1, n_blocks - 1)

    na_e, na_v = [], []
    for qrow in range(Q_ROWS):
        _, r0 = window_start(t, qrow)
        key0 = pl.multiple_of(r0 * GRID_W, GRID_W)
        for p in range(n_pairs):
            s_ref = sna_rd.at[qrow * n_pairs + p]
            m = jnp.max(s_ref[...], axis=-1, keepdims=True)
            na_e.append(jnp.exp2(s_ref[...] - m).astype(bf16))
            na_v.append(va_ref[0, pl.ds(key0, NA_KEYS), p * LANES:(p + 1) * LANES])
    sw_e, sw_sink, sw_v = [], [], []
    for g in range(SWA_KV_HEADS):
        s_ref = ssw_rd.at[g]
        sink = jnp.concatenate(
            [jnp.full((1, TQ), sink_ref[SWA_GROUP * g + j] * LOG2E, f32) for j in range(SWA_GROUP)], axis=1)
        m = jnp.maximum(jnp.max(s_ref[...], axis=0, keepdims=True), sink)
        sw_e.append(jnp.exp2(s_ref[...] - m).astype(bf16))
        sw_sink.append(jnp.exp2(sink - m))
        sw_v.append(jnp.concatenate(
            [vbt_ref[0, j, g * HEAD_DIM:(g + 1) * HEAD_DIM, :] for j in (jl, t, jr)], axis=1))

    na_out = []
    for u in range(Q_ROWS * n_pairs):
        v1 = jnp.concatenate([na_v[u], jnp.ones_like(na_v[u])], axis=1)
        o = jnp.dot(na_e[u], v1, preferred_element_type=f32)
        o = o[:, :LANES] * (1.0 / o[:, LANES:])
        na_out.append(jnp.where(low, o[:GRID_W], o[GRID_W:]))
    mixed_cols = [jnp.concatenate([na_out[qrow * n_pairs + p] for qrow in range(Q_ROWS)], axis=0)
                  for p in range(n_pairs)]
    for g in range(SWA_KV_HEADS):
        v1 = jnp.concatenate([sw_v[g], jnp.ones((BF16_ROWS, SWA_KEYS), bf16)], axis=0)
        o = jnp.dot(v1, sw_e[g], preferred_element_type=f32)
        o = o[:HEAD_DIM] * (1.0 / (o[HEAD_DIM:HEAD_DIM + 1] + sw_sink[g]))
        for pp in range(SWA_GROUP // 2):
            both = jnp.concatenate([o[:, (2 * pp) * TQ:(2 * pp + 1) * TQ],
                                    o[:, (2 * pp + 1) * TQ:(2 * pp + 2) * TQ]], axis=0)
            mixed_cols.append(both.T)

    mixed = jnp.concatenate(mixed_cols, axis=1)
    x_blk = x_ref[...]
    z = jnp.dot(x_blk.astype(bf16), wzb_ref[...], preferred_element_type=f32)
    mixed = (mixed * (z * jax.nn.sigmoid(z))).astype(bf16)
    y = jnp.dot(mixed, woutb_ref[...], preferred_element_type=f32)
    r = DEEPNORM_ALPHA * x_blk + y
    mu = jnp.mean(r, axis=-1, keepdims=True)
    d = r - mu
    var = jnp.mean(d * d, axis=-1, keepdims=True)
    out_ref[...] = d * lax.rsqrt(var + LN_EPS) * gain_ref[...] + beta_ref[...]

    t1 = jnp.clip(i, 0, n_total - 1) % n_blocks
    for qrow in range(Q_ROWS):
        r, r0 = window_start(t1, qrow)
        par = r0 & 1
        d0 = r0 - r + (NA_KH - 1)
        for p in range(n_pairs):
            cols = slice(p * LANES, (p + 1) * LANES)
            qp = qa_ref[qrow * GRID_W:(qrow + 1) * GRID_W, cols]
            zero = jnp.zeros_like(qp)
            q2 = jnp.concatenate([jnp.where(low, qp, zero), jnp.where(low, zero, qp)], axis=0)
            kt = jnp.concatenate([kta_ref[0, par, (r0 + 2 * j + 1) // 2, cols, :] for j in range(KEY_PAIRS)],
                                 axis=1)
            bias = jnp.concatenate(
                [jnp.concatenate([tbl_ref[d0 + 2 * j, 2 * p + hp] for j in range(KEY_PAIRS)], axis=1)
                 for hp in range(2)], axis=0)
            sna_wr[qrow * n_pairs + p] = jnp.dot(q2, kt, preferred_element_type=f32) + bias
    jl1 = jnp.maximum(t1 - 1, 0)
    jr1 = jnp.minimum(t1 + 1, n_blocks - 1)
    variant = jnp.where(t1 == 0, 0, jnp.where(t1 == n_blocks - 1, 2, 1))
    band = band_ref[variant]
    band4 = jnp.concatenate([band] * SWA_GROUP, axis=1)
    k3 = jnp.concatenate([kb_ref[0, pl.ds(pl.multiple_of(j * TQ, TQ), TQ), :] for j in (jl1, t1, jr1)],
                         axis=0)
    for g in range(SWA_KV_HEADS):
        qt = jnp.concatenate(
            [qbt_ref[0, (SWA_GROUP * g + j) * HEAD_DIM:(SWA_GROUP * g + j + 1) * HEAD_DIM, :]
             for j in range(SWA_GROUP)], axis=1)
        parts = [jnp.zeros_like(qt)] * SWA_KV_HEADS
        parts[g] = qt
        ssw_wr[g] = jnp.dot(k3, jnp.concatenate(parts, axis=0), preferred_element_type=f32) + band4


def kernel(x, w_in, rel_pos_bias, sink_logits, w_out, ln_gain, ln_bias):
    B, S, D = x.shape
    assert D == D_MODEL and S % PROJ_TM == 0 and S % GRID_W == 0
    rows = S // GRID_W
    assert rows >= NA_KH and rows % Q_ROWS == 0
    n_blocks = S // TQ
    n_tok = B * S
    bf16 = jnp.bfloat16
    x2 = x.reshape(n_tok, D)

    cos_tok, sin_tok, cos_feat, sin_feat = _rope_tables(S)
    band = jnp.asarray(_swa_band_tiles())

    for layer in range(DEPTH):
        tiles_per_seq = S // PROJ_TM
        blk128 = PROJ_TM // LANES
        seq_tile = lambda i: (i // tiles_per_seq, i % tiles_per_seq)
        qa, va, kb, kta, qbt, vbt = pl.pallas_call(
            functools.partial(_proj_kernel, tiles_per_seq=tiles_per_seq),
            grid=(n_tok // PROJ_TM,),
            in_specs=[
                pl.BlockSpec((PROJ_TM, D), lambda i: (i, 0)),
                pl.BlockSpec((D, IN_WIDTH), lambda i: (0, 0), pipeline_mode=pl.Buffered(1)),
                pl.BlockSpec((PROJ_TM, LANES), lambda i: (i % tiles_per_seq, 0)),
                pl.BlockSpec((PROJ_TM, LANES), lambda i: (i % tiles_per_seq, 0)),
                pl.BlockSpec((HEAD_DIM, PROJ_TM), lambda i: (0, i % tiles_per_seq)),
                pl.BlockSpec((HEAD_DIM, PROJ_TM), lambda i: (0, i % tiles_per_seq)),
            ],
            out_specs=[
                pl.BlockSpec((PROJ_TM, NA_WIDTH), lambda i: (i, 0)),
                pl.BlockSpec((PROJ_TM, NA_WIDTH), lambda i: (i, 0)),
                pl.BlockSpec((PROJ_TM, SWA_KV_WIDTH), lambda i: (i, 0)),
                pl.BlockSpec((1, 2, blk128, NA_WIDTH, LANES), lambda i: (seq_tile(i)[0], 0, seq_tile(i)[1], 0, 0)),
                pl.BlockSpec((1, SWA_WIDTH, PROJ_TM), lambda i: (seq_tile(i)[0], 0, seq_tile(i)[1])),
                pl.BlockSpec((1, blk128, SWA_KV_WIDTH, LANES), lambda i: (seq_tile(i)[0], seq_tile(i)[1], 0, 0)),
            ],
            out_shape=[
                jax.ShapeDtypeStruct((n_tok, NA_WIDTH), bf16),
                jax.ShapeDtypeStruct((n_tok, NA_WIDTH), bf16),
                jax.ShapeDtypeStruct((n_tok, SWA_KV_WIDTH), bf16),
                jax.ShapeDtypeStruct((B, 2, S // LANES, NA_WIDTH, LANES), bf16),
                jax.ShapeDtypeStruct((B, SWA_WIDTH, S), bf16),
                jax.ShapeDtypeStruct((B, S // LANES, SWA_KV_WIDTH, LANES), bf16),
            ],
            scratch_shapes=[pltpu.VMEM((D, TOK_WIDTH), bf16), pltpu.VMEM((FEAT_ROWS, D), bf16),
                            pltpu.VMEM((NA_WIDTH, LANES), jnp.float32)],
            compiler_params=pltpu.CompilerParams(
                dimension_semantics=("arbitrary",), vmem_limit_bytes=56 * 1024 * 1024),
            name="in_proj",
        )(x2, w_in[layer], cos_tok, sin_tok, cos_feat, sin_feat)

        rel_rows = jnp.pad(rel_pos_bias[layer].astype(jnp.float32).reshape(NA_HEADS * N_DR, N_DC),
                           ((0, 0), (0, LANES - N_DC)))
        n_total = B * n_blocks
        gate_blocks = [(_W_ZA + c * GATE_BLOCK) // GATE_BLOCK for c in range(NA_WIDTH // GATE_BLOCK)] + \
                      [(_W_ZB + c * GATE_BLOCK) // GATE_BLOCK for c in range(SWA_WIDTH // GATE_BLOCK)]
        assert _W_ZA % GATE_BLOCK == 0 and _W_ZB % GATE_BLOCK == 0
        assert TICKS % 2 == 0 and n_blocks % TICKS == 0
        blk_of = lambda h: (lambda j: jnp.clip(TICKS * (j - 1) + 1 + h, 0, n_total - 1))
        lo, hi = blk_of(TICKS - 2), blk_of(TICKS - 1)
        fin = lambda j: jnp.maximum(j - 1, 0)
        q_spec = lambda blk: pl.BlockSpec((TQ, NA_WIDTH), lambda j, *_: (blk(j), 0))
        kta_spec = lambda blk: pl.BlockSpec((1, 2, S // LANES, NA_WIDTH, LANES),
                                            lambda j, *_: (blk(j) // n_blocks, 0, 0, 0, 0))
        qbt_spec = lambda blk: pl.BlockSpec((1, SWA_WIDTH, TQ),
                                            lambda j, *_: (blk(j) // n_blocks, 0, blk(j) % n_blocks))
        kb_spec = lambda blk: pl.BlockSpec((1, S, SWA_KV_WIDTH), lambda j, *_: (blk(j) // n_blocks, 0, 0))
        fin_batch = lambda j: (TICKS * fin(j)) // n_blocks
        grid_spec = pltpu.PrefetchScalarGridSpec(
            num_scalar_prefetch=1,
            grid=(n_total // TICKS + 1,),
            in_specs=[
                *[q_spec(blk_of(h)) for h in range(TICKS)],
                kta_spec(lo), kta_spec(hi),
                pl.BlockSpec((1, S, NA_WIDTH), lambda j, *_: (fin_batch(j), 0, 0)),
                *[qbt_spec(blk_of(h)) for h in range(TICKS)],
                kb_spec(lo), kb_spec(hi),
                pl.BlockSpec((1, S // LANES, SWA_KV_WIDTH, LANES), lambda j, *_: (fin_batch(j), 0, 0, 0)),
                pl.BlockSpec((TICKS * TQ, D), lambda j, *_: (fin(j), 0)),
                pl.BlockSpec((MIX_WIDTH, D), lambda j, *_: (0, 0), pipeline_mode=pl.Buffered(1)),
                *[pl.BlockSpec((D, GATE_BLOCK), functools.partial(lambda c, j, *_: (0, c), c),
                               pipeline_mode=pl.Buffered(1)) for c in gate_blocks],
                pl.BlockSpec((NA_HEADS * N_DR, LANES), lambda j, *_: (0, 0)),
                pl.BlockSpec((3, SWA_KEYS, TQ), lambda j, *_: (0, 0, 0)),
                pl.BlockSpec((1, D), lambda j, *_: (0, 0)),
                pl.BlockSpec((1, D), lambda j, *_: (0, 0)),
            ],
            out_specs=pl.BlockSpec((TICKS * TQ, D), lambda j, *_: (fin(j), 0)),
            scratch_shapes=[pltpu.VMEM((MIX_WIDTH, D), bf16),
                            pltpu.VMEM((D, MIX_WIDTH), bf16),
                            pltpu.VMEM((N_BIAS_TILES, NA_HEADS, GRID_W, 2 * GRID_W), jnp.float32),
                            pltpu.VMEM((2, Q_ROWS * NA_HEADS // 2, TQ, NA_KEYS), jnp.float32),
                            pltpu.VMEM((2, SWA_KV_HEADS, SWA_KEYS, SWA_GROUP * TQ), jnp.float32)],
        )
        x2 = pl.pallas_call(
            functools.partial(_attn_kernel, rows=rows, n_blocks=n_blocks, n_total=n_total),
            grid_spec=grid_spec,
            out_shape=jax.ShapeDtypeStruct((n_tok, D), jnp.float32),
            compiler_params=pltpu.CompilerParams(
                dimension_semantics=("arbitrary",), vmem_limit_bytes=ATTN_VMEM_LIMIT),
            name="attn_out",
        )(sink_logits[layer].astype(jnp.float32),
          *([qa] * TICKS), kta, kta, va.reshape(B, S, NA_WIDTH), *([qbt] * TICKS),
          kb.reshape(B, S, SWA_KV_WIDTH), kb.reshape(B, S, SWA_KV_WIDTH), vbt, x2,
          w_out[layer], *([w_in[layer]] * len(gate_blocks)), rel_rows, band, ln_gain[layer].reshape(1, D), ln_bias[layer].reshape(1, D))
    return x2.reshape(B, S, D)
```

```python
import functools

import numpy as np
import jax
import jax.numpy as jnp
from jax import lax
from jax.experimental import pallas as pl
from jax.experimental.pallas import tpu as pltpu

D_MODEL = 1024
HEAD_DIM = 64
NA_HEADS = 8
NA_WIDTH = NA_HEADS * HEAD_DIM
NA_KH = 8
NA_KW = 16
GRID_W = 64
SWA_Q_HEADS = 8
SWA_KV_HEADS = 2
SWA_GROUP = SWA_Q_HEADS // SWA_KV_HEADS
SWA_WIDTH = SWA_Q_HEADS * HEAD_DIM
SWA_KV_WIDTH = SWA_KV_HEADS * HEAD_DIM
SWA_WINDOW = 128
ROPE_THETA = 10000.0
MIX_WIDTH = NA_WIDTH + SWA_WIDTH
LN_EPS = 1e-5
MASK_VALUE = -1e30
DEPTH = 1
DEEPNORM_ALPHA = (2.0 * DEPTH) ** 0.25
LOG2E = 1.4426950408889634
Q_SCALE = HEAD_DIM ** -0.5 * LOG2E
N_DR = 2 * NA_KH - 1
N_DC = 2 * NA_KW - 1

LANES = 128
BF16_ROWS = 16
PAIR = 2 * HEAD_DIM
assert PAIR == LANES and 2 * GRID_W == LANES and SWA_KV_WIDTH == LANES

PROJ_TM = 1024
TQ = 128
TICKS = 4
V7X_VMEM_BYTES = 64 * 1024 * 1024
PROJ_VMEM_LIMIT = V7X_VMEM_BYTES * 7 // 8
ATTN_VMEM_LIMIT = V7X_VMEM_BYTES * 15 // 16
Q_ROWS = TQ // GRID_W
NA_KEYS = NA_KH * GRID_W
KEY_PAIRS = NA_KEYS // LANES
N_BIAS_TILES = N_DR - 1
SWA_KEYS = 3 * TQ

_QA = (0, NA_WIDTH)
_VA = (_QA[1], _QA[1] + NA_WIDTH)
_KB = (_VA[1], _VA[1] + SWA_KV_WIDTH)
TOK_WIDTH = _KB[1]
GATE_BLOCK = 256
_T_KA = (0, NA_WIDTH)
_T_QB = (_T_KA[1], _T_KA[1] + SWA_WIDTH)
_T_VB = (_T_QB[1], _T_QB[1] + SWA_KV_WIDTH)
FEAT_ROWS = _T_VB[1]
_W_QA = 0
_W_KA = _W_QA + NA_WIDTH
_W_VA = _W_KA + NA_WIDTH
_W_ZA = _W_VA + NA_WIDTH
_W_QB = _W_ZA + NA_WIDTH
_W_KB = _W_QB + SWA_WIDTH
_W_VB = _W_KB + SWA_KV_WIDTH
_W_ZB = _W_VB + SWA_KV_WIDTH
IN_WIDTH = _W_ZB + SWA_WIDTH


def _swa_band_tiles():
    q = np.arange(TQ)[None, :]
    k = np.arange(SWA_KEYS)[:, None] - TQ
    band = np.abs(k - q) <= SWA_WINDOW
    left_ok = np.concatenate([np.zeros((TQ, 1), bool), np.ones((2 * TQ, 1), bool)], axis=0)
    right_ok = np.concatenate([np.ones((2 * TQ, 1), bool), np.zeros((TQ, 1), bool)], axis=0)
    variants = np.stack([band & left_ok, band, band & right_ok])
    return np.where(variants, 0.0, MASK_VALUE).astype(np.float32)


def _rope_tables(seq):
    inv_freq = ROPE_THETA ** (-jnp.arange(0, HEAD_DIM, 2, dtype=jnp.float32) / HEAD_DIM)
    ang = jnp.arange(seq, dtype=jnp.int32).astype(jnp.float32)[:, None] * inv_freq[None, :]
    cos = jnp.cos(ang)
    sin = jnp.sin(ang)
    cos_head = jnp.concatenate([cos, cos], axis=1)
    sin_head = jnp.concatenate([-sin, sin], axis=1)
    cos_tok = jnp.concatenate([cos_head, cos_head], axis=1)
    sin_tok = jnp.concatenate([sin_head, sin_head], axis=1)
    return cos_tok, sin_tok, cos_head.T * Q_SCALE, sin_head.T * Q_SCALE


def _build_proj_weights(win_ref, wtok_ref, wft_ref):
    bf16 = jnp.bfloat16
    wtok_ref[:, _QA[0]:_QA[1]] = win_ref[:, _W_QA:_W_QA + NA_WIDTH].astype(bf16)
    wtok_ref[:, _VA[0]:_VA[1]] = win_ref[:, _W_VA:_W_VA + NA_WIDTH].astype(bf16)
    wtok_ref[:, _KB[0]:_KB[1]] = win_ref[:, _W_KB:_W_KB + SWA_KV_WIDTH].astype(bf16)
    for dst, src, width in ((_T_KA[0], _W_KA, NA_WIDTH), (_T_QB[0], _W_QB, SWA_WIDTH), (_T_VB[0], _W_VB, SWA_KV_WIDTH)):
        for c in range(width // LANES):
            wft_ref[dst + c * LANES:dst + (c + 1) * LANES, :] = \
                win_ref[:, src + c * LANES:src + (c + 1) * LANES].T.astype(bf16)


def _proj_kernel(x_ref, win_ref, cos_ref, sin_ref, cost_ref, sint_ref,
                 qa_ref, va_ref, kb_ref, kta_ref, qbt_ref, vbt_ref,
                 wtok_ref, wft_ref, carry_ref, *, tiles_per_seq):
    bf16 = jnp.bfloat16
    i = pl.program_id(0)
    half = HEAD_DIM // 2
    n_blk = PROJ_TM // LANES

    @pl.when(i == 0)
    def _():
        _build_proj_weights(win_ref, wtok_ref, wft_ref)

    @pl.when(i % tiles_per_seq == 0)
    def _():
        carry_ref[...] = jnp.zeros_like(carry_ref)

    xb = x_ref[...].astype(bf16)

    def proj(span):
        return jnp.dot(xb, wtok_ref[:, span[0]:span[1]], preferred_element_type=jnp.float32)

    qa_ref[...] = (proj(_QA) * Q_SCALE).astype(bf16)
    va_ref[...] = proj(_VA).astype(bf16)

    kb = proj(_KB)
    lane = lax.broadcasted_iota(jnp.int32, (PROJ_TM, LANES), 1)
    low_half = (lane % HEAD_DIM) < half
    partner = jnp.where(low_half, pltpu.roll(kb, LANES - half, 1), pltpu.roll(kb, half, 1))
    kb_ref[...] = (kb * cos_ref[...] + partner * sin_ref[...]).astype(bf16)

    ft = lax.dot_general(wft_ref[...], xb, (((1,), (1,)), ((), ())), preferred_element_type=jnp.float32)
    cols = [carry_ref[...]] + [ft[_T_KA[0]:_T_KA[1], j * LANES:(j + 1) * LANES] for j in range(n_blk)]
    low_k = lax.broadcasted_iota(jnp.int32, (NA_WIDTH, LANES), 1) < GRID_W
    rolled = [pltpu.roll(c, GRID_W, 1) for c in cols]
    for j in range(n_blk):
        kta_ref[0, 0, j] = cols[j + 1].astype(bf16)
        kta_ref[0, 1, j] = jnp.where(low_k, rolled[j], rolled[j + 1]).astype(bf16)
    carry_ref[...] = cols[n_blk]

    cos_t = cost_ref[...]
    sin_t = sint_ref[...]
    for h in range(SWA_Q_HEADS):
        blk = ft[_T_QB[0] + h * HEAD_DIM:_T_QB[0] + (h + 1) * HEAD_DIM]
        partner_t = jnp.concatenate([blk[half:], blk[:half]], axis=0)
        qbt_ref[0, h * HEAD_DIM:(h + 1) * HEAD_DIM, :] = (blk * cos_t + partner_t * sin_t).astype(bf16)

    vt = ft[_T_VB[0]:_T_VB[1]].astype(bf16)
    for j in range(n_blk):
        vbt_ref[0, j] = vt[:, j * LANES:(j + 1) * LANES]


def _build_na_bias_tiles(rel_ref, tbl_ref):
    c = lax.broadcasted_iota(jnp.int32, (GRID_W, LANES), 0)
    lane = lax.broadcasted_iota(jnp.int32, (GRID_W, LANES), 1)
    kc = lane & (GRID_W - 1)
    start = jnp.clip(c - NA_KW // 2, 0, GRID_W - NA_KW)
    in_win = (kc >= start) & (kc < start + NA_KW)
    low = lane < GRID_W
    masked = jnp.full((GRID_W, LANES), MASK_VALUE, jnp.float32)
    for h in range(NA_HEADS):
        def half(dr, upper):
            row = jnp.broadcast_to(rel_ref[pl.ds(h * N_DR + dr, 1), :] * LOG2E, (GRID_W, LANES))
            shift = (LANES - (NA_KW - 1) + (GRID_W if upper else 0)) % LANES
            return pltpu.roll(row, shift, 1, stride=1, stride_axis=0)

        lower = [half(dr, False) for dr in range(N_DR - 1)]
        upper = [half(dr, True) for dr in range(1, N_DR)]
        for d in range(N_BIAS_TILES):
            tbl_ref[d, h] = jnp.where(in_win, jnp.where(low, lower[d], upper[d]), masked)


def _attn_kernel(sink_ref, *refs, rows, n_blocks, n_total):
    refs = list(refs)
    take = lambda n: [refs.pop(0) for _ in range(n)]
    qa_refs = take(TICKS)
    kta_lo, kta_hi, va_ref = take(3)
    qbt_refs = take(TICKS)
    kb_lo, kb_hi, vbt_ref, x_ref, wout_ref = take(5)
    wz_refs = take(MIX_WIDTH // GATE_BLOCK)
    rel_ref, band_ref, gain_ref, beta_ref, out_ref, woutb_ref, wzb_ref, tbl_ref, sna_ref, ssw_ref = refs
    j = pl.program_id(0)

    @pl.when(j == 0)
    def _():
        woutb_ref[...] = wout_ref[...].astype(jnp.bfloat16)
        for c, wz_ref in enumerate(wz_refs):
            wzb_ref[:, c * GATE_BLOCK:(c + 1) * GATE_BLOCK] = wz_ref[...].astype(jnp.bfloat16)
        _build_na_bias_tiles(rel_ref, tbl_ref)
        sna_ref[0] = jnp.zeros(sna_ref.shape[1:], sna_ref.dtype)
        ssw_ref[0] = jnp.zeros(ssw_ref.shape[1:], ssw_ref.dtype)

    n_units = Q_ROWS * NA_HEADS // 2
    for h in range(TICKS):
        i = TICKS * (j - 1) + 1 + h
        last = h == TICKS - 1
        rd, wr = h % 2, 1 - h % 2
        rows_h = pl.ds(h * TQ, TQ)
        ena, esw, snk = [None] * n_units, [None] * SWA_KV_HEADS, [None] * SWA_KV_HEADS
        sm_na, sm_sw = _softmax_steps(sink_ref, sna_ref.at[rd], ssw_ref.at[rd], ena, esw, snk)
        out_na, out_sw, out_tail = _output_steps(
            jnp.maximum(i - 1, 0) % n_blocks, va_ref, vbt_ref, wzb_ref, woutb_ref, x_ref.at[rows_h],
            gain_ref, beta_ref, out_ref.at[rows_h], ena, esw, snk, rows=rows, n_blocks=n_blocks)
        sc_na, sc_sw = _score_steps(jnp.clip(i, 0, n_total - 1) % n_blocks, qa_refs[h], kta_hi if last else kta_lo,
                                    qbt_refs[h], kb_hi if last else kb_lo, band_ref, tbl_ref,
                                    sna_ref.at[wr], ssw_ref.at[wr], rows=rows, n_blocks=n_blocks)
        _emit(sm_na, sm_sw, out_na, out_sw, out_tail, sc_na, sc_sw)


def _emit(sm_na, sm_sw, out_na, out_sw, out_tail, sc_na, sc_sw):
    for sm, pvs, scs in ((sm_na, out_na, sc_na), (sm_sw, out_sw, sc_sw)):
        for step in sm:
            step()
        for pv, sc in zip(pvs, scs):
            pv()
            sc()
    out_tail()


def _window_start(t, qrow, rows):
    kh = min(NA_KH, rows)
    r = Q_ROWS * t + qrow
    return r, jnp.clip(r - kh // 2, 0, rows - kh)


def _softmax_steps(sink_ref, sna_rd, ssw_rd, ena_wr, esw_wr, snk_wr):
    f32, bf16 = jnp.float32, jnp.bfloat16

    def na(u):
        s_ref = sna_rd.at[u]
        m = jnp.max(s_ref[...], axis=-1, keepdims=True)
        ena_wr[u] = jnp.exp2(s_ref[...] - m).astype(bf16)

    def sw(g):
        s_ref = ssw_rd.at[g]
        sink = jnp.concatenate(
            [jnp.full((1, TQ), sink_ref[SWA_GROUP * g + j] * LOG2E, f32) for j in range(SWA_GROUP)], axis=1)
        m = jnp.maximum(jnp.max(s_ref[...], axis=0, keepdims=True), sink)
        esw_wr[g] = jnp.exp2(s_ref[...] - m).astype(bf16)
        snk_wr[g] = jnp.exp2(sink - m)

    return ([functools.partial(na, u) for u in range(Q_ROWS * NA_HEADS // 2)],
            [functools.partial(sw, g) for g in range(SWA_KV_HEADS)])


def _output_steps(t, va_ref, vbt_ref, wzb_ref, woutb_ref, x_ref, gain_ref, beta_ref, out_ref,
                  ena_rd, esw_rd, snk_rd, *, rows, n_blocks):
    f32, bf16 = jnp.float32, jnp.bfloat16
    n_pairs = NA_HEADS // 2
    low = lax.broadcasted_iota(jnp.int32, (GRID_W, LANES), 1) < HEAD_DIM
    jl = jnp.maximum(t - 1, 0)
    jr = jnp.minimum(t + 1, n_blocks - 1)
    na_out = [None] * (Q_ROWS * n_pairs)
    sw_cols = [None] * (SWA_Q_HEADS // 2)

    def na(u):
        qrow, p = divmod(u, n_pairs)
        _, r0 = _window_start(t, qrow, rows)
        key0 = pl.multiple_of(r0 * GRID_W, GRID_W)
        v = va_ref[0, pl.ds(key0, NA_KEYS), p * LANES:(p + 1) * LANES]
        v1 = jnp.concatenate([v, jnp.ones_like(v)], axis=1)
        o = jnp.dot(ena_rd[u], v1, preferred_element_type=f32)
        o = o[:, :LANES] * (1.0 / o[:, LANES:])
        na_out[u] = jnp.where(low, o[:GRID_W], o[GRID_W:])

    def sw(g):
        vt = jnp.concatenate([vbt_ref[0, j, g * HEAD_DIM:(g + 1) * HEAD_DIM, :] for j in (jl, t, jr)], axis=1)
        v1 = jnp.concatenate([vt, jnp.ones((BF16_ROWS, SWA_KEYS), bf16)], axis=0)
        o = jnp.dot(v1, esw_rd[g], preferred_element_type=f32)
        o = o[:HEAD_DIM] * (1.0 / (o[HEAD_DIM:HEAD_DIM + 1] + snk_rd[g]))
        for pp in range(SWA_GROUP // 2):
            both = jnp.concatenate([o[:, (2 * pp) * TQ:(2 * pp + 1) * TQ],
                                    o[:, (2 * pp + 1) * TQ:(2 * pp + 2) * TQ]], axis=0)
            sw_cols[(SWA_GROUP // 2) * g + pp] = both.T

    def tail():
        mixed_cols = [jnp.concatenate([na_out[qrow * n_pairs + p] for qrow in range(Q_ROWS)], axis=0)
                      for p in range(n_pairs)]
        mixed = jnp.concatenate(mixed_cols + sw_cols, axis=1)
        x_blk = x_ref[...]
        z = jnp.dot(x_blk.astype(bf16), wzb_ref[...], preferred_element_type=f32)
        mixed = (mixed * (z * jax.nn.sigmoid(z))).astype(bf16)
        y = jnp.dot(mixed, woutb_ref[...], preferred_element_type=f32)
        r = DEEPNORM_ALPHA * x_blk + y
        mu = jnp.mean(r, axis=-1, keepdims=True)
        d = r - mu
        var = jnp.mean(d * d, axis=-1, keepdims=True)
        out_ref[...] = d * lax.rsqrt(var + LN_EPS) * gain_ref[...] + beta_ref[...]

    return ([functools.partial(na, u) for u in range(Q_ROWS * n_pairs)],
            [functools.partial(sw, g) for g in range(SWA_KV_HEADS)], tail)


def _score_steps(t1, qa_ref, kta_ref, qbt_ref, kb_ref, band_ref, tbl_ref, sna_wr, ssw_wr, *, rows, n_blocks):
    f32 = jnp.float32
    n_pairs = NA_HEADS // 2
    low = lax.broadcasted_iota(jnp.int32, (GRID_W, LANES), 1) < HEAD_DIM

    def na(u):
        qrow, p = divmod(u, n_pairs)
        r, r0 = _window_start(t1, qrow, rows)
        par = r0 & 1
        d0 = r0 - r + (NA_KH - 1)
        cols = slice(p * LANES, (p + 1) * LANES)
        qp = qa_ref[qrow * GRID_W:(qrow + 1) * GRID_W, cols]
        zero = jnp.zeros_like(qp)
        q2 = jnp.concatenate([jnp.where(low, qp, zero), jnp.where(low, zero, qp)], axis=0)
        kt = jnp.concatenate([kta_ref[0, par, (r0 + 2 * j + 1) // 2, cols, :] for j in range(KEY_PAIRS)],
                             axis=1)
        bias = jnp.concatenate(
            [jnp.concatenate([tbl_ref[d0 + 2 * j, 2 * p + hp] for j in range(KEY_PAIRS)], axis=1)
             for hp in range(2)], axis=0)
        sna_wr[u] = jnp.dot(q2, kt, preferred_element_type=f32) + bias

    def sw(g):
        jl1 = jnp.maximum(t1 - 1, 0)
        jr1 = jnp.minimum(t1 + 1, n_blocks - 1)
        variant = jnp.where(t1 == 0, 0, jnp.where(t1 == n_blocks - 1, 2, 1))
        band = band_ref[variant]
        band4 = jnp.concatenate([band] * SWA_GROUP, axis=1)
        k3 = jnp.concatenate([kb_ref[0, pl.ds(pl.multiple_of(j * TQ, TQ), TQ), :] for j in (jl1, t1, jr1)],
                             axis=0)
        qt = jnp.concatenate(
            [qbt_ref[0, (SWA_GROUP * g + j) * HEAD_DIM:(SWA_GROUP * g + j + 1) * HEAD_DIM, :]
             for j in range(SWA_GROUP)], axis=1)
        parts = [jnp.zeros_like(qt)] * SWA_KV_HEADS
        parts[g] = qt
        ssw_wr[g] = jnp.dot(k3, jnp.concatenate(parts, axis=0), preferred_element_type=f32) + band4

    return ([functools.partial(na, u) for u in range(Q_ROWS * n_pairs)],
            [functools.partial(sw, g) for g in range(SWA_KV_HEADS)])


def kernel(x, w_in, rel_pos_bias, sink_logits, w_out, ln_gain, ln_bias):
    B, S, D = x.shape
    assert D == D_MODEL and S % PROJ_TM == 0 and S % GRID_W == 0
    rows = S // GRID_W
    assert rows >= NA_KH and rows % Q_ROWS == 0
    n_blocks = S // TQ
    n_tok = B * S
    bf16 = jnp.bfloat16
    x2 = x.reshape(n_tok, D)

    cos_tok, sin_tok, cos_feat, sin_feat = _rope_tables(S)
    band = jnp.asarray(_swa_band_tiles())

    for layer in range(DEPTH):
        tiles_per_seq = S // PROJ_TM
        blk128 = PROJ_TM // LANES
        seq_tile = lambda i: (i // tiles_per_seq, i % tiles_per_seq)
        qa, va, kb, kta, qbt, vbt = pl.pallas_call(
            functools.partial(_proj_kernel, tiles_per_seq=tiles_per_seq),
            grid=(n_tok // PROJ_TM,),
            in_specs=[
                pl.BlockSpec((PROJ_TM, D), lambda i: (i, 0)),
                pl.BlockSpec((D, IN_WIDTH), lambda i: (0, 0), pipeline_mode=pl.Buffered(1)),
                pl.BlockSpec((PROJ_TM, LANES), lambda i: (i % tiles_per_seq, 0)),
                pl.BlockSpec((PROJ_TM, LANES), lambda i: (i % tiles_per_seq, 0)),
                pl.BlockSpec((HEAD_DIM, PROJ_TM), lambda i: (0, i % tiles_per_seq)),
                pl.BlockSpec((HEAD_DIM, PROJ_TM), lambda i: (0, i % tiles_per_seq)),
            ],
            out_specs=[
                pl.BlockSpec((PROJ_TM, NA_WIDTH), lambda i: (i, 0)),
                pl.BlockSpec((PROJ_TM, NA_WIDTH), lambda i: (i, 0)),
                pl.BlockSpec((PROJ_TM, SWA_KV_WIDTH), lambda i: (i, 0)),
                pl.BlockSpec((1, 2, blk128, NA_WIDTH, LANES), lambda i: (seq_tile(i)[0], 0, seq_tile(i)[1], 0, 0)),
                pl.BlockSpec((1, SWA_WIDTH, PROJ_TM), lambda i: (seq_tile(i)[0], 0, seq_tile(i)[1])),
                pl.BlockSpec((1, blk128, SWA_KV_WIDTH, LANES), lambda i: (seq_tile(i)[0], seq_tile(i)[1], 0, 0)),
            ],
            out_shape=[
                jax.ShapeDtypeStruct((n_tok, NA_WIDTH), bf16),
                jax.ShapeDtypeStruct((n_tok, NA_WIDTH), bf16),
                jax.ShapeDtypeStruct((n_tok, SWA_KV_WIDTH), bf16),
                jax.ShapeDtypeStruct((B, 2, S // LANES, NA_WIDTH, LANES), bf16),
                jax.ShapeDtypeStruct((B, SWA_WIDTH, S), bf16),
                jax.ShapeDtypeStruct((B, S // LANES, SWA_KV_WIDTH, LANES), bf16),
            ],
            scratch_shapes=[pltpu.VMEM((D, TOK_WIDTH), bf16), pltpu.VMEM((FEAT_ROWS, D), bf16),
                            pltpu.VMEM((NA_WIDTH, LANES), jnp.float32)],
            compiler_params=pltpu.CompilerParams(
                dimension_semantics=("arbitrary",), vmem_limit_bytes=PROJ_VMEM_LIMIT),
            name="in_proj",
        )(x2, w_in[layer], cos_tok, sin_tok, cos_feat, sin_feat)

        rel_rows = jnp.pad(rel_pos_bias[layer].astype(jnp.float32).reshape(NA_HEADS * N_DR, N_DC),
                           ((0, 0), (0, LANES - N_DC)))
        n_total = B * n_blocks
        gate_blocks = [(_W_ZA + c * GATE_BLOCK) // GATE_BLOCK for c in range(NA_WIDTH // GATE_BLOCK)] + \
                      [(_W_ZB + c * GATE_BLOCK) // GATE_BLOCK for c in range(SWA_WIDTH // GATE_BLOCK)]
        assert _W_ZA % GATE_BLOCK == 0 and _W_ZB % GATE_BLOCK == 0
        assert TICKS % 2 == 0 and n_blocks % TICKS == 0
        blk_of = lambda h: (lambda j: jnp.clip(TICKS * (j - 1) + 1 + h, 0, n_total - 1))
        k_blks = [blk_of(TICKS - 2), blk_of(TICKS - 1)]
        fin = lambda j: jnp.maximum(j - 1, 0)
        q_spec = lambda blk: pl.BlockSpec((TQ, NA_WIDTH), lambda j, *_: (blk(j), 0))
        kta_spec = lambda blk: pl.BlockSpec((1, 2, S // LANES, NA_WIDTH, LANES),
                                            lambda j, *_: (blk(j) // n_blocks, 0, 0, 0, 0))
        qbt_spec = lambda blk: pl.BlockSpec((1, SWA_WIDTH, TQ),
                                            lambda j, *_: (blk(j) // n_blocks, 0, blk(j) % n_blocks))
        kb_spec = lambda blk: pl.BlockSpec((1, S, SWA_KV_WIDTH), lambda j, *_: (blk(j) // n_blocks, 0, 0))
        fin_batch = lambda j: (TICKS * fin(j)) // n_blocks
        grid_spec = pltpu.PrefetchScalarGridSpec(
            num_scalar_prefetch=1,
            grid=(n_total // TICKS + 1,),
            in_specs=[
                *[q_spec(blk_of(h)) for h in range(TICKS)],
                *[kta_spec(blk) for blk in k_blks],
                pl.BlockSpec((1, S, NA_WIDTH), lambda j, *_: (fin_batch(j), 0, 0)),
                *[qbt_spec(blk_of(h)) for h in range(TICKS)],
                *[kb_spec(blk) for blk in k_blks],
                pl.BlockSpec((1, S // LANES, SWA_KV_WIDTH, LANES), lambda j, *_: (fin_batch(j), 0, 0, 0)),
                pl.BlockSpec((TICKS * TQ, D), lambda j, *_: (fin(j), 0)),
                pl.BlockSpec((MIX_WIDTH, D), lambda j, *_: (0, 0), pipeline_mode=pl.Buffered(1)),
                *[pl.BlockSpec((D, GATE_BLOCK), functools.partial(lambda c, j, *_: (0, c), c),
                               pipeline_mode=pl.Buffered(1)) for c in gate_blocks],
                pl.BlockSpec((NA_HEADS * N_DR, LANES), lambda j, *_: (0, 0)),
                pl.BlockSpec((3, SWA_KEYS, TQ), lambda j, *_: (0, 0, 0)),
                pl.BlockSpec((1, D), lambda j, *_: (0, 0)),
                pl.BlockSpec((1, D), lambda j, *_: (0, 0)),
            ],
            out_specs=pl.BlockSpec((TICKS * TQ, D), lambda j, *_: (fin(j), 0)),
            scratch_shapes=[pltpu.VMEM((MIX_WIDTH, D), bf16),
                            pltpu.VMEM((D, MIX_WIDTH), bf16),
                            pltpu.VMEM((N_BIAS_TILES, NA_HEADS, GRID_W, 2 * GRID_W), jnp.float32),
                            pltpu.VMEM((2, Q_ROWS * NA_HEADS // 2, TQ, NA_KEYS), jnp.float32),
                            pltpu.VMEM((2, SWA_KV_HEADS, SWA_KEYS, SWA_GROUP * TQ), jnp.float32)],
        )
        x2 = pl.pallas_call(
            functools.partial(_attn_kernel, rows=rows, n_blocks=n_blocks, n_total=n_total),
            grid_spec=grid_spec,
            out_shape=jax.ShapeDtypeStruct((n_tok, D), jnp.float32),
            compiler_params=pltpu.CompilerParams(
                dimension_semantics=("arbitrary",), vmem_limit_bytes=ATTN_VMEM_LIMIT),
            name="attn_out",
        )(sink_logits[layer].astype(jnp.float32),
          *([qa] * TICKS), *([kta] * len(k_blks)), va.reshape(B, S, NA_WIDTH), *([qbt] * TICKS),
          *([kb.reshape(B, S, SWA_KV_WIDTH)] * len(k_blks)), vbt, x2,
          w_out[layer], *([w_in[layer]] * len(gate_blocks)), rel_rows, band, ln_gain[layer].reshape(1, D), ln_bias[layer].reshape(1, D))
    return x2.reshape(B, S, D)
```

```python
import functools

import numpy as np
import jax
import jax.numpy as jnp
from jax import lax
from jax.experimental import pallas as pl
from jax.experimental.pallas import tpu as pltpu

D_MODEL = 1024
HEAD_DIM = 64
NA_HEADS = 8
NA_WIDTH = NA_HEADS * HEAD_DIM
NA_KH = 8
NA_KW = 16
GRID_W = 64
SWA_Q_HEADS = 8
SWA_KV_HEADS = 2
SWA_GROUP = SWA_Q_HEADS // SWA_KV_HEADS
SWA_WIDTH = SWA_Q_HEADS * HEAD_DIM
SWA_KV_WIDTH = SWA_KV_HEADS * HEAD_DIM
SWA_WINDOW = 128
ROPE_THETA = 10000.0
MIX_WIDTH = NA_WIDTH + SWA_WIDTH
LN_EPS = 1e-5
MASK_VALUE = -1e30
DEPTH = 1
DEEPNORM_ALPHA = (2.0 * DEPTH) ** 0.25
LOG2E = 1.4426950408889634
Q_SCALE = HEAD_DIM ** -0.5 * LOG2E
N_DR = 2 * NA_KH - 1
N_DC = 2 * NA_KW - 1

LANES = 128
BF16_ROWS = 16
PAIR = 2 * HEAD_DIM
assert PAIR == LANES and 2 * GRID_W == LANES and SWA_KV_WIDTH == LANES

PROJ_TM = 1024
TQ = 128
TICKS = 4
V7X_VMEM_BYTES = 64 * 1024 * 1024
PROJ_VMEM_LIMIT = V7X_VMEM_BYTES * 7 // 8
ATTN_VMEM_LIMIT = V7X_VMEM_BYTES * 15 // 16
Q_ROWS = TQ // GRID_W
NA_KEYS = NA_KH * GRID_W
KEY_PAIRS = NA_KEYS // LANES
N_BIAS_TILES = N_DR - 1
SWA_KEYS = 3 * TQ

_QA = (0, NA_WIDTH)
_VA = (_QA[1], _QA[1] + NA_WIDTH)
_KB = (_VA[1], _VA[1] + SWA_KV_WIDTH)
TOK_WIDTH = _KB[1]
GATE_BLOCK = 256
_T_KA = (0, NA_WIDTH)
_T_QB = (_T_KA[1], _T_KA[1] + SWA_WIDTH)
_T_VB = (_T_QB[1], _T_QB[1] + SWA_KV_WIDTH)
FEAT_ROWS = _T_VB[1]
_W_QA = 0
_W_KA = _W_QA + NA_WIDTH
_W_VA = _W_KA + NA_WIDTH
_W_ZA = _W_VA + NA_WIDTH
_W_QB = _W_ZA + NA_WIDTH
_W_KB = _W_QB + SWA_WIDTH
_W_VB = _W_KB + SWA_KV_WIDTH
_W_ZB = _W_VB + SWA_KV_WIDTH
IN_WIDTH = _W_ZB + SWA_WIDTH


def _swa_band_tiles():
    q = np.arange(TQ)[None, :]
    k = np.arange(SWA_KEYS)[:, None] - TQ
    band = np.abs(k - q) <= SWA_WINDOW
    left_ok = np.concatenate([np.zeros((TQ, 1), bool), np.ones((2 * TQ, 1), bool)], axis=0)
    right_ok = np.concatenate([np.ones((2 * TQ, 1), bool), np.zeros((TQ, 1), bool)], axis=0)
    variants = np.stack([band & left_ok, band, band & right_ok])
    assert variants[:, TQ:2 * TQ].all()
    return np.where(variants, 0.0, MASK_VALUE).astype(np.float32)


def _rope_tables(seq):
    inv_freq = ROPE_THETA ** (-jnp.arange(0, HEAD_DIM, 2, dtype=jnp.float32) / HEAD_DIM)
    ang = jnp.arange(seq, dtype=jnp.int32).astype(jnp.float32)[:, None] * inv_freq[None, :]
    cos = jnp.cos(ang)
    sin = jnp.sin(ang)
    cos_head = jnp.concatenate([cos, cos], axis=1)
    sin_head = jnp.concatenate([-sin, sin], axis=1)
    cos_tok = jnp.concatenate([cos_head, cos_head], axis=1)
    sin_tok = jnp.concatenate([sin_head, sin_head], axis=1)
    return cos_tok, sin_tok, cos_head.T * Q_SCALE, sin_head.T * Q_SCALE


def _build_proj_weights(win_ref, wtok_ref, wft_ref):
    bf16 = jnp.bfloat16
    wtok_ref[:, _QA[0]:_QA[1]] = win_ref[:, _W_QA:_W_QA + NA_WIDTH].astype(bf16)
    wtok_ref[:, _VA[0]:_VA[1]] = win_ref[:, _W_VA:_W_VA + NA_WIDTH].astype(bf16)
    wtok_ref[:, _KB[0]:_KB[1]] = win_ref[:, _W_KB:_W_KB + SWA_KV_WIDTH].astype(bf16)
    for dst, src, width in ((_T_KA[0], _W_KA, NA_WIDTH), (_T_QB[0], _W_QB, SWA_WIDTH), (_T_VB[0], _W_VB, SWA_KV_WIDTH)):
        for c in range(width // LANES):
            wft_ref[dst + c * LANES:dst + (c + 1) * LANES, :] = \
                win_ref[:, src + c * LANES:src + (c + 1) * LANES].T.astype(bf16)


def _proj_kernel(x_ref, win_ref, cos_ref, sin_ref, cost_ref, sint_ref,
                 qa_ref, va_ref, kb_ref, kta_ref, qbt_ref, vbt_ref,
                 wtok_ref, wft_ref, carry_ref, *, tiles_per_seq):
    bf16 = jnp.bfloat16
    i = pl.program_id(0)
    half = HEAD_DIM // 2
    n_blk = PROJ_TM // LANES

    @pl.when(i == 0)
    def _():
        _build_proj_weights(win_ref, wtok_ref, wft_ref)

    @pl.when(i % tiles_per_seq == 0)
    def _():
        carry_ref[...] = jnp.zeros_like(carry_ref)

    xb = x_ref[...].astype(bf16)

    def proj(span):
        return jnp.dot(xb, wtok_ref[:, span[0]:span[1]], preferred_element_type=jnp.float32)

    qa_ref[...] = (proj(_QA) * Q_SCALE).astype(bf16)
    va_ref[...] = proj(_VA).astype(bf16)

    kb = proj(_KB)
    lane = lax.broadcasted_iota(jnp.int32, (PROJ_TM, LANES), 1)
    low_half = (lane % HEAD_DIM) < half
    partner = jnp.where(low_half, pltpu.roll(kb, LANES - half, 1), pltpu.roll(kb, half, 1))
    kb_ref[...] = (kb * cos_ref[...] + partner * sin_ref[...]).astype(bf16)

    ft = lax.dot_general(wft_ref[...], xb, (((1,), (1,)), ((), ())), preferred_element_type=jnp.float32)
    cols = [carry_ref[...]] + [ft[_T_KA[0]:_T_KA[1], j * LANES:(j + 1) * LANES] for j in range(n_blk)]
    low_k = lax.broadcasted_iota(jnp.int32, (NA_WIDTH, LANES), 1) < GRID_W
    rolled = [pltpu.roll(c, GRID_W, 1) for c in cols]
    for j in range(n_blk):
        kta_ref[0, 0, j] = cols[j + 1].astype(bf16)
        kta_ref[0, 1, j] = jnp.where(low_k, rolled[j], rolled[j + 1]).astype(bf16)
    carry_ref[...] = cols[n_blk]

    cos_t = cost_ref[...]
    sin_t = sint_ref[...]
    for h in range(SWA_Q_HEADS):
        blk = ft[_T_QB[0] + h * HEAD_DIM:_T_QB[0] + (h + 1) * HEAD_DIM]
        partner_t = jnp.concatenate([blk[half:], blk[:half]], axis=0)
        qbt_ref[0, h * HEAD_DIM:(h + 1) * HEAD_DIM, :] = (blk * cos_t + partner_t * sin_t).astype(bf16)

    vt = ft[_T_VB[0]:_T_VB[1]].astype(bf16)
    for j in range(n_blk):
        vbt_ref[0, j] = vt[:, j * LANES:(j + 1) * LANES]


def _build_na_bias_tiles(rel_ref, tbl_ref):
    c = lax.broadcasted_iota(jnp.int32, (GRID_W, LANES), 0)
    lane = lax.broadcasted_iota(jnp.int32, (GRID_W, LANES), 1)
    kc = lane & (GRID_W - 1)
    start = jnp.clip(c - NA_KW // 2, 0, GRID_W - NA_KW)
    in_win = (kc >= start) & (kc < start + NA_KW)
    low = lane < GRID_W
    masked = jnp.full((GRID_W, LANES), MASK_VALUE, jnp.float32)
    for h in range(NA_HEADS):
        def half(dr, upper):
            row = jnp.broadcast_to(rel_ref[pl.ds(h * N_DR + dr, 1), :] * LOG2E, (GRID_W, LANES))
            shift = (LANES - (NA_KW - 1) + (GRID_W if upper else 0)) % LANES
            return pltpu.roll(row, shift, 1, stride=1, stride_axis=0)

        lower = [half(dr, False) for dr in range(N_DR - 1)]
        upper = [half(dr, True) for dr in range(1, N_DR)]
        for d in range(N_BIAS_TILES):
            tbl_ref[d, h] = jnp.where(in_win, jnp.where(low, lower[d], upper[d]), masked)


def _attn_kernel(sink_ref, *refs, rows, n_blocks, n_total):
    refs = list(refs)
    take = lambda n: [refs.pop(0) for _ in range(n)]
    qa_refs = take(TICKS)
    kta_lo, kta_hi, va_ref = take(3)
    qbt_refs = take(TICKS)
    kb_lo, kb_hi, vbt_ref, x_ref, wout_ref = take(5)
    wz_refs = take(MIX_WIDTH // GATE_BLOCK)
    rel_ref, band_ref, gain_ref, beta_ref, out_ref, woutb_ref, wzb_ref, tbl_ref, sna_ref, ssw_ref = refs
    j = pl.program_id(0)

    @pl.when(j == 0)
    def _():
        woutb_ref[...] = wout_ref[...].astype(jnp.bfloat16)
        for c, wz_ref in enumerate(wz_refs):
            wzb_ref[:, c * GATE_BLOCK:(c + 1) * GATE_BLOCK] = wz_ref[...].astype(jnp.bfloat16)
        _build_na_bias_tiles(rel_ref, tbl_ref)
        sna_ref[0] = jnp.zeros(sna_ref.shape[1:], sna_ref.dtype)
        ssw_ref[0] = jnp.zeros(ssw_ref.shape[1:], ssw_ref.dtype)

    n_units = Q_ROWS * NA_HEADS // 2
    for h in range(TICKS):
        i = TICKS * (j - 1) + 1 + h
        last = h == TICKS - 1
        rd, wr = h % 2, 1 - h % 2
        rows_h = pl.ds(h * TQ, TQ)
        ena, esw, snk = [None] * n_units, [None] * SWA_KV_HEADS, [None] * SWA_KV_HEADS
        sm_na, sm_sw = _softmax_steps(sink_ref, sna_ref.at[rd], ssw_ref.at[rd], ena, esw, snk)
        out_na, out_sw, out_tail = _output_steps(
            jnp.maximum(i - 1, 0) % n_blocks, va_ref, vbt_ref, wzb_ref, woutb_ref, x_ref.at[rows_h],
            gain_ref, beta_ref, out_ref.at[rows_h], ena, esw, snk, rows=rows, n_blocks=n_blocks)
        sc_na, sc_sw = _score_steps(jnp.clip(i, 0, n_total - 1) % n_blocks, qa_refs[h], kta_hi if last else kta_lo,
                                    qbt_refs[h], kb_hi if last else kb_lo, band_ref, tbl_ref,
                                    sna_ref.at[wr], ssw_ref.at[wr], rows=rows, n_blocks=n_blocks)
        _emit(sm_na, sm_sw, out_na, out_sw, out_tail, sc_na, sc_sw)


def _emit(sm_na, sm_sw, out_na, out_sw, out_tail, sc_na, sc_sw):
    for step in sm_na + sm_sw + out_na + out_sw:
        step()
    out_tail()
    for step in sc_na + sc_sw:
        step()


def _window_start(t, qrow, rows):
    kh = min(NA_KH, rows)
    r = Q_ROWS * t + qrow
    return r, jnp.clip(r - kh // 2, 0, rows - kh)


def _softmax_steps(sink_ref, sna_rd, ssw_rd, ena_wr, esw_wr, snk_wr):
    f32, bf16 = jnp.float32, jnp.bfloat16

    def na(u):
        s_ref = sna_rd.at[u]
        m = jnp.max(s_ref[...], axis=-1, keepdims=True)
        ena_wr[u] = jnp.exp2(s_ref[...] - m).astype(bf16)

    def sw(g):
        s_ref = ssw_rd.at[g]
        sink = jnp.concatenate(
            [jnp.full((1, TQ), sink_ref[SWA_GROUP * g + j] * LOG2E, f32) for j in range(SWA_GROUP)], axis=1)
        m = jnp.maximum(jnp.max(s_ref[...], axis=0, keepdims=True), sink)
        esw_wr[g] = jnp.exp2(s_ref[...] - m).astype(bf16)
        snk_wr[g] = jnp.exp2(sink - m)

    return ([functools.partial(na, u) for u in range(Q_ROWS * NA_HEADS // 2)],
            [functools.partial(sw, g) for g in range(SWA_KV_HEADS)])


def _output_steps(t, va_ref, vbt_ref, wzb_ref, woutb_ref, x_ref, gain_ref, beta_ref, out_ref,
                  ena_rd, esw_rd, snk_rd, *, rows, n_blocks):
    f32, bf16 = jnp.float32, jnp.bfloat16
    n_pairs = NA_HEADS // 2
    low = lax.broadcasted_iota(jnp.int32, (GRID_W, LANES), 1) < HEAD_DIM
    jl = jnp.maximum(t - 1, 0)
    jr = jnp.minimum(t + 1, n_blocks - 1)
    na_out = [None] * (Q_ROWS * n_pairs)
    sw_cols = [None] * (SWA_Q_HEADS // 2)

    def na(u):
        qrow, p = divmod(u, n_pairs)
        _, r0 = _window_start(t, qrow, rows)
        key0 = pl.multiple_of(r0 * GRID_W, GRID_W)
        v = va_ref[0, pl.ds(key0, NA_KEYS), p * LANES:(p + 1) * LANES]
        v1 = jnp.concatenate([v, jnp.ones_like(v)], axis=1)
        o = jnp.dot(ena_rd[u], v1, preferred_element_type=f32)
        o = o[:, :LANES] * (1.0 / o[:, LANES:])
        na_out[u] = jnp.where(low, o[:GRID_W], o[GRID_W:])

    def sw(g):
        vt = jnp.concatenate([vbt_ref[0, j, g * HEAD_DIM:(g + 1) * HEAD_DIM, :] for j in (jl, t, jr)], axis=1)
        v1 = jnp.concatenate([vt, jnp.ones((BF16_ROWS, SWA_KEYS), bf16)], axis=0)
        o = jnp.dot(v1, esw_rd[g], preferred_element_type=f32)
        o = o[:HEAD_DIM] * (1.0 / (o[HEAD_DIM:HEAD_DIM + 1] + snk_rd[g]))
        for pp in range(SWA_GROUP // 2):
            both = jnp.concatenate([o[:, (2 * pp) * TQ:(2 * pp + 1) * TQ],
                                    o[:, (2 * pp + 1) * TQ:(2 * pp + 2) * TQ]], axis=0)
            sw_cols[(SWA_GROUP // 2) * g + pp] = both.T

    def tail():
        mixed_cols = [jnp.concatenate([na_out[qrow * n_pairs + p] for qrow in range(Q_ROWS)], axis=0)
                      for p in range(n_pairs)]
        mixed = jnp.concatenate(mixed_cols + sw_cols, axis=1)
        x_blk = x_ref[...]
        z = jnp.dot(x_blk.astype(bf16), wzb_ref[...], preferred_element_type=f32)
        mixed = (mixed * (z * jax.nn.sigmoid(z))).astype(bf16)
        y = jnp.dot(mixed, woutb_ref[...], preferred_element_type=f32)
        r = DEEPNORM_ALPHA * x_blk + y
        mu = jnp.mean(r, axis=-1, keepdims=True)
        d = r - mu
        var = jnp.mean(d * d, axis=-1, keepdims=True)
        out_ref[...] = d * lax.rsqrt(var + LN_EPS) * gain_ref[...] + beta_ref[...]

    return ([functools.partial(na, u) for u in range(Q_ROWS * n_pairs)],
            [functools.partial(sw, g) for g in range(SWA_KV_HEADS)], tail)


def _score_steps(t1, qa_ref, kta_ref, qbt_ref, kb_ref, band_ref, tbl_ref, sna_wr, ssw_wr, *, rows, n_blocks):
    f32 = jnp.float32
    n_pairs = NA_HEADS // 2
    low = lax.broadcasted_iota(jnp.int32, (GRID_W, LANES), 1) < HEAD_DIM

    def na(u):
        qrow, p = divmod(u, n_pairs)
        r, r0 = _window_start(t1, qrow, rows)
        par = r0 & 1
        d0 = r0 - r + (NA_KH - 1)
        cols = slice(p * LANES, (p + 1) * LANES)
        qp = qa_ref[qrow * GRID_W:(qrow + 1) * GRID_W, cols]
        zero = jnp.zeros_like(qp)
        q2 = jnp.concatenate([jnp.where(low, qp, zero), jnp.where(low, zero, qp)], axis=0)
        kt = jnp.concatenate([kta_ref[0, par, (r0 + 2 * j + 1) // 2, cols, :] for j in range(KEY_PAIRS)],
                             axis=1)
        bias = jnp.concatenate(
            [jnp.concatenate([tbl_ref[d0 + 2 * j, 2 * p + hp] for j in range(KEY_PAIRS)], axis=1)
             for hp in range(2)], axis=0)
        sna_wr[u] = jnp.dot(q2, kt, preferred_element_type=f32) + bias

    def sw(g):
        jl1 = jnp.maximum(t1 - 1, 0)
        jr1 = jnp.minimum(t1 + 1, n_blocks - 1)
        variant = jnp.where(t1 == 0, 0, jnp.where(t1 == n_blocks - 1, 2, 1))
        k3 = jnp.concatenate([kb_ref[0, pl.ds(pl.multiple_of(j * TQ, TQ), TQ), :] for j in (jl1, t1, jr1)],
                             axis=0)
        qt = jnp.concatenate(
            [qbt_ref[0, (SWA_GROUP * g + j) * HEAD_DIM:(SWA_GROUP * g + j + 1) * HEAD_DIM, :]
             for j in range(SWA_GROUP)], axis=1)
        parts = [jnp.zeros_like(qt)] * SWA_KV_HEADS
        parts[g] = qt
        s = jnp.dot(k3, jnp.concatenate(parts, axis=0), preferred_element_type=f32)
        for blk in (0, 2):
            band = band_ref[variant, blk * TQ:(blk + 1) * TQ, :]
            ssw_wr[g, blk * TQ:(blk + 1) * TQ, :] = (s[blk * TQ:(blk + 1) * TQ]
                                                      + jnp.concatenate([band] * SWA_GROUP, axis=1))
        ssw_wr[g, TQ:2 * TQ, :] = s[TQ:2 * TQ]

    return ([functools.partial(na, u) for u in range(Q_ROWS * n_pairs)],
            [functools.partial(sw, g) for g in range(SWA_KV_HEADS)])


def kernel(x, w_in, rel_pos_bias, sink_logits, w_out, ln_gain, ln_bias):
    B, S, D = x.shape
    assert D == D_MODEL and S % PROJ_TM == 0 and S % GRID_W == 0
    rows = S // GRID_W
    assert rows >= NA_KH and rows % Q_ROWS == 0
    n_blocks = S // TQ
    n_tok = B * S
    bf16 = jnp.bfloat16
    x2 = x.reshape(n_tok, D)

    cos_tok, sin_tok, cos_feat, sin_feat = _rope_tables(S)
    band = jnp.asarray(_swa_band_tiles())

    for layer in range(DEPTH):
        tiles_per_seq = S // PROJ_TM
        blk128 = PROJ_TM // LANES
        seq_tile = lambda i: (i // tiles_per_seq, i % tiles_per_seq)
        qa, va, kb, kta, qbt, vbt = pl.pallas_call(
            functools.partial(_proj_kernel, tiles_per_seq=tiles_per_seq),
            grid=(n_tok // PROJ_TM,),
            in_specs=[
                pl.BlockSpec((PROJ_TM, D), lambda i: (i, 0)),
                pl.BlockSpec((D, IN_WIDTH), lambda i: (0, 0), pipeline_mode=pl.Buffered(1)),
                pl.BlockSpec((PROJ_TM, LANES), lambda i: (i % tiles_per_seq, 0)),
                pl.BlockSpec((PROJ_TM, LANES), lambda i: (i % tiles_per_seq, 0)),
                pl.BlockSpec((HEAD_DIM, PROJ_TM), lambda i: (0, i % tiles_per_seq)),
                pl.BlockSpec((HEAD_DIM, PROJ_TM), lambda i: (0, i % tiles_per_seq)),
            ],
            out_specs=[
                pl.BlockSpec((PROJ_TM, NA_WIDTH), lambda i: (i, 0)),
                pl.BlockSpec((PROJ_TM, NA_WIDTH), lambda i: (i, 0)),
                pl.BlockSpec((PROJ_TM, SWA_KV_WIDTH), lambda i: (i, 0)),
                pl.BlockSpec((1, 2, blk128, NA_WIDTH, LANES), lambda i: (seq_tile(i)[0], 0, seq_tile(i)[1], 0, 0)),
                pl.BlockSpec((1, SWA_WIDTH, PROJ_TM), lambda i: (seq_tile(i)[0], 0, seq_tile(i)[1])),
                pl.BlockSpec((1, blk128, SWA_KV_WIDTH, LANES), lambda i: (seq_tile(i)[0], seq_tile(i)[1], 0, 0)),
            ],
            out_shape=[
                jax.ShapeDtypeStruct((n_tok, NA_WIDTH), bf16),
                jax.ShapeDtypeStruct((n_tok, NA_WIDTH), bf16),
                jax.ShapeDtypeStruct((n_tok, SWA_KV_WIDTH), bf16),
                jax.ShapeDtypeStruct((B, 2, S // LANES, NA_WIDTH, LANES), bf16),
                jax.ShapeDtypeStruct((B, SWA_WIDTH, S), bf16),
                jax.ShapeDtypeStruct((B, S // LANES, SWA_KV_WIDTH, LANES), bf16),
            ],
            scratch_shapes=[pltpu.VMEM((D, TOK_WIDTH), bf16), pltpu.VMEM((FEAT_ROWS, D), bf16),
                            pltpu.VMEM((NA_WIDTH, LANES), jnp.float32)],
            compiler_params=pltpu.CompilerParams(
                dimension_semantics=("arbitrary",), vmem_limit_bytes=PROJ_VMEM_LIMIT),
            name="in_proj",
        )(x2, w_in[layer], cos_tok, sin_tok, cos_feat, sin_feat)

        rel_rows = jnp.pad(rel_pos_bias[layer].astype(jnp.float32).reshape(NA_HEADS * N_DR, N_DC),
                           ((0, 0), (0, LANES - N_DC)))
        n_total = B * n_blocks
        gate_blocks = [(_W_ZA + c * GATE_BLOCK) // GATE_BLOCK for c in range(NA_WIDTH // GATE_BLOCK)] + \
                      [(_W_ZB + c * GATE_BLOCK) // GATE_BLOCK for c in range(SWA_WIDTH // GATE_BLOCK)]
        assert _W_ZA % GATE_BLOCK == 0 and _W_ZB % GATE_BLOCK == 0
        assert TICKS % 2 == 0 and n_blocks % TICKS == 0
        blk_of = lambda h: (lambda j: jnp.clip(TICKS * (j - 1) + 1 + h, 0, n_total - 1))
        k_blks = [blk_of(TICKS - 2), blk_of(TICKS - 1)]
        fin = lambda j: jnp.maximum(j - 1, 0)
        q_spec = lambda blk: pl.BlockSpec((TQ, NA_WIDTH), lambda j, *_: (blk(j), 0))
        kta_spec = lambda blk: pl.BlockSpec((1, 2, S // LANES, NA_WIDTH, LANES),
                                            lambda j, *_: (blk(j) // n_blocks, 0, 0, 0, 0))
        qbt_spec = lambda blk: pl.BlockSpec((1, SWA_WIDTH, TQ),
                                            lambda j, *_: (blk(j) // n_blocks, 0, blk(j) % n_blocks))
        kb_spec = lambda blk: pl.BlockSpec((1, S, SWA_KV_WIDTH), lambda j, *_: (blk(j) // n_blocks, 0, 0))
        fin_batch = lambda j: (TICKS * fin(j)) // n_blocks
        grid_spec = pltpu.PrefetchScalarGridSpec(
            num_scalar_prefetch=1,
            grid=(n_total // TICKS + 1,),
            in_specs=[
                *[q_spec(blk_of(h)) for h in range(TICKS)],
                *[kta_spec(blk) for blk in k_blks],
                pl.BlockSpec((1, S, NA_WIDTH), lambda j, *_: (fin_batch(j), 0, 0)),
                *[qbt_spec(blk_of(h)) for h in range(TICKS)],
                *[kb_spec(blk) for blk in k_blks],
                pl.BlockSpec((1, S // LANES, SWA_KV_WIDTH, LANES), lambda j, *_: (fin_batch(j), 0, 0, 0)),
                pl.BlockSpec((TICKS * TQ, D), lambda j, *_: (fin(j), 0)),
                pl.BlockSpec((MIX_WIDTH, D), lambda j, *_: (0, 0), pipeline_mode=pl.Buffered(1)),
                *[pl.BlockSpec((D, GATE_BLOCK), functools.partial(lambda c, j, *_: (0, c), c),
                               pipeline_mode=pl.Buffered(1)) for c in gate_blocks],
                pl.BlockSpec((NA_HEADS * N_DR, LANES), lambda j, *_: (0, 0)),
                pl.BlockSpec((3, SWA_KEYS, TQ), lambda j, *_: (0, 0, 0)),
                pl.BlockSpec((1, D), lambda j, *_: (0, 0)),
                pl.BlockSpec((1, D), lambda j, *_: (0, 0)),
            ],
            out_specs=pl.BlockSpec((TICKS * TQ, D), lambda j, *_: (fin(j), 0)),
            scratch_shapes=[pltpu.VMEM((MIX_WIDTH, D), bf16),
                            pltpu.VMEM((D, MIX_WIDTH), bf16),
                            pltpu.VMEM((N_BIAS_TILES, NA_HEADS, GRID_W, 2 * GRID_W), jnp.float32),
                            pltpu.VMEM((2, Q_ROWS * NA_HEADS // 2, TQ, NA_KEYS), jnp.float32),
                            pltpu.VMEM((2, SWA_KV_HEADS, SWA_KEYS, SWA_GROUP * TQ), jnp.float32)],
        )
        x2 = pl.pallas_call(
            functools.partial(_attn_kernel, rows=rows, n_blocks=n_blocks, n_total=n_total),
            grid_spec=grid_spec,
            out_shape=jax.ShapeDtypeStruct((n_tok, D), jnp.float32),
            compiler_params=pltpu.CompilerParams(
                dimension_semantics=("arbitrary",), vmem_limit_bytes=ATTN_VMEM_LIMIT),
            name="attn_out",
        )(sink_logits[layer].astype(jnp.float32),
          *([qa] * TICKS), *([kta] * len(k_blks)), va.reshape(B, S, NA_WIDTH), *([qbt] * TICKS),
          *([kb.reshape(B, S, SWA_KV_WIDTH)] * len(k_blks)), vbt, x2,
          w_out[layer], *([w_in[layer]] * len(gate_blocks)), rel_rows, band, ln_gain[layer].reshape(1, D), ln_bias[layer].reshape(1, D))
    return x2.reshape(B, S, D)
```

```python
import functools

import numpy as np
import jax
import jax.numpy as jnp
from jax import lax
from jax.experimental import pallas as pl
from jax.experimental.pallas import tpu as pltpu

D_MODEL = 1024
HEAD_DIM = 64
NA_HEADS = 8
NA_WIDTH = NA_HEADS * HEAD_DIM
NA_KH = 8
NA_KW = 16
GRID_W = 64
SWA_Q_HEADS = 8
SWA_KV_HEADS = 2
SWA_GROUP = SWA_Q_HEADS // SWA_KV_HEADS
SWA_WIDTH = SWA_Q_HEADS * HEAD_DIM
SWA_KV_WIDTH = SWA_KV_HEADS * HEAD_DIM
SWA_WINDOW = 128
ROPE_THETA = 10000.0
MIX_WIDTH = NA_WIDTH + SWA_WIDTH
LN_EPS = 1e-5
MASK_VALUE = -1e30
DEPTH = 1
DEEPNORM_ALPHA = (2.0 * DEPTH) ** 0.25
LOG2E = 1.4426950408889634
Q_SCALE = HEAD_DIM ** -0.5 * LOG2E
N_DR = 2 * NA_KH - 1
N_DC = 2 * NA_KW - 1

LANES = 128
BF16_ROWS = 16
PAIR = 2 * HEAD_DIM
assert PAIR == LANES and 2 * GRID_W == LANES and SWA_KV_WIDTH == LANES

PROJ_TM = 1024
TQ = 128
TICKS = 4
V7X_VMEM_BYTES = 64 * 1024 * 1024
PROJ_VMEM_LIMIT = V7X_VMEM_BYTES * 7 // 8
ATTN_VMEM_LIMIT = V7X_VMEM_BYTES * 31 // 32
Q_ROWS = TQ // GRID_W
NA_KEYS = NA_KH * GRID_W
KEY_PAIRS = NA_KEYS // LANES
N_BIAS_TILES = N_DR - 1
SWA_KEYS = 3 * TQ

_QA = (0, NA_WIDTH)
_VA = (_QA[1], _QA[1] + NA_WIDTH)
_KB = (_VA[1], _VA[1] + SWA_KV_WIDTH)
TOK_WIDTH = _KB[1]
GATE_BLOCK = 256
_T_KA = (0, NA_WIDTH)
_T_QB = (_T_KA[1], _T_KA[1] + SWA_WIDTH)
_T_VB = (_T_QB[1], _T_QB[1] + SWA_KV_WIDTH)
FEAT_ROWS = _T_VB[1]
_W_QA = 0
_W_KA = _W_QA + NA_WIDTH
_W_VA = _W_KA + NA_WIDTH
_W_ZA = _W_VA + NA_WIDTH
_W_QB = _W_ZA + NA_WIDTH
_W_KB = _W_QB + SWA_WIDTH
_W_VB = _W_KB + SWA_KV_WIDTH
_W_ZB = _W_VB + SWA_KV_WIDTH
IN_WIDTH = _W_ZB + SWA_WIDTH


def _swa_band_tiles():
    q = np.arange(TQ)[None, :]
    k = np.arange(SWA_KEYS)[:, None] - TQ
    band = np.abs(k - q) <= SWA_WINDOW
    left_ok = np.concatenate([np.zeros((TQ, 1), bool), np.ones((2 * TQ, 1), bool)], axis=0)
    right_ok = np.concatenate([np.ones((2 * TQ, 1), bool), np.zeros((TQ, 1), bool)], axis=0)
    variants = np.stack([band & left_ok, band, band & right_ok])
    assert variants[:, TQ:2 * TQ].all()
    return np.where(variants, 0.0, MASK_VALUE).astype(np.float32)


def _rope_tables(seq):
    inv_freq = ROPE_THETA ** (-jnp.arange(0, HEAD_DIM, 2, dtype=jnp.float32) / HEAD_DIM)
    ang = jnp.arange(seq, dtype=jnp.int32).astype(jnp.float32)[:, None] * inv_freq[None, :]
    cos = jnp.cos(ang)
    sin = jnp.sin(ang)
    cos_head = jnp.concatenate([cos, cos], axis=1)
    sin_head = jnp.concatenate([-sin, sin], axis=1)
    cos_tok = jnp.concatenate([cos_head, cos_head], axis=1)
    sin_tok = jnp.concatenate([sin_head, sin_head], axis=1)
    return cos_tok, sin_tok, cos_head.T * Q_SCALE, sin_head.T * Q_SCALE


def _build_proj_weights(win_ref, wtok_ref, wft_ref):
    bf16 = jnp.bfloat16
    wtok_ref[:, _QA[0]:_QA[1]] = win_ref[:, _W_QA:_W_QA + NA_WIDTH].astype(bf16)
    wtok_ref[:, _VA[0]:_VA[1]] = win_ref[:, _W_VA:_W_VA + NA_WIDTH].astype(bf16)
    wtok_ref[:, _KB[0]:_KB[1]] = win_ref[:, _W_KB:_W_KB + SWA_KV_WIDTH].astype(bf16)
    for dst, src, width in ((_T_KA[0], _W_KA, NA_WIDTH), (_T_QB[0], _W_QB, SWA_WIDTH), (_T_VB[0], _W_VB, SWA_KV_WIDTH)):
        for c in range(width // LANES):
            wft_ref[dst + c * LANES:dst + (c + 1) * LANES, :] = \
                win_ref[:, src + c * LANES:src + (c + 1) * LANES].T.astype(bf16)


def _proj_kernel(x_ref, win_ref, cos_ref, sin_ref, cost_ref, sint_ref,
                 qa_ref, va_ref, kb_ref, kta_ref, qbt_ref, vbt_ref,
                 wtok_ref, wft_ref, carry_ref, *, tiles_per_seq):
    bf16 = jnp.bfloat16
    i = pl.program_id(0)
    half = HEAD_DIM // 2
    n_blk = PROJ_TM // LANES

    @pl.when(i == 0)
    def _():
        _build_proj_weights(win_ref, wtok_ref, wft_ref)

    @pl.when(i % tiles_per_seq == 0)
    def _():
        carry_ref[...] = jnp.zeros_like(carry_ref)

    xb = x_ref[...].astype(bf16)

    def proj(span):
        return jnp.dot(xb, wtok_ref[:, span[0]:span[1]], preferred_element_type=jnp.float32)

    qa_ref[...] = (proj(_QA) * Q_SCALE).astype(bf16)
    va_ref[...] = proj(_VA).astype(bf16)

    kb = proj(_KB)
    lane = lax.broadcasted_iota(jnp.int32, (PROJ_TM, LANES), 1)
    low_half = (lane % HEAD_DIM) < half
    partner = jnp.where(low_half, pltpu.roll(kb, LANES - half, 1), pltpu.roll(kb, half, 1))
    kb_ref[...] = (kb * cos_ref[...] + partner * sin_ref[...]).astype(bf16)

    ft = lax.dot_general(wft_ref[...], xb, (((1,), (1,)), ((), ())), preferred_element_type=jnp.float32)
    cols = [carry_ref[...]] + [ft[_T_KA[0]:_T_KA[1], j * LANES:(j + 1) * LANES] for j in range(n_blk)]
    low_k = lax.broadcasted_iota(jnp.int32, (NA_WIDTH, LANES), 1) < GRID_W
    rolled = [pltpu.roll(c, GRID_W, 1) for c in cols]
    for j in range(n_blk):
        kta_ref[0, 0, j] = cols[j + 1].astype(bf16)
        kta_ref[0, 1, j] = jnp.where(low_k, rolled[j], rolled[j + 1]).astype(bf16)
    carry_ref[...] = cols[n_blk]

    cos_t = cost_ref[...]
    sin_t = sint_ref[...]
    for h in range(SWA_Q_HEADS):
        blk = ft[_T_QB[0] + h * HEAD_DIM:_T_QB[0] + (h + 1) * HEAD_DIM]
        partner_t = jnp.concatenate([blk[half:], blk[:half]], axis=0)
        qbt_ref[0, h * HEAD_DIM:(h + 1) * HEAD_DIM, :] = (blk * cos_t + partner_t * sin_t).astype(bf16)

    vt = ft[_T_VB[0]:_T_VB[1]].astype(bf16)
    for j in range(n_blk):
        vbt_ref[0, j] = vt[:, j * LANES:(j + 1) * LANES]


def _build_na_bias_tiles(rel_ref, tbl_ref):
    c = lax.broadcasted_iota(jnp.int32, (GRID_W, LANES), 0)
    lane = lax.broadcasted_iota(jnp.int32, (GRID_W, LANES), 1)
    kc = lane & (GRID_W - 1)
    start = jnp.clip(c - NA_KW // 2, 0, GRID_W - NA_KW)
    in_win = (kc >= start) & (kc < start + NA_KW)
    low = lane < GRID_W
    masked = jnp.full((GRID_W, LANES), MASK_VALUE, jnp.float32)
    for h in range(NA_HEADS):
        def half(dr, upper):
            row = jnp.broadcast_to(rel_ref[pl.ds(h * N_DR + dr, 1), :] * LOG2E, (GRID_W, LANES))
            shift = (LANES - (NA_KW - 1) + (GRID_W if upper else 0)) % LANES
            return pltpu.roll(row, shift, 1, stride=1, stride_axis=0)

        lower = [half(dr, False) for dr in range(N_DR - 1)]
        upper = [half(dr, True) for dr in range(1, N_DR)]
        for d in range(N_BIAS_TILES):
            tbl_ref[d, h] = jnp.where(in_win, jnp.where(low, lower[d], upper[d]), masked)


def _attn_kernel(sink_ref, *refs, rows, n_blocks, n_total):
    refs = list(refs)
    take = lambda n: [refs.pop(0) for _ in range(n)]
    qa_refs = take(TICKS)
    kta_lo, kta_hi, va_ref = take(3)
    qbt_refs = take(TICKS)
    kb_lo, kb_hi, vbt_ref, x_ref, wout_ref = take(5)
    wz_refs = take(MIX_WIDTH // GATE_BLOCK)
    (rel_ref, band_ref, gain_ref, beta_ref, out_ref,
     woutb_ref, wzb_ref, tbl_ref, sna_ref, ssw_ref, mna_ref, msw_ref) = refs
    j = pl.program_id(0)

    @pl.when(j == 0)
    def _():
        woutb_ref[...] = wout_ref[...].astype(jnp.bfloat16)
        for c, wz_ref in enumerate(wz_refs):
            wzb_ref[:, c * GATE_BLOCK:(c + 1) * GATE_BLOCK] = wz_ref[...].astype(jnp.bfloat16)
        _build_na_bias_tiles(rel_ref, tbl_ref)
        for ref in (sna_ref, ssw_ref, mna_ref, msw_ref):
            ref[0] = jnp.zeros(ref.shape[1:], ref.dtype)

    n_units = Q_ROWS * NA_HEADS // 2
    for h in range(TICKS):
        i = TICKS * (j - 1) + 1 + h
        last = h == TICKS - 1
        rd, wr = h % 2, 1 - h % 2
        rows_h = pl.ds(h * TQ, TQ)
        ena, esw, snk = [None] * n_units, [None] * SWA_KV_HEADS, [None] * SWA_KV_HEADS
        sm_na, sm_sw = _softmax_steps(sink_ref, sna_ref.at[rd], ssw_ref.at[rd], mna_ref.at[rd], msw_ref.at[rd],
                                      ena, esw, snk)
        out_na, out_sw, out_tail = _output_steps(
            jnp.maximum(i - 1, 0) % n_blocks, va_ref, vbt_ref, wzb_ref, woutb_ref, x_ref.at[rows_h],
            gain_ref, beta_ref, out_ref.at[rows_h], ena, esw, snk, rows=rows, n_blocks=n_blocks)
        sc_na, sc_sw = _score_steps(jnp.clip(i, 0, n_total - 1) % n_blocks, qa_refs[h], kta_hi if last else kta_lo,
                                    qbt_refs[h], kb_hi if last else kb_lo, band_ref, tbl_ref,
                                    sna_ref.at[wr], ssw_ref.at[wr], mna_ref.at[wr], msw_ref.at[wr],
                                    rows=rows, n_blocks=n_blocks)
        _emit(sm_na, sm_sw, out_na, out_sw, out_tail, sc_na, sc_sw)


def _emit(sm_na, sm_sw, out_na, out_sw, out_tail, sc_na, sc_sw):
    for step in sm_na + sm_sw + out_na + out_sw:
        step()
    out_tail()
    for step in sc_na + sc_sw:
        step()


def _window_start(t, qrow, rows):
    kh = min(NA_KH, rows)
    r = Q_ROWS * t + qrow
    return r, jnp.clip(r - kh // 2, 0, rows - kh)


def _softmax_steps(sink_ref, sna_rd, ssw_rd, mna_rd, msw_rd, ena_wr, esw_wr, snk_wr):
    f32, bf16 = jnp.float32, jnp.bfloat16

    def na(u):
        s_ref = sna_rd.at[u]
        m = jnp.max(mna_rd[u], axis=-1, keepdims=True)
        ena_wr[u] = jnp.exp2(s_ref[...] - m).astype(bf16)

    def sw(g):
        s_ref = ssw_rd.at[g]
        sink = jnp.concatenate(
            [jnp.full((1, TQ), sink_ref[SWA_GROUP * g + j] * LOG2E, f32) for j in range(SWA_GROUP)], axis=1)
        m = jnp.maximum(jnp.max(msw_rd[g], axis=0, keepdims=True), sink)
        esw_wr[g] = jnp.exp2(s_ref[...] - m).astype(bf16)
        snk_wr[g] = jnp.exp2(sink - m)

    return ([functools.partial(na, u) for u in range(Q_ROWS * NA_HEADS // 2)],
            [functools.partial(sw, g) for g in range(SWA_KV_HEADS)])


def _output_steps(t, va_ref, vbt_ref, wzb_ref, woutb_ref, x_ref, gain_ref, beta_ref, out_ref,
                  ena_rd, esw_rd, snk_rd, *, rows, n_blocks):
    f32, bf16 = jnp.float32, jnp.bfloat16
    n_pairs = NA_HEADS // 2
    low = lax.broadcasted_iota(jnp.int32, (GRID_W, LANES), 1) < HEAD_DIM
    jl = jnp.maximum(t - 1, 0)
    jr = jnp.minimum(t + 1, n_blocks - 1)
    na_out = [None] * (Q_ROWS * n_pairs)
    sw_cols = [None] * (SWA_Q_HEADS // 2)

    def na(u):
        qrow, p = divmod(u, n_pairs)
        _, r0 = _window_start(t, qrow, rows)
        key0 = pl.multiple_of(r0 * GRID_W, GRID_W)
        v = va_ref[0, pl.ds(key0, NA_KEYS), p * LANES:(p + 1) * LANES]
        v1 = jnp.concatenate([v, jnp.ones_like(v)], axis=1)
        o = jnp.dot(ena_rd[u], v1, preferred_element_type=f32)
        o = o[:, :LANES] * (1.0 / o[:, LANES:])
        na_out[u] = jnp.where(low, o[:GRID_W], o[GRID_W:])

    def sw(g):
        vt = jnp.concatenate([vbt_ref[0, j, g * HEAD_DIM:(g + 1) * HEAD_DIM, :] for j in (jl, t, jr)], axis=1)
        v1 = jnp.concatenate([vt, jnp.ones((BF16_ROWS, SWA_KEYS), bf16)], axis=0)
        o = jnp.dot(v1, esw_rd[g], preferred_element_type=f32)
        o = o[:HEAD_DIM] * (1.0 / (o[HEAD_DIM:HEAD_DIM + 1] + snk_rd[g]))
        for pp in range(SWA_GROUP // 2):
            both = jnp.concatenate([o[:, (2 * pp) * TQ:(2 * pp + 1) * TQ],
                                    o[:, (2 * pp + 1) * TQ:(2 * pp + 2) * TQ]], axis=0)
            sw_cols[(SWA_GROUP // 2) * g + pp] = both.T

    def tail():
        mixed_cols = [jnp.concatenate([na_out[qrow * n_pairs + p] for qrow in range(Q_ROWS)], axis=0)
                      for p in range(n_pairs)]
        mixed = jnp.concatenate(mixed_cols + sw_cols, axis=1)
        x_blk = x_ref[...]
        z = jnp.dot(x_blk.astype(bf16), wzb_ref[...], preferred_element_type=f32)
        mixed = (mixed * (z * jax.nn.sigmoid(z))).astype(bf16)
        y = jnp.dot(mixed, woutb_ref[...], preferred_element_type=f32)
        r = DEEPNORM_ALPHA * x_blk + y
        mu = jnp.mean(r, axis=-1, keepdims=True)
        d = r - mu
        var = jnp.mean(d * d, axis=-1, keepdims=True)
        out_ref[...] = d * lax.rsqrt(var + LN_EPS) * gain_ref[...] + beta_ref[...]

    return ([functools.partial(na, u) for u in range(Q_ROWS * n_pairs)],
            [functools.partial(sw, g) for g in range(SWA_KV_HEADS)], tail)


def _score_steps(t1, qa_ref, kta_ref, qbt_ref, kb_ref, band_ref, tbl_ref, sna_wr, ssw_wr, mna_wr, msw_wr,
                 *, rows, n_blocks):
    f32 = jnp.float32
    n_pairs = NA_HEADS // 2
    low = lax.broadcasted_iota(jnp.int32, (GRID_W, LANES), 1) < HEAD_DIM

    def na(u):
        qrow, p = divmod(u, n_pairs)
        r, r0 = _window_start(t1, qrow, rows)
        par = r0 & 1
        d0 = r0 - r + (NA_KH - 1)
        cols = slice(p * LANES, (p + 1) * LANES)
        qp = qa_ref[qrow * GRID_W:(qrow + 1) * GRID_W, cols]
        zero = jnp.zeros_like(qp)
        q2 = jnp.concatenate([jnp.where(low, qp, zero), jnp.where(low, zero, qp)], axis=0)
        kt = jnp.concatenate([kta_ref[0, par, (r0 + 2 * j + 1) // 2, cols, :] for j in range(KEY_PAIRS)],
                             axis=1)
        bias = jnp.concatenate(
            [jnp.concatenate([tbl_ref[d0 + 2 * j, 2 * p + hp] for j in range(KEY_PAIRS)], axis=1)
             for hp in range(2)], axis=0)
        s = jnp.dot(q2, kt, preferred_element_type=f32) + bias
        sna_wr[u] = s
        mna_wr[u] = functools.reduce(jnp.maximum, [s[:, c * LANES:(c + 1) * LANES] for c in range(KEY_PAIRS)])

    def sw(g):
        jl1 = jnp.maximum(t1 - 1, 0)
        jr1 = jnp.minimum(t1 + 1, n_blocks - 1)
        variant = jnp.where(t1 == 0, 0, jnp.where(t1 == n_blocks - 1, 2, 1))
        k3 = jnp.concatenate([kb_ref[0, pl.ds(pl.multiple_of(j * TQ, TQ), TQ), :] for j in (jl1, t1, jr1)],
                             axis=0)
        qt = jnp.concatenate(
            [qbt_ref[0, (SWA_GROUP * g + j) * HEAD_DIM:(SWA_GROUP * g + j + 1) * HEAD_DIM, :]
             for j in range(SWA_GROUP)], axis=1)
        parts = [jnp.zeros_like(qt)] * SWA_KV_HEADS
        parts[g] = qt
        s = jnp.dot(k3, jnp.concatenate(parts, axis=0), preferred_element_type=f32)
        blocks = []
        for blk in range(3):
            sb = s[blk * TQ:(blk + 1) * TQ]
            if blk != 1:
                band = band_ref[variant, blk * TQ:(blk + 1) * TQ, :]
                sb = sb + jnp.concatenate([band] * SWA_GROUP, axis=1)
            ssw_wr[g, blk * TQ:(blk + 1) * TQ, :] = sb
            blocks.append(sb)
        s_all = jnp.concatenate(blocks, axis=0)
        msw_wr[g] = jnp.max(s_all.reshape(SWA_KEYS // 8, 8, SWA_GROUP * TQ), axis=0)

    return ([functools.partial(na, u) for u in range(Q_ROWS * n_pairs)],
            [functools.partial(sw, g) for g in range(SWA_KV_HEADS)])


def kernel(x, w_in, rel_pos_bias, sink_logits, w_out, ln_gain, ln_bias):
    B, S, D = x.shape
    assert D == D_MODEL and S % PROJ_TM == 0 and S % GRID_W == 0
    rows = S // GRID_W
    assert rows >= NA_KH and rows % Q_ROWS == 0
    n_blocks = S // TQ
    n_tok = B * S
    bf16 = jnp.bfloat16
    x2 = x.reshape(n_tok, D)

    cos_tok, sin_tok, cos_feat, sin_feat = _rope_tables(S)
    band = jnp.asarray(_swa_band_tiles())

    for layer in range(DEPTH):
        tiles_per_seq = S // PROJ_TM
        blk128 = PROJ_TM // LANES
        seq_tile = lambda i: (i // tiles_per_seq, i % tiles_per_seq)
        qa, va, kb, kta, qbt, vbt = pl.pallas_call(
            functools.partial(_proj_kernel, tiles_per_seq=tiles_per_seq),
            grid=(n_tok // PROJ_TM,),
            in_specs=[
                pl.BlockSpec((PROJ_TM, D), lambda i: (i, 0)),
                pl.BlockSpec((D, IN_WIDTH), lambda i: (0, 0), pipeline_mode=pl.Buffered(1)),
                pl.BlockSpec((PROJ_TM, LANES), lambda i: (i % tiles_per_seq, 0)),
                pl.BlockSpec((PROJ_TM, LANES), lambda i: (i % tiles_per_seq, 0)),
                pl.BlockSpec((HEAD_DIM, PROJ_TM), lambda i: (0, i % tiles_per_seq)),
                pl.BlockSpec((HEAD_DIM, PROJ_TM), lambda i: (0, i % tiles_per_seq)),
            ],
            out_specs=[
                pl.BlockSpec((PROJ_TM, NA_WIDTH), lambda i: (i, 0)),
                pl.BlockSpec((PROJ_TM, NA_WIDTH), lambda i: (i, 0)),
                pl.BlockSpec((PROJ_TM, SWA_KV_WIDTH), lambda i: (i, 0)),
                pl.BlockSpec((1, 2, blk128, NA_WIDTH, LANES), lambda i: (seq_tile(i)[0], 0, seq_tile(i)[1], 0, 0)),
                pl.BlockSpec((1, SWA_WIDTH, PROJ_TM), lambda i: (seq_tile(i)[0], 0, seq_tile(i)[1])),
                pl.BlockSpec((1, blk128, SWA_KV_WIDTH, LANES), lambda i: (seq_tile(i)[0], seq_tile(i)[1], 0, 0)),
            ],
            out_shape=[
                jax.ShapeDtypeStruct((n_tok, NA_WIDTH), bf16),
                jax.ShapeDtypeStruct((n_tok, NA_WIDTH), bf16),
                jax.ShapeDtypeStruct((n_tok, SWA_KV_WIDTH), bf16),
                jax.ShapeDtypeStruct((B, 2, S // LANES, NA_WIDTH, LANES), bf16),
                jax.ShapeDtypeStruct((B, SWA_WIDTH, S), bf16),
                jax.ShapeDtypeStruct((B, S // LANES, SWA_KV_WIDTH, LANES), bf16),
            ],
            scratch_shapes=[pltpu.VMEM((D, TOK_WIDTH), bf16), pltpu.VMEM((FEAT_ROWS, D), bf16),
                            pltpu.VMEM((NA_WIDTH, LANES), jnp.float32)],
            compiler_params=pltpu.CompilerParams(
                dimension_semantics=("arbitrary",), vmem_limit_bytes=PROJ_VMEM_LIMIT),
            name="in_proj",
        )(x2, w_in[layer], cos_tok, sin_tok, cos_feat, sin_feat)

        rel_rows = jnp.pad(rel_pos_bias[layer].astype(jnp.float32).reshape(NA_HEADS * N_DR, N_DC),
                           ((0, 0), (0, LANES - N_DC)))
        n_total = B * n_blocks
        gate_blocks = [(_W_ZA + c * GATE_BLOCK) // GATE_BLOCK for c in range(NA_WIDTH // GATE_BLOCK)] + \
                      [(_W_ZB + c * GATE_BLOCK) // GATE_BLOCK for c in range(SWA_WIDTH // GATE_BLOCK)]
        assert _W_ZA % GATE_BLOCK == 0 and _W_ZB % GATE_BLOCK == 0
        assert TICKS % 2 == 0 and n_blocks % TICKS == 0
        blk_of = lambda h: (lambda j: jnp.clip(TICKS * (j - 1) + 1 + h, 0, n_total - 1))
        k_blks = [blk_of(TICKS - 2), blk_of(TICKS - 1)]
        fin = lambda j: jnp.maximum(j - 1, 0)
        q_spec = lambda blk: pl.BlockSpec((TQ, NA_WIDTH), lambda j, *_: (blk(j), 0))
        kta_spec = lambda blk: pl.BlockSpec((1, 2, S // LANES, NA_WIDTH, LANES),
                                            lambda j, *_: (blk(j) // n_blocks, 0, 0, 0, 0))
        qbt_spec = lambda blk: pl.BlockSpec((1, SWA_WIDTH, TQ),
                                            lambda j, *_: (blk(j) // n_blocks, 0, blk(j) % n_blocks))
        kb_spec = lambda blk: pl.BlockSpec((1, S, SWA_KV_WIDTH), lambda j, *_: (blk(j) // n_blocks, 0, 0))
        fin_batch = lambda j: (TICKS * fin(j)) // n_blocks
        grid_spec = pltpu.PrefetchScalarGridSpec(
            num_scalar_prefetch=1,
            grid=(n_total // TICKS + 1,),
            in_specs=[
                *[q_spec(blk_of(h)) for h in range(TICKS)],
                *[kta_spec(blk) for blk in k_blks],
                pl.BlockSpec((1, S, NA_WIDTH), lambda j, *_: (fin_batch(j), 0, 0)),
                *[qbt_spec(blk_of(h)) for h in range(TICKS)],
                *[kb_spec(blk) for blk in k_blks],
                pl.BlockSpec((1, S // LANES, SWA_KV_WIDTH, LANES), lambda j, *_: (fin_batch(j), 0, 0, 0)),
                pl.BlockSpec((TICKS * TQ, D), lambda j, *_: (fin(j), 0)),
                pl.BlockSpec((MIX_WIDTH, D), lambda j, *_: (0, 0), pipeline_mode=pl.Buffered(1)),
                *[pl.BlockSpec((D, GATE_BLOCK), functools.partial(lambda c, j, *_: (0, c), c),
                               pipeline_mode=pl.Buffered(1)) for c in gate_blocks],
                pl.BlockSpec((NA_HEADS * N_DR, LANES), lambda j, *_: (0, 0)),
                pl.BlockSpec((3, SWA_KEYS, TQ), lambda j, *_: (0, 0, 0)),
                pl.BlockSpec((1, D), lambda j, *_: (0, 0)),
                pl.BlockSpec((1, D), lambda j, *_: (0, 0)),
            ],
            out_specs=pl.BlockSpec((TICKS * TQ, D), lambda j, *_: (fin(j), 0)),
            scratch_shapes=[pltpu.VMEM((MIX_WIDTH, D), bf16),
                            pltpu.VMEM((D, MIX_WIDTH), bf16),
                            pltpu.VMEM((N_BIAS_TILES, NA_HEADS, GRID_W, 2 * GRID_W), jnp.float32),
                            pltpu.VMEM((2, Q_ROWS * NA_HEADS // 2, TQ, NA_KEYS), jnp.float32),
                            pltpu.VMEM((2, SWA_KV_HEADS, SWA_KEYS, SWA_GROUP * TQ), jnp.float32),
                            pltpu.VMEM((2, Q_ROWS * NA_HEADS // 2, TQ, LANES), jnp.float32),
                            pltpu.VMEM((2, SWA_KV_HEADS, 8, SWA_GROUP * TQ), jnp.float32)],
        )
        x2 = pl.pallas_call(
            functools.partial(_attn_kernel, rows=rows, n_blocks=n_blocks, n_total=n_total),
            grid_spec=grid_spec,
            out_shape=jax.ShapeDtypeStruct((n_tok, D), jnp.float32),
            compiler_params=pltpu.CompilerParams(
                dimension_semantics=("arbitrary",), vmem_limit_bytes=ATTN_VMEM_LIMIT),
            name="attn_out",
        )(sink_logits[layer].astype(jnp.float32),
          *([qa] * TICKS), *([kta] * len(k_blks)), va.reshape(B, S, NA_WIDTH), *([qbt] * TICKS),
          *([kb.reshape(B, S, SWA_KV_WIDTH)] * len(k_blks)), vbt, x2,
          w_out[layer], *([w_in[layer]] * len(gate_blocks)), rel_rows, band, ln_gain[layer].reshape(1, D), ln_bias[layer].reshape(1, D))
    return x2.reshape(B, S, D)
```

```python
import functools

import numpy as np
import jax
import jax.numpy as jnp
from jax import lax
from jax.experimental import pallas as pl
from jax.experimental.pallas import tpu as pltpu

D_MODEL = 1024
HEAD_DIM = 64
NA_HEADS = 8
NA_WIDTH = NA_HEADS * HEAD_DIM
NA_KH = 8
NA_KW = 16
GRID_W = 64
SWA_Q_HEADS = 8
SWA_KV_HEADS = 2
SWA_GROUP = SWA_Q_HEADS // SWA_KV_HEADS
SWA_WIDTH = SWA_Q_HEADS * HEAD_DIM
SWA_KV_WIDTH = SWA_KV_HEADS * HEAD_DIM
SWA_WINDOW = 128
ROPE_THETA = 10000.0
MIX_WIDTH = NA_WIDTH + SWA_WIDTH
LN_EPS = 1e-5
MASK_VALUE = -1e30
DEPTH = 1
DEEPNORM_ALPHA = (2.0 * DEPTH) ** 0.25
LOG2E = 1.4426950408889634
Q_SCALE = HEAD_DIM ** -0.5 * LOG2E
N_DR = 2 * NA_KH - 1
N_DC = 2 * NA_KW - 1

LANES = 128
BF16_ROWS = 16
PAIR = 2 * HEAD_DIM
assert PAIR == LANES and 2 * GRID_W == LANES and SWA_KV_WIDTH == LANES

PROJ_TM = 1024
TQ = 128
TICKS = 4
V7X_VMEM_BYTES = 64 * 1024 * 1024
PROJ_VMEM_LIMIT = V7X_VMEM_BYTES * 7 // 8
ATTN_VMEM_LIMIT = V7X_VMEM_BYTES * 31 // 32
Q_ROWS = TQ // GRID_W
NA_KEYS = NA_KH * GRID_W
KEY_PAIRS = NA_KEYS // LANES
N_BIAS_TILES = N_DR - 1
SWA_KEYS = 3 * TQ

_QA = (0, NA_WIDTH)
_VA = (_QA[1], _QA[1] + NA_WIDTH)
_KB = (_VA[1], _VA[1] + SWA_KV_WIDTH)
TOK_WIDTH = _KB[1]
GATE_BLOCK = 256
_T_KA = (0, NA_WIDTH)
_T_QB = (_T_KA[1], _T_KA[1] + SWA_WIDTH)
_T_VB = (_T_QB[1], _T_QB[1] + SWA_KV_WIDTH)
FEAT_ROWS = _T_VB[1]
_W_QA = 0
_W_KA = _W_QA + NA_WIDTH
_W_VA = _W_KA + NA_WIDTH
_W_ZA = _W_VA + NA_WIDTH
_W_QB = _W_ZA + NA_WIDTH
_W_KB = _W_QB + SWA_WIDTH
_W_VB = _W_KB + SWA_KV_WIDTH
_W_ZB = _W_VB + SWA_KV_WIDTH
IN_WIDTH = _W_ZB + SWA_WIDTH


def _swa_band_tiles():
    q = np.arange(TQ)[None, :]
    k = np.arange(SWA_KEYS)[:, None] - TQ
    band = np.abs(k - q) <= SWA_WINDOW
    left_ok = np.concatenate([np.zeros((TQ, 1), bool), np.ones((2 * TQ, 1), bool)], axis=0)
    right_ok = np.concatenate([np.ones((2 * TQ, 1), bool), np.zeros((TQ, 1), bool)], axis=0)
    variants = np.stack([band & left_ok, band, band & right_ok])
    assert variants[:, TQ:2 * TQ].all()
    return np.where(variants, 0.0, MASK_VALUE).astype(np.float32)


def _rope_tables(seq):
    inv_freq = ROPE_THETA ** (-jnp.arange(0, HEAD_DIM, 2, dtype=jnp.float32) / HEAD_DIM)
    ang = jnp.arange(seq, dtype=jnp.int32).astype(jnp.float32)[:, None] * inv_freq[None, :]
    cos = jnp.cos(ang)
    sin = jnp.sin(ang)
    cos_head = jnp.concatenate([cos, cos], axis=1)
    sin_head = jnp.concatenate([-sin, sin], axis=1)
    cos_tok = jnp.concatenate([cos_head, cos_head], axis=1)
    sin_tok = jnp.concatenate([sin_head, sin_head], axis=1)
    return cos_tok, sin_tok, cos_head.T * Q_SCALE, sin_head.T * Q_SCALE


def _build_proj_weights(win_ref, wtok_ref, wft_ref):
    bf16 = jnp.bfloat16
    wtok_ref[:, _QA[0]:_QA[1]] = win_ref[:, _W_QA:_W_QA + NA_WIDTH].astype(bf16)
    wtok_ref[:, _VA[0]:_VA[1]] = win_ref[:, _W_VA:_W_VA + NA_WIDTH].astype(bf16)
    wtok_ref[:, _KB[0]:_KB[1]] = win_ref[:, _W_KB:_W_KB + SWA_KV_WIDTH].astype(bf16)
    for dst, src, width in ((_T_KA[0], _W_KA, NA_WIDTH), (_T_QB[0], _W_QB, SWA_WIDTH), (_T_VB[0], _W_VB, SWA_KV_WIDTH)):
        for c in range(width // LANES):
            wft_ref[dst + c * LANES:dst + (c + 1) * LANES, :] = \
                win_ref[:, src + c * LANES:src + (c + 1) * LANES].T.astype(bf16)


def _proj_kernel(x_ref, win_ref, cos_ref, sin_ref, cost_ref, sint_ref,
                 qa_ref, va_ref, kb_ref, kta_ref, qbt_ref, vbt_ref,
                 wtok_ref, wft_ref, carry_ref, *, tiles_per_seq):
    bf16 = jnp.bfloat16
    i = pl.program_id(0)
    half = HEAD_DIM // 2
    n_blk = PROJ_TM // LANES

    @pl.when(i == 0)
    def _():
        _build_proj_weights(win_ref, wtok_ref, wft_ref)

    @pl.when(i % tiles_per_seq == 0)
    def _():
        carry_ref[...] = jnp.zeros_like(carry_ref)

    xb = x_ref[...].astype(bf16)

    def proj(span):
        return jnp.dot(xb, wtok_ref[:, span[0]:span[1]], preferred_element_type=jnp.float32)

    qa_ref[...] = (proj(_QA) * Q_SCALE).astype(bf16)
    va_ref[...] = proj(_VA).astype(bf16)

    kb = proj(_KB)
    lane = lax.broadcasted_iota(jnp.int32, (PROJ_TM, LANES), 1)
    low_half = (lane % HEAD_DIM) < half
    partner = jnp.where(low_half, pltpu.roll(kb, LANES - half, 1), pltpu.roll(kb, half, 1))
    kb_ref[...] = (kb * cos_ref[...] + partner * sin_ref[...]).astype(bf16)

    ft = lax.dot_general(wft_ref[...], xb, (((1,), (1,)), ((), ())), preferred_element_type=jnp.float32)
    cols = [carry_ref[...]] + [ft[_T_KA[0]:_T_KA[1], j * LANES:(j + 1) * LANES] for j in range(n_blk)]
    low_k = lax.broadcasted_iota(jnp.int32, (NA_WIDTH, LANES), 1) < GRID_W
    rolled = [pltpu.roll(c, GRID_W, 1) for c in cols]
    for j in range(n_blk):
        kta_ref[0, 0, j] = cols[j + 1].astype(bf16)
        kta_ref[0, 1, j] = jnp.where(low_k, rolled[j], rolled[j + 1]).astype(bf16)
    carry_ref[...] = cols[n_blk]

    cos_t = cost_ref[...]
    sin_t = sint_ref[...]
    for h in range(SWA_Q_HEADS):
        blk = ft[_T_QB[0] + h * HEAD_DIM:_T_QB[0] + (h + 1) * HEAD_DIM]
        partner_t = jnp.concatenate([blk[half:], blk[:half]], axis=0)
        qbt_ref[0, h * HEAD_DIM:(h + 1) * HEAD_DIM, :] = (blk * cos_t + partner_t * sin_t).astype(bf16)

    vt = ft[_T_VB[0]:_T_VB[1]].astype(bf16)
    for j in range(n_blk):
        vbt_ref[0, j] = vt[:, j * LANES:(j + 1) * LANES]


def _build_na_bias_tiles(rel_ref, tbl_ref):
    c = lax.broadcasted_iota(jnp.int32, (GRID_W, LANES), 0)
    lane = lax.broadcasted_iota(jnp.int32, (GRID_W, LANES), 1)
    kc = lane & (GRID_W - 1)
    start = jnp.clip(c - NA_KW // 2, 0, GRID_W - NA_KW)
    in_win = (kc >= start) & (kc < start + NA_KW)
    low = lane < GRID_W
    masked = jnp.full((GRID_W, LANES), MASK_VALUE, jnp.float32)
    for h in range(NA_HEADS):
        def half(dr, upper):
            row = jnp.broadcast_to(rel_ref[pl.ds(h * N_DR + dr, 1), :] * LOG2E, (GRID_W, LANES))
            shift = (LANES - (NA_KW - 1) + (GRID_W if upper else 0)) % LANES
            return pltpu.roll(row, shift, 1, stride=1, stride_axis=0)

        lower = [half(dr, False) for dr in range(N_DR - 1)]
        upper = [half(dr, True) for dr in range(1, N_DR)]
        for d in range(N_BIAS_TILES):
            tbl_ref[d, h] = jnp.where(in_win, jnp.where(low, lower[d], upper[d]), masked)


def _attn_kernel(sink_ref, *refs, rows, n_blocks, n_total):
    refs = list(refs)
    take = lambda n: [refs.pop(0) for _ in range(n)]
    qa_refs = take(TICKS)
    kta_lo, kta_hi, va_ref = take(3)
    qbt_refs = take(TICKS)
    kb_lo, kb_hi, vbt_ref, x_ref, wout_ref = take(5)
    wz_refs = take(MIX_WIDTH // GATE_BLOCK)
    (rel_ref, band_ref, gain_ref, beta_ref, out_ref,
     woutb_ref, wzb_ref, tbl_ref, sna_ref, ssw_ref, mna_ref, msw_ref) = refs
    j = pl.program_id(0)

    @pl.when(j == 0)
    def _():
        woutb_ref[...] = wout_ref[...].astype(jnp.bfloat16)
        for c, wz_ref in enumerate(wz_refs):
            wzb_ref[:, c * GATE_BLOCK:(c + 1) * GATE_BLOCK] = wz_ref[...].astype(jnp.bfloat16)
        _build_na_bias_tiles(rel_ref, tbl_ref)
        for ref in (sna_ref, ssw_ref, mna_ref, msw_ref):
            ref[0] = jnp.zeros(ref.shape[1:], ref.dtype)

    n_units = Q_ROWS * NA_HEADS // 2
    for h in range(TICKS):
        i = TICKS * (j - 1) + 1 + h
        last = h == TICKS - 1
        rd, wr = h % 2, 1 - h % 2
        ena, esw, snk = [None] * n_units, [None] * SWA_KV_HEADS, [None] * SWA_KV_HEADS
        sm_na, sm_sw = _softmax_steps(sink_ref, sna_ref.at[rd], ssw_ref.at[rd], mna_ref.at[rd], msw_ref.at[rd],
                                      ena, esw, snk)
        out_na, out_sw, mixed = _output_steps(jnp.maximum(i - 1, 0) % n_blocks, va_ref, vbt_ref, ena, esw, snk,
                                              rows=rows, n_blocks=n_blocks)
        sc_na, sc_sw = _score_steps(jnp.clip(i, 0, n_total - 1) % n_blocks, qa_refs[h], kta_hi if last else kta_lo,
                                    qbt_refs[h], kb_hi if last else kb_lo, band_ref, tbl_ref,
                                    sna_ref.at[wr], ssw_ref.at[wr], mna_ref.at[wr], msw_ref.at[wr],
                                    rows=rows, n_blocks=n_blocks)
        for step in sm_na + sm_sw + out_na + out_sw:
            step()
        rows_h = pl.ds(h * TQ, TQ)
        _project_rows(mixed(), wzb_ref, woutb_ref, x_ref.at[rows_h], gain_ref, beta_ref, out_ref.at[rows_h])
        for step in sc_na + sc_sw:
            step()


def _window_start(t, qrow, rows):
    kh = min(NA_KH, rows)
    r = Q_ROWS * t + qrow
    return r, jnp.clip(r - kh // 2, 0, rows - kh)


def _softmax_steps(sink_ref, sna_rd, ssw_rd, mna_rd, msw_rd, ena_wr, esw_wr, snk_wr):
    f32, bf16 = jnp.float32, jnp.bfloat16

    def na(u):
        m = jnp.max(mna_rd[u], axis=-1, keepdims=True).astype(bf16)
        ena_wr[u] = jnp.exp2(sna_rd[u] - m)

    def sw(g):
        s_ref = ssw_rd.at[g]
        sink = jnp.concatenate(
            [jnp.full((1, TQ), sink_ref[SWA_GROUP * g + j] * LOG2E, f32) for j in range(SWA_GROUP)], axis=1)
        m = jnp.maximum(jnp.max(msw_rd[g], axis=0, keepdims=True), sink).astype(bf16)
        esw_wr[g] = jnp.exp2(s_ref[...] - m)
        snk_wr[g] = jnp.exp2(sink - m.astype(f32))

    return ([functools.partial(na, u) for u in range(Q_ROWS * NA_HEADS // 2)],
            [functools.partial(sw, g) for g in range(SWA_KV_HEADS)])


def _output_steps(t, va_ref, vbt_ref, ena_rd, esw_rd, snk_rd, *, rows, n_blocks):
    f32, bf16 = jnp.float32, jnp.bfloat16
    n_pairs = NA_HEADS // 2
    low = lax.broadcasted_iota(jnp.int32, (GRID_W, LANES), 1) < HEAD_DIM
    jl = jnp.maximum(t - 1, 0)
    jr = jnp.minimum(t + 1, n_blocks - 1)
    na_out = [None] * (Q_ROWS * n_pairs)
    sw_cols = [None] * (SWA_Q_HEADS // 2)

    def na(u):
        qrow, p = divmod(u, n_pairs)
        _, r0 = _window_start(t, qrow, rows)
        key0 = pl.multiple_of(r0 * GRID_W, GRID_W)
        v = va_ref[0, pl.ds(key0, NA_KEYS), p * LANES:(p + 1) * LANES]
        v1 = jnp.concatenate([v, jnp.ones_like(v)], axis=1)
        o = jnp.dot(ena_rd[u], v1, preferred_element_type=f32)
        o = o[:, :LANES] * (1.0 / o[:, LANES:])
        na_out[u] = jnp.where(low, o[:GRID_W], o[GRID_W:])

    def sw(g):
        vt = jnp.concatenate([vbt_ref[0, j, g * HEAD_DIM:(g + 1) * HEAD_DIM, :] for j in (jl, t, jr)], axis=1)
        v1 = jnp.concatenate([vt, jnp.ones((BF16_ROWS, SWA_KEYS), bf16)], axis=0)
        o = jnp.dot(v1, esw_rd[g], preferred_element_type=f32)
        o = o[:HEAD_DIM] * (1.0 / (o[HEAD_DIM:HEAD_DIM + 1] + snk_rd[g]))
        for pp in range(SWA_GROUP // 2):
            both = jnp.concatenate([o[:, (2 * pp) * TQ:(2 * pp + 1) * TQ],
                                    o[:, (2 * pp + 1) * TQ:(2 * pp + 2) * TQ]], axis=0)
            sw_cols[(SWA_GROUP // 2) * g + pp] = both.T

    def mixed():
        mixed_cols = [jnp.concatenate([na_out[qrow * n_pairs + p] for qrow in range(Q_ROWS)], axis=0)
                      for p in range(n_pairs)]
        return jnp.concatenate(mixed_cols + sw_cols, axis=1)

    return ([functools.partial(na, u) for u in range(Q_ROWS * n_pairs)],
            [functools.partial(sw, g) for g in range(SWA_KV_HEADS)], mixed)


def _project_rows(mixed, wzb_ref, woutb_ref, x_ref, gain_ref, beta_ref, out_ref):
    f32, bf16 = jnp.float32, jnp.bfloat16
    x_blk = x_ref[...]
    z = jnp.dot(x_blk.astype(bf16), wzb_ref[...], preferred_element_type=f32)
    gated = (mixed * (z * jax.nn.sigmoid(z))).astype(bf16)
    y = jnp.dot(gated, woutb_ref[...], preferred_element_type=f32)
    r = DEEPNORM_ALPHA * x_blk + y
    mu = jnp.mean(r, axis=-1, keepdims=True)
    d = r - mu
    var = jnp.mean(d * d, axis=-1, keepdims=True)
    out_ref[...] = d * lax.rsqrt(var + LN_EPS) * gain_ref[...] + beta_ref[...]


def _score_steps(t1, qa_ref, kta_ref, qbt_ref, kb_ref, band_ref, tbl_ref, sna_wr, ssw_wr, mna_wr, msw_wr,
                 *, rows, n_blocks):
    f32 = jnp.float32
    n_pairs = NA_HEADS // 2
    low = lax.broadcasted_iota(jnp.int32, (GRID_W, LANES), 1) < HEAD_DIM

    def na(u):
        qrow, p = divmod(u, n_pairs)
        r, r0 = _window_start(t1, qrow, rows)
        par = r0 & 1
        d0 = r0 - r + (NA_KH - 1)
        cols = slice(p * LANES, (p + 1) * LANES)
        qp = qa_ref[qrow * GRID_W:(qrow + 1) * GRID_W, cols]
        zero = jnp.zeros_like(qp)
        q2 = jnp.concatenate([jnp.where(low, qp, zero), jnp.where(low, zero, qp)], axis=0)
        kt = jnp.concatenate([kta_ref[0, par, (r0 + 2 * j + 1) // 2, cols, :] for j in range(KEY_PAIRS)],
                             axis=1)
        bias = jnp.concatenate(
            [jnp.concatenate([tbl_ref[d0 + 2 * j, 2 * p + hp] for j in range(KEY_PAIRS)], axis=1)
             for hp in range(2)], axis=0)
        s = jnp.dot(q2, kt, preferred_element_type=f32) + bias
        sna_wr[u] = s.astype(sna_wr.dtype)
        mna_wr[u] = functools.reduce(jnp.maximum, [s[:, c * LANES:(c + 1) * LANES] for c in range(KEY_PAIRS)])

    def sw(g):
        jl1 = jnp.maximum(t1 - 1, 0)
        jr1 = jnp.minimum(t1 + 1, n_blocks - 1)
        variant = jnp.where(t1 == 0, 0, jnp.where(t1 == n_blocks - 1, 2, 1))
        k3 = jnp.concatenate([kb_ref[0, pl.ds(pl.multiple_of(j * TQ, TQ), TQ), :] for j in (jl1, t1, jr1)],
                             axis=0)
        qt = jnp.concatenate(
            [qbt_ref[0, (SWA_GROUP * g + j) * HEAD_DIM:(SWA_GROUP * g + j + 1) * HEAD_DIM, :]
             for j in range(SWA_GROUP)], axis=1)
        parts = [jnp.zeros_like(qt)] * SWA_KV_HEADS
        parts[g] = qt
        s = jnp.dot(k3, jnp.concatenate(parts, axis=0), preferred_element_type=f32)
        blocks = []
        for blk in range(3):
            sb = s[blk * TQ:(blk + 1) * TQ]
            if blk != 1:
                band = band_ref[variant, blk * TQ:(blk + 1) * TQ, :]
                sb = sb + jnp.concatenate([band] * SWA_GROUP, axis=1)
            ssw_wr[g, blk * TQ:(blk + 1) * TQ, :] = sb.astype(ssw_wr.dtype)
            blocks.append(sb)
        s_all = jnp.concatenate(blocks, axis=0)
        msw_wr[g] = jnp.max(s_all.reshape(SWA_KEYS // 8, 8, SWA_GROUP * TQ), axis=0)

    return ([functools.partial(na, u) for u in range(Q_ROWS * n_pairs)],
            [functools.partial(sw, g) for g in range(SWA_KV_HEADS)])


def kernel(x, w_in, rel_pos_bias, sink_logits, w_out, ln_gain, ln_bias):
    B, S, D = x.shape
    assert D == D_MODEL and S % PROJ_TM == 0 and S % GRID_W == 0
    rows = S // GRID_W
    assert rows >= NA_KH and rows % Q_ROWS == 0
    n_blocks = S // TQ
    n_tok = B * S
    bf16 = jnp.bfloat16
    x2 = x.reshape(n_tok, D)

    cos_tok, sin_tok, cos_feat, sin_feat = _rope_tables(S)
    band = jnp.asarray(_swa_band_tiles())

    for layer in range(DEPTH):
        tiles_per_seq = S // PROJ_TM
        blk128 = PROJ_TM // LANES
        seq_tile = lambda i: (i // tiles_per_seq, i % tiles_per_seq)
        qa, va, kb, kta, qbt, vbt = pl.pallas_call(
            functools.partial(_proj_kernel, tiles_per_seq=tiles_per_seq),
            grid=(n_tok // PROJ_TM,),
            in_specs=[
                pl.BlockSpec((PROJ_TM, D), lambda i: (i, 0)),
                pl.BlockSpec((D, IN_WIDTH), lambda i: (0, 0), pipeline_mode=pl.Buffered(1)),
                pl.BlockSpec((PROJ_TM, LANES), lambda i: (i % tiles_per_seq, 0)),
                pl.BlockSpec((PROJ_TM, LANES), lambda i: (i % tiles_per_seq, 0)),
                pl.BlockSpec((HEAD_DIM, PROJ_TM), lambda i: (0, i % tiles_per_seq)),
                pl.BlockSpec((HEAD_DIM, PROJ_TM), lambda i: (0, i % tiles_per_seq)),
            ],
            out_specs=[
                pl.BlockSpec((PROJ_TM, NA_WIDTH), lambda i: (i, 0)),
                pl.BlockSpec((PROJ_TM, NA_WIDTH), lambda i: (i, 0)),
                pl.BlockSpec((PROJ_TM, SWA_KV_WIDTH), lambda i: (i, 0)),
                pl.BlockSpec((1, 2, blk128, NA_WIDTH, LANES), lambda i: (seq_tile(i)[0], 0, seq_tile(i)[1], 0, 0)),
                pl.BlockSpec((1, SWA_WIDTH, PROJ_TM), lambda i: (seq_tile(i)[0], 0, seq_tile(i)[1])),
                pl.BlockSpec((1, blk128, SWA_KV_WIDTH, LANES), lambda i: (seq_tile(i)[0], seq_tile(i)[1], 0, 0)),
            ],
            out_shape=[
                jax.ShapeDtypeStruct((n_tok, NA_WIDTH), bf16),
                jax.ShapeDtypeStruct((n_tok, NA_WIDTH), bf16),
                jax.ShapeDtypeStruct((n_tok, SWA_KV_WIDTH), bf16),
                jax.ShapeDtypeStruct((B, 2, S // LANES, NA_WIDTH, LANES), bf16),
                jax.ShapeDtypeStruct((B, SWA_WIDTH, S), bf16),
                jax.ShapeDtypeStruct((B, S // LANES, SWA_KV_WIDTH, LANES), bf16),
            ],
            scratch_shapes=[pltpu.VMEM((D, TOK_WIDTH), bf16), pltpu.VMEM((FEAT_ROWS, D), bf16),
                            pltpu.VMEM((NA_WIDTH, LANES), jnp.float32)],
            compiler_params=pltpu.CompilerParams(
                dimension_semantics=("arbitrary",), vmem_limit_bytes=PROJ_VMEM_LIMIT),
            name="in_proj",
        )(x2, w_in[layer], cos_tok, sin_tok, cos_feat, sin_feat)

        rel_rows = jnp.pad(rel_pos_bias[layer].astype(jnp.float32).reshape(NA_HEADS * N_DR, N_DC),
                           ((0, 0), (0, LANES - N_DC)))
        n_total = B * n_blocks
        gate_blocks = [(_W_ZA + c * GATE_BLOCK) // GATE_BLOCK for c in range(NA_WIDTH // GATE_BLOCK)] + \
                      [(_W_ZB + c * GATE_BLOCK) // GATE_BLOCK for c in range(SWA_WIDTH // GATE_BLOCK)]
        assert _W_ZA % GATE_BLOCK == 0 and _W_ZB % GATE_BLOCK == 0
        assert TICKS % 2 == 0 and n_blocks % TICKS == 0
        blk_of = lambda h: (lambda j: jnp.clip(TICKS * (j - 1) + 1 + h, 0, n_total - 1))
        k_blks = [blk_of(TICKS - 2), blk_of(TICKS - 1)]
        fin = lambda j: jnp.maximum(j - 1, 0)
        q_spec = lambda blk: pl.BlockSpec((TQ, NA_WIDTH), lambda j, *_: (blk(j), 0))
        kta_spec = lambda blk: pl.BlockSpec((1, 2, S // LANES, NA_WIDTH, LANES),
                                            lambda j, *_: (blk(j) // n_blocks, 0, 0, 0, 0))
        qbt_spec = lambda blk: pl.BlockSpec((1, SWA_WIDTH, TQ),
                                            lambda j, *_: (blk(j) // n_blocks, 0, blk(j) % n_blocks))
        kb_spec = lambda blk: pl.BlockSpec((1, S, SWA_KV_WIDTH), lambda j, *_: (blk(j) // n_blocks, 0, 0))
        fin_batch = lambda j: (TICKS * fin(j)) // n_blocks
        grid_spec = pltpu.PrefetchScalarGridSpec(
            num_scalar_prefetch=1,
            grid=(n_total // TICKS + 1,),
            in_specs=[
                *[q_spec(blk_of(h)) for h in range(TICKS)],
                *[kta_spec(blk) for blk in k_blks],
                pl.BlockSpec((1, S, NA_WIDTH), lambda j, *_: (fin_batch(j), 0, 0)),
                *[qbt_spec(blk_of(h)) for h in range(TICKS)],
                *[kb_spec(blk) for blk in k_blks],
                pl.BlockSpec((1, S // LANES, SWA_KV_WIDTH, LANES), lambda j, *_: (fin_batch(j), 0, 0, 0)),
                pl.BlockSpec((TICKS * TQ, D), lambda j, *_: (fin(j), 0)),
                pl.BlockSpec((MIX_WIDTH, D), lambda j, *_: (0, 0), pipeline_mode=pl.Buffered(1)),
                *[pl.BlockSpec((D, GATE_BLOCK), functools.partial(lambda c, j, *_: (0, c), c),
                               pipeline_mode=pl.Buffered(1)) for c in gate_blocks],
                pl.BlockSpec((NA_HEADS * N_DR, LANES), lambda j, *_: (0, 0)),
                pl.BlockSpec((3, SWA_KEYS, TQ), lambda j, *_: (0, 0, 0)),
                pl.BlockSpec((1, D), lambda j, *_: (0, 0)),
                pl.BlockSpec((1, D), lambda j, *_: (0, 0)),
            ],
            out_specs=pl.BlockSpec((TICKS * TQ, D), lambda j, *_: (fin(j), 0)),
            scratch_shapes=[pltpu.VMEM((MIX_WIDTH, D), bf16),
                            pltpu.VMEM((D, MIX_WIDTH), bf16),
                            pltpu.VMEM((N_BIAS_TILES, NA_HEADS, GRID_W, 2 * GRID_W), jnp.float32),
                            pltpu.VMEM((2, Q_ROWS * NA_HEADS // 2, TQ, NA_KEYS), bf16),
                            pltpu.VMEM((2, SWA_KV_HEADS, SWA_KEYS, SWA_GROUP * TQ), bf16),
                            pltpu.VMEM((2, Q_ROWS * NA_HEADS // 2, TQ, LANES), jnp.float32),
                            pltpu.VMEM((2, SWA_KV_HEADS, 8, SWA_GROUP * TQ), jnp.float32)],
        )
        x2 = pl.pallas_call(
            functools.partial(_attn_kernel, rows=rows, n_blocks=n_blocks, n_total=n_total),
            grid_spec=grid_spec,
            out_shape=jax.ShapeDtypeStruct((n_tok, D), jnp.float32),
            compiler_params=pltpu.CompilerParams(
                dimension_semantics=("arbitrary",), vmem_limit_bytes=ATTN_VMEM_LIMIT),
            name="attn_out",
        )(sink_logits[layer].astype(jnp.float32),
          *([qa] * TICKS), *([kta] * len(k_blks)), va.reshape(B, S, NA_WIDTH), *([qbt] * TICKS),
          *([kb.reshape(B, S, SWA_KV_WIDTH)] * len(k_blks)), vbt, x2,
          w_out[layer], *([w_in[layer]] * len(gate_blocks)), rel_rows, band, ln_gain[layer].reshape(1, D), ln_bias[layer].reshape(1, D))
    return x2.reshape(B, S, D)
```

```python
import functools

import numpy as np
import jax
import jax.numpy as jnp
from jax import lax
from jax.experimental import pallas as pl
from jax.experimental.pallas import tpu as pltpu

D_MODEL = 1024
HEAD_DIM = 64
NA_HEADS = 8
NA_WIDTH = NA_HEADS * HEAD_DIM
NA_KH = 8
NA_KW = 16
GRID_W = 64
SWA_Q_HEADS = 8
SWA_KV_HEADS = 2
SWA_GROUP = SWA_Q_HEADS // SWA_KV_HEADS
SWA_WIDTH = SWA_Q_HEADS * HEAD_DIM
SWA_KV_WIDTH = SWA_KV_HEADS * HEAD_DIM
SWA_WINDOW = 128
ROPE_THETA = 10000.0
MIX_WIDTH = NA_WIDTH + SWA_WIDTH
LN_EPS = 1e-5
MASK_VALUE = -1e30
DEPTH = 1
DEEPNORM_ALPHA = (2.0 * DEPTH) ** 0.25
LOG2E = 1.4426950408889634
Q_SCALE = HEAD_DIM ** -0.5 * LOG2E
N_DR = 2 * NA_KH - 1
N_DC = 2 * NA_KW - 1

LANES = 128
BF16_ROWS = 16
PAIR = 2 * HEAD_DIM
assert PAIR == LANES and 2 * GRID_W == LANES and SWA_KV_WIDTH == LANES

PROJ_TM = 1024
TQ = 128
TICKS = 4
V7X_VMEM_BYTES = 64 * 1024 * 1024
PROJ_VMEM_LIMIT = V7X_VMEM_BYTES * 7 // 8
ATTN_VMEM_LIMIT = V7X_VMEM_BYTES * 31 // 32
Q_ROWS = TQ // GRID_W
NA_KEYS = NA_KH * GRID_W
KEY_PAIRS = NA_KEYS // LANES
N_BIAS_TILES = N_DR - 1
SWA_KEYS = 3 * TQ

_QA = (0, NA_WIDTH)
_VA = (_QA[1], _QA[1] + NA_WIDTH)
_KB = (_VA[1], _VA[1] + SWA_KV_WIDTH)
TOK_WIDTH = _KB[1]
GATE_BLOCK = 256
_T_KA = (0, NA_WIDTH)
_T_QB = (_T_KA[1], _T_KA[1] + SWA_WIDTH)
_T_VB = (_T_QB[1], _T_QB[1] + SWA_KV_WIDTH)
FEAT_ROWS = _T_VB[1]
_W_QA = 0
_W_KA = _W_QA + NA_WIDTH
_W_VA = _W_KA + NA_WIDTH
_W_ZA = _W_VA + NA_WIDTH
_W_QB = _W_ZA + NA_WIDTH
_W_KB = _W_QB + SWA_WIDTH
_W_VB = _W_KB + SWA_KV_WIDTH
_W_ZB = _W_VB + SWA_KV_WIDTH
IN_WIDTH = _W_ZB + SWA_WIDTH


def _swa_band_tiles():
    q = np.arange(TQ)[None, :]
    k = np.arange(SWA_KEYS)[:, None] - TQ
    band = np.abs(k - q) <= SWA_WINDOW
    left_ok = np.concatenate([np.zeros((TQ, 1), bool), np.ones((2 * TQ, 1), bool)], axis=0)
    right_ok = np.concatenate([np.ones((2 * TQ, 1), bool), np.zeros((TQ, 1), bool)], axis=0)
    variants = np.stack([band & left_ok, band, band & right_ok])
    assert variants[:, TQ:2 * TQ].all()
    return np.where(variants, 0.0, MASK_VALUE).astype(np.float32)


def _rope_tables(seq):
    inv_freq = ROPE_THETA ** (-jnp.arange(0, HEAD_DIM, 2, dtype=jnp.float32) / HEAD_DIM)
    ang = jnp.arange(seq, dtype=jnp.int32).astype(jnp.float32)[:, None] * inv_freq[None, :]
    cos = jnp.cos(ang)
    sin = jnp.sin(ang)
    cos_head = jnp.concatenate([cos, cos], axis=1)
    sin_head = jnp.concatenate([-sin, sin], axis=1)
    cos_tok = jnp.concatenate([cos_head, cos_head], axis=1)
    sin_tok = jnp.concatenate([sin_head, sin_head], axis=1)
    return cos_tok, sin_tok, cos_head.T * Q_SCALE, sin_head.T * Q_SCALE


def _build_proj_weights(win_ref, wtok_ref, wft_ref):
    bf16 = jnp.bfloat16
    wtok_ref[:, _QA[0]:_QA[1]] = win_ref[:, _W_QA:_W_QA + NA_WIDTH].astype(bf16)
    wtok_ref[:, _VA[0]:_VA[1]] = win_ref[:, _W_VA:_W_VA + NA_WIDTH].astype(bf16)
    wtok_ref[:, _KB[0]:_KB[1]] = win_ref[:, _W_KB:_W_KB + SWA_KV_WIDTH].astype(bf16)
    for dst, src, width in ((_T_KA[0], _W_KA, NA_WIDTH), (_T_QB[0], _W_QB, SWA_WIDTH), (_T_VB[0], _W_VB, SWA_KV_WIDTH)):
        for c in range(width // LANES):
            wft_ref[dst + c * LANES:dst + (c + 1) * LANES, :] = \
                win_ref[:, src + c * LANES:src + (c + 1) * LANES].T.astype(bf16)


def _proj_kernel(x_ref, win_ref, cos_ref, sin_ref, cost_ref, sint_ref,
                 qa_ref, va_ref, kb_ref, kta_ref, qbt_ref, vbt_ref,
                 wtok_ref, wft_ref, carry_ref, *, tiles_per_seq):
    bf16 = jnp.bfloat16
    i = pl.program_id(0)
    half = HEAD_DIM // 2
    n_blk = PROJ_TM // LANES

    @pl.when(i == 0)
    def _():
        _build_proj_weights(win_ref, wtok_ref, wft_ref)

    @pl.when(i % tiles_per_seq == 0)
    def _():
        carry_ref[...] = jnp.zeros_like(carry_ref)

    xb = x_ref[...].astype(bf16)

    def proj(span):
        return jnp.dot(xb, wtok_ref[:, span[0]:span[1]], preferred_element_type=jnp.float32)

    qa_ref[...] = (proj(_QA) * Q_SCALE).astype(bf16)
    va_ref[...] = proj(_VA).astype(bf16)

    kb = proj(_KB)
    lane = lax.broadcasted_iota(jnp.int32, (PROJ_TM, LANES), 1)
    low_half = (lane % HEAD_DIM) < half
    partner = jnp.where(low_half, pltpu.roll(kb, LANES - half, 1), pltpu.roll(kb, half, 1))
    kb_ref[...] = (kb * cos_ref[...] + partner * sin_ref[...]).astype(bf16)

    ft = lax.dot_general(wft_ref[...], xb, (((1,), (1,)), ((), ())), preferred_element_type=jnp.float32)
    cols = [carry_ref[...]] + [ft[_T_KA[0]:_T_KA[1], j * LANES:(j + 1) * LANES] for j in range(n_blk)]
    low_k = lax.broadcasted_iota(jnp.int32, (NA_WIDTH, LANES), 1) < GRID_W
    rolled = [pltpu.roll(c, GRID_W, 1) for c in cols]
    for j in range(n_blk):
        kta_ref[0, 0, j] = cols[j + 1].astype(bf16)
        kta_ref[0, 1, j] = jnp.where(low_k, rolled[j], rolled[j + 1]).astype(bf16)
    carry_ref[...] = cols[n_blk]

    cos_t = cost_ref[...]
    sin_t = sint_ref[...]
    for h in range(SWA_Q_HEADS):
        blk = ft[_T_QB[0] + h * HEAD_DIM:_T_QB[0] + (h + 1) * HEAD_DIM]
        partner_t = jnp.concatenate([blk[half:], blk[:half]], axis=0)
        roped = (blk * cos_t + partner_t * sin_t).astype(bf16)
        for j in range(n_blk):
            qbt_ref[0, j, h * HEAD_DIM:(h + 1) * HEAD_DIM, :] = roped[:, j * LANES:(j + 1) * LANES]

    vt = ft[_T_VB[0]:_T_VB[1]].astype(bf16)
    for j in range(n_blk):
        vbt_ref[0, j] = vt[:, j * LANES:(j + 1) * LANES]


def _build_na_bias_tiles(rel_ref, tbl_ref):
    c = lax.broadcasted_iota(jnp.int32, (GRID_W, LANES), 0)
    lane = lax.broadcasted_iota(jnp.int32, (GRID_W, LANES), 1)
    kc = lane & (GRID_W - 1)
    start = jnp.clip(c - NA_KW // 2, 0, GRID_W - NA_KW)
    in_win = (kc >= start) & (kc < start + NA_KW)
    low = lane < GRID_W
    masked = jnp.full((GRID_W, LANES), MASK_VALUE, jnp.float32)
    for h in range(NA_HEADS):
        def half(dr, upper):
            row = jnp.broadcast_to(rel_ref[pl.ds(h * N_DR + dr, 1), :] * LOG2E, (GRID_W, LANES))
            shift = (LANES - (NA_KW - 1) + (GRID_W if upper else 0)) % LANES
            return pltpu.roll(row, shift, 1, stride=1, stride_axis=0)

        lower = [half(dr, False) for dr in range(N_DR - 1)]
        upper = [half(dr, True) for dr in range(1, N_DR)]
        for d in range(N_BIAS_TILES):
            tbl_ref[d, h] = jnp.where(in_win, jnp.where(low, lower[d], upper[d]), masked)


def _attn_kernel(sink_ref, *refs, rows, n_blocks, n_total):
    refs = list(refs)
    take = lambda n: [refs.pop(0) for _ in range(n)]
    qa_refs = take(TICKS)
    kta_lo, kta_hi, va_ref = take(3)
    qbt_refs = take(TICKS)
    kb_lo, kb_hi, vbt_ref, x_ref, wout_ref = take(5)
    wz_refs = take(MIX_WIDTH // GATE_BLOCK)
    (rel_ref, band_ref, gain_ref, beta_ref, out_ref,
     woutb_ref, wzb_ref, tbl_ref, sna_ref, ssw_ref, mna_ref, msw_ref) = refs
    j = pl.program_id(0)

    @pl.when(j == 0)
    def _():
        woutb_ref[...] = wout_ref[...].astype(jnp.bfloat16)
        for c, wz_ref in enumerate(wz_refs):
            wzb_ref[:, c * GATE_BLOCK:(c + 1) * GATE_BLOCK] = wz_ref[...].astype(jnp.bfloat16)
        _build_na_bias_tiles(rel_ref, tbl_ref)
        for ref in (sna_ref, ssw_ref, mna_ref, msw_ref):
            ref[0] = jnp.zeros(ref.shape[1:], ref.dtype)

    n_units = Q_ROWS * NA_HEADS // 2
    for h in range(TICKS):
        i = TICKS * (j - 1) + 1 + h
        last = h == TICKS - 1
        rd, wr = h % 2, 1 - h % 2
        ena, esw, snk = [None] * n_units, [None] * SWA_KV_HEADS, [None] * SWA_KV_HEADS
        sm_na, sm_sw = _softmax_steps(sink_ref, sna_ref.at[rd], ssw_ref.at[rd], mna_ref.at[rd], msw_ref.at[rd],
                                      ena, esw, snk)
        out_na, out_sw, mixed = _output_steps(jnp.maximum(i - 1, 0) % n_blocks, va_ref, vbt_ref, ena, esw, snk,
                                              rows=rows, n_blocks=n_blocks)
        sc_na, sc_sw = _score_steps(jnp.clip(i, 0, n_total - 1) % n_blocks, qa_refs[h], kta_hi if last else kta_lo,
                                    qbt_refs[h], kb_hi if last else kb_lo, band_ref, tbl_ref,
                                    sna_ref.at[wr], ssw_ref.at[wr], mna_ref.at[wr], msw_ref.at[wr],
                                    rows=rows, n_blocks=n_blocks)
        for step in sm_na + sm_sw + out_na + out_sw:
            step()
        rows_h = pl.ds(h * TQ, TQ)
        _project_rows(mixed(), wzb_ref, woutb_ref, x_ref.at[rows_h], gain_ref, beta_ref, out_ref.at[rows_h])
        for step in sc_na + sc_sw:
            step()


def _window_start(t, qrow, rows):
    kh = min(NA_KH, rows)
    r = Q_ROWS * t + qrow
    return r, jnp.clip(r - kh // 2, 0, rows - kh)


def _softmax_steps(sink_ref, sna_rd, ssw_rd, mna_rd, msw_rd, ena_wr, esw_wr, snk_wr):
    f32, bf16 = jnp.float32, jnp.bfloat16

    def na(u):
        s_ref = sna_rd.at[u]
        m = jnp.max(mna_rd[u], axis=-1, keepdims=True)
        ena_wr[u] = jnp.exp2(s_ref[...] - m).astype(bf16)

    def sw(g):
        s_ref = ssw_rd.at[g]
        sink = jnp.concatenate(
            [jnp.full((1, TQ), sink_ref[SWA_GROUP * g + j] * LOG2E, f32) for j in range(SWA_GROUP)], axis=1)
        m = jnp.maximum(jnp.max(msw_rd[g], axis=0, keepdims=True), sink)
        esw_wr[g] = jnp.exp2(s_ref[...] - m).astype(bf16)
        snk_wr[g] = jnp.exp2(sink - m)

    return ([functools.partial(na, u) for u in range(Q_ROWS * NA_HEADS // 2)],
            [functools.partial(sw, g) for g in range(SWA_KV_HEADS)])


def _output_steps(t, va_ref, vbt_ref, ena_rd, esw_rd, snk_rd, *, rows, n_blocks):
    f32, bf16 = jnp.float32, jnp.bfloat16
    n_pairs = NA_HEADS // 2
    low = lax.broadcasted_iota(jnp.int32, (GRID_W, LANES), 1) < HEAD_DIM
    jl = jnp.maximum(t - 1, 0)
    jr = jnp.minimum(t + 1, n_blocks - 1)
    na_out = [None] * (Q_ROWS * n_pairs)
    sw_cols = [None] * (SWA_Q_HEADS // 2)

    def na(u):
        qrow, p = divmod(u, n_pairs)
        _, r0 = _window_start(t, qrow, rows)
        key0 = pl.multiple_of(r0 * GRID_W, GRID_W)
        v = va_ref[0, pl.ds(key0, NA_KEYS), p * LANES:(p + 1) * LANES]
        v1 = jnp.concatenate([v, jnp.ones_like(v)], axis=1)
        o = jnp.dot(ena_rd[u], v1, preferred_element_type=f32)
        o = o[:, :LANES] * (1.0 / o[:, LANES:])
        na_out[u] = jnp.where(low, o[:GRID_W], o[GRID_W:])

    def sw(g):
        vt = jnp.concatenate([vbt_ref[0, j, g * HEAD_DIM:(g + 1) * HEAD_DIM, :] for j in (jl, t, jr)], axis=1)
        v1 = jnp.concatenate([vt, jnp.ones((BF16_ROWS, SWA_KEYS), bf16)], axis=0)
        o = jnp.dot(v1, esw_rd[g], preferred_element_type=f32)
        o = o[:HEAD_DIM] * (1.0 / (o[HEAD_DIM:HEAD_DIM + 1] + snk_rd[g]))
        for pp in range(SWA_GROUP // 2):
            both = jnp.concatenate([o[:, (2 * pp) * TQ:(2 * pp + 1) * TQ],
                                    o[:, (2 * pp + 1) * TQ:(2 * pp + 2) * TQ]], axis=0)
            sw_cols[(SWA_GROUP // 2) * g + pp] = both.T

    def mixed():
        mixed_cols = [jnp.concatenate([na_out[qrow * n_pairs + p] for qrow in range(Q_ROWS)], axis=0)
                      for p in range(n_pairs)]
        return jnp.concatenate(mixed_cols + sw_cols, axis=1)

    return ([functools.partial(na, u) for u in range(Q_ROWS * n_pairs)],
            [functools.partial(sw, g) for g in range(SWA_KV_HEADS)], mixed)


def _project_rows(mixed, wzb_ref, woutb_ref, x_ref, gain_ref, beta_ref, out_ref):
    f32, bf16 = jnp.float32, jnp.bfloat16
    x_blk = x_ref[...]
    z = jnp.dot(x_blk.astype(bf16), wzb_ref[...], preferred_element_type=f32)
    gated = (mixed * (z * jax.nn.sigmoid(z))).astype(bf16)
    y = jnp.dot(gated, woutb_ref[...], preferred_element_type=f32)
    r = DEEPNORM_ALPHA * x_blk + y
    mu = jnp.mean(r, axis=-1, keepdims=True)
    d = r - mu
    var = jnp.mean(d * d, axis=-1, keepdims=True)
    out_ref[...] = d * lax.rsqrt(var + LN_EPS) * gain_ref[...] + beta_ref[...]


def _score_steps(t1, qa_ref, kta_ref, qbt_ref, kb_ref, band_ref, tbl_ref, sna_wr, ssw_wr, mna_wr, msw_wr,
                 *, rows, n_blocks):
    f32 = jnp.float32
    n_pairs = NA_HEADS // 2
    low = lax.broadcasted_iota(jnp.int32, (GRID_W, LANES), 1) < HEAD_DIM

    def na(u):
        qrow, p = divmod(u, n_pairs)
        r, r0 = _window_start(t1, qrow, rows)
        par = r0 & 1
        d0 = r0 - r + (NA_KH - 1)
        cols = slice(p * LANES, (p + 1) * LANES)
        qp = qa_ref[qrow * GRID_W:(qrow + 1) * GRID_W, cols]
        zero = jnp.zeros_like(qp)
        q2 = jnp.concatenate([jnp.where(low, qp, zero), jnp.where(low, zero, qp)], axis=0)
        kt = jnp.concatenate([kta_ref[0, par, (r0 + 2 * j + 1) // 2, cols, :] for j in range(KEY_PAIRS)],
                             axis=1)
        bias = jnp.concatenate(
            [jnp.concatenate([tbl_ref[d0 + 2 * j, 2 * p + hp] for j in range(KEY_PAIRS)], axis=1)
             for hp in range(2)], axis=0)
        s = jnp.dot(q2, kt, preferred_element_type=f32) + bias
        sna_wr[u] = s
        mna_wr[u] = functools.reduce(jnp.maximum, [s[:, c * LANES:(c + 1) * LANES] for c in range(KEY_PAIRS)])

    def sw(g):
        jl1 = jnp.maximum(t1 - 1, 0)
        jr1 = jnp.minimum(t1 + 1, n_blocks - 1)
        variant = jnp.where(t1 == 0, 0, jnp.where(t1 == n_blocks - 1, 2, 1))
        k3 = jnp.concatenate([kb_ref[0, pl.ds(pl.multiple_of(j * TQ, TQ), TQ), :] for j in (jl1, t1, jr1)],
                             axis=0)
        qt = jnp.concatenate(
            [qbt_ref[0, 0, (SWA_GROUP * g + j) * HEAD_DIM:(SWA_GROUP * g + j + 1) * HEAD_DIM, :]
             for j in range(SWA_GROUP)], axis=1)
        parts = [jnp.zeros_like(qt)] * SWA_KV_HEADS
        parts[g] = qt
        s = jnp.dot(k3, jnp.concatenate(parts, axis=0), preferred_element_type=f32)
        blocks = []
        for blk in range(3):
            sb = s[blk * TQ:(blk + 1) * TQ]
            if blk != 1:
                band = band_ref[variant, blk * TQ:(blk + 1) * TQ, :]
                sb = sb + jnp.concatenate([band] * SWA_GROUP, axis=1)
            ssw_wr[g, blk * TQ:(blk + 1) * TQ, :] = sb
            blocks.append(sb)
        s_all = jnp.concatenate(blocks, axis=0)
        msw_wr[g] = jnp.max(s_all.reshape(SWA_KEYS // 8, 8, SWA_GROUP * TQ), axis=0)

    return ([functools.partial(na, u) for u in range(Q_ROWS * n_pairs)],
            [functools.partial(sw, g) for g in range(SWA_KV_HEADS)])


def kernel(x, w_in, rel_pos_bias, sink_logits, w_out, ln_gain, ln_bias):
    B, S, D = x.shape
    assert D == D_MODEL and S % PROJ_TM == 0 and S % GRID_W == 0
    rows = S // GRID_W
    assert rows >= NA_KH and rows % Q_ROWS == 0
    n_blocks = S // TQ
    n_tok = B * S
    bf16 = jnp.bfloat16
    x2 = x.reshape(n_tok, D)

    cos_tok, sin_tok, cos_feat, sin_feat = _rope_tables(S)
    band = jnp.asarray(_swa_band_tiles())

    for layer in range(DEPTH):
        tiles_per_seq = S // PROJ_TM
        blk128 = PROJ_TM // LANES
        seq_tile = lambda i: (i // tiles_per_seq, i % tiles_per_seq)
        qa, va, kb, kta, qbt, vbt = pl.pallas_call(
            functools.partial(_proj_kernel, tiles_per_seq=tiles_per_seq),
            grid=(n_tok // PROJ_TM,),
            in_specs=[
                pl.BlockSpec((PROJ_TM, D), lambda i: (i, 0)),
                pl.BlockSpec((D, IN_WIDTH), lambda i: (0, 0), pipeline_mode=pl.Buffered(1)),
                pl.BlockSpec((PROJ_TM, LANES), lambda i: (i % tiles_per_seq, 0)),
                pl.BlockSpec((PROJ_TM, LANES), lambda i: (i % tiles_per_seq, 0)),
                pl.BlockSpec((HEAD_DIM, PROJ_TM), lambda i: (0, i % tiles_per_seq)),
                pl.BlockSpec((HEAD_DIM, PROJ_TM), lambda i: (0, i % tiles_per_seq)),
            ],
            out_specs=[
                pl.BlockSpec((PROJ_TM, NA_WIDTH), lambda i: (i, 0)),
                pl.BlockSpec((PROJ_TM, NA_WIDTH), lambda i: (i, 0)),
                pl.BlockSpec((PROJ_TM, SWA_KV_WIDTH), lambda i: (i, 0)),
                pl.BlockSpec((1, 2, blk128, NA_WIDTH, LANES), lambda i: (seq_tile(i)[0], 0, seq_tile(i)[1], 0, 0)),
                pl.BlockSpec((1, blk128, SWA_WIDTH, LANES), lambda i: (seq_tile(i)[0], seq_tile(i)[1], 0, 0)),
                pl.BlockSpec((1, blk128, SWA_KV_WIDTH, LANES), lambda i: (seq_tile(i)[0], seq_tile(i)[1], 0, 0)),
            ],
            out_shape=[
                jax.ShapeDtypeStruct((n_tok, NA_WIDTH), bf16),
                jax.ShapeDtypeStruct((n_tok, NA_WIDTH), bf16),
                jax.ShapeDtypeStruct((n_tok, SWA_KV_WIDTH), bf16),
                jax.ShapeDtypeStruct((B, 2, S // LANES, NA_WIDTH, LANES), bf16),
                jax.ShapeDtypeStruct((B, S // LANES, SWA_WIDTH, LANES), bf16),
                jax.ShapeDtypeStruct((B, S // LANES, SWA_KV_WIDTH, LANES), bf16),
            ],
            scratch_shapes=[pltpu.VMEM((D, TOK_WIDTH), bf16), pltpu.VMEM((FEAT_ROWS, D), bf16),
                            pltpu.VMEM((NA_WIDTH, LANES), jnp.float32)],
            compiler_params=pltpu.CompilerParams(
                dimension_semantics=("arbitrary",), vmem_limit_bytes=PROJ_VMEM_LIMIT),
            name="in_proj",
        )(x2, w_in[layer], cos_tok, sin_tok, cos_feat, sin_feat)

        rel_rows = jnp.pad(rel_pos_bias[layer].astype(jnp.float32).reshape(NA_HEADS * N_DR, N_DC),
                           ((0, 0), (0, LANES - N_DC)))
        n_total = B * n_blocks
        gate_blocks = [(_W_ZA + c * GATE_BLOCK) // GATE_BLOCK for c in range(NA_WIDTH // GATE_BLOCK)] + \
                      [(_W_ZB + c * GATE_BLOCK) // GATE_BLOCK for c in range(SWA_WIDTH // GATE_BLOCK)]
        assert _W_ZA % GATE_BLOCK == 0 and _W_ZB % GATE_BLOCK == 0
        assert TICKS % 2 == 0 and n_blocks % TICKS == 0
        blk_of = lambda h: (lambda j: jnp.clip(TICKS * (j - 1) + 1 + h, 0, n_total - 1))
        k_blks = [blk_of(TICKS - 2), blk_of(TICKS - 1)]
        fin = lambda j: jnp.maximum(j - 1, 0)
        q_spec = lambda blk: pl.BlockSpec((TQ, NA_WIDTH), lambda j, *_: (blk(j), 0))
        kta_spec = lambda blk: pl.BlockSpec((1, 2, S // LANES, NA_WIDTH, LANES),
                                            lambda j, *_: (blk(j) // n_blocks, 0, 0, 0, 0))
        qbt_spec = lambda blk: pl.BlockSpec((1, 1, SWA_WIDTH, TQ),
                                            lambda j, *_: (blk(j) // n_blocks, blk(j) % n_blocks, 0, 0))
        kb_spec = lambda blk: pl.BlockSpec((1, S, SWA_KV_WIDTH), lambda j, *_: (blk(j) // n_blocks, 0, 0))
        fin_batch = lambda j: (TICKS * fin(j)) // n_blocks
        grid_spec = pltpu.PrefetchScalarGridSpec(
            num_scalar_prefetch=1,
            grid=(n_total // TICKS + 1,),
            in_specs=[
                *[q_spec(blk_of(h)) for h in range(TICKS)],
                *[kta_spec(blk) for blk in k_blks],
                pl.BlockSpec((1, S, NA_WIDTH), lambda j, *_: (fin_batch(j), 0, 0)),
                *[qbt_spec(blk_of(h)) for h in range(TICKS)],
                *[kb_spec(blk) for blk in k_blks],
                pl.BlockSpec((1, S // LANES, SWA_KV_WIDTH, LANES), lambda j, *_: (fin_batch(j), 0, 0, 0)),
                pl.BlockSpec((TICKS * TQ, D), lambda j, *_: (fin(j), 0)),
                pl.BlockSpec((MIX_WIDTH, D), lambda j, *_: (0, 0), pipeline_mode=pl.Buffered(1)),
                *[pl.BlockSpec((D, GATE_BLOCK), functools.partial(lambda c, j, *_: (0, c), c),
                               pipeline_mode=pl.Buffered(1)) for c in gate_blocks],
                pl.BlockSpec((NA_HEADS * N_DR, LANES), lambda j, *_: (0, 0)),
                pl.BlockSpec((3, SWA_KEYS, TQ), lambda j, *_: (0, 0, 0)),
                pl.BlockSpec((1, D), lambda j, *_: (0, 0)),
                pl.BlockSpec((1, D), lambda j, *_: (0, 0)),
            ],
            out_specs=pl.BlockSpec((TICKS * TQ, D), lambda j, *_: (fin(j), 0)),
            scratch_shapes=[pltpu.VMEM((MIX_WIDTH, D), bf16),
                            pltpu.VMEM((D, MIX_WIDTH), bf16),
                            pltpu.VMEM((N_BIAS_TILES, NA_HEADS, GRID_W, 2 * GRID_W), jnp.float32),
                            pltpu.VMEM((2, Q_ROWS * NA_HEADS // 2, TQ, NA_KEYS), jnp.float32),
                            pltpu.VMEM((2, SWA_KV_HEADS, SWA_KEYS, SWA_GROUP * TQ), jnp.float32),
                            pltpu.VMEM((2, Q_ROWS * NA_HEADS // 2, TQ, LANES), jnp.float32),
                            pltpu.VMEM((2, SWA_KV_HEADS, 8, SWA_GROUP * TQ), jnp.float32)],
        )
        x2 = pl.pallas_call(
            functools.partial(_attn_kernel, rows=rows, n_blocks=n_blocks, n_total=n_total),
            grid_spec=grid_spec,
            out_shape=jax.ShapeDtypeStruct((n_tok, D), jnp.float32),
            compiler_params=pltpu.CompilerParams(
                dimension_semantics=("arbitrary",), vmem_limit_bytes=ATTN_VMEM_LIMIT),
            name="attn_out",
        )(sink_logits[layer].astype(jnp.float32),
          *([qa] * TICKS), *([kta] * len(k_blks)), va.reshape(B, S, NA_WIDTH), *([qbt] * TICKS),
          *([kb.reshape(B, S, SWA_KV_WIDTH)] * len(k_blks)), vbt, x2,
          w_out[layer], *([w_in[layer]] * len(gate_blocks)), rel_rows, band, ln_gain[layer].reshape(1, D), ln_bias[layer].reshape(1, D))
    return x2.reshape(B, S, D)
```

```python
import functools

import numpy as np
import jax
import jax.numpy as jnp
from jax import lax
from jax.experimental import pallas as pl
from jax.experimental.pallas import tpu as pltpu

D_MODEL = 1024
HEAD_DIM = 64
NA_HEADS = 8
NA_WIDTH = NA_HEADS * HEAD_DIM
NA_KH = 8
NA_KW = 16
GRID_W = 64
SWA_Q_HEADS = 8
SWA_KV_HEADS = 2
SWA_GROUP = SWA_Q_HEADS // SWA_KV_HEADS
SWA_WIDTH = SWA_Q_HEADS * HEAD_DIM
SWA_KV_WIDTH = SWA_KV_HEADS * HEAD_DIM
SWA_WINDOW = 128
ROPE_THETA = 10000.0
MIX_WIDTH = NA_WIDTH + SWA_WIDTH
LN_EPS = 1e-5
MASK_VALUE = -1e30
DEPTH = 1
DEEPNORM_ALPHA = (2.0 * DEPTH) ** 0.25
LOG2E = 1.4426950408889634
Q_SCALE = HEAD_DIM ** -0.5 * LOG2E
N_DR = 2 * NA_KH - 1
N_DC = 2 * NA_KW - 1

LANES = 128
BF16_ROWS = 16
PAIR = 2 * HEAD_DIM
assert PAIR == LANES and 2 * GRID_W == LANES and SWA_KV_WIDTH == LANES

PROJ_TM = 1024
TQ = 128
TICKS = 4
PROJ_TICKS = 2
assert TICKS % PROJ_TICKS == 0
V7X_VMEM_BYTES = 64 * 1024 * 1024
PROJ_VMEM_LIMIT = V7X_VMEM_BYTES * 7 // 8
ATTN_VMEM_LIMIT = V7X_VMEM_BYTES * 31 // 32
Q_ROWS = TQ // GRID_W
NA_KEYS = NA_KH * GRID_W
KEY_PAIRS = NA_KEYS // LANES
N_BIAS_TILES = N_DR - 1
SWA_KEYS = 3 * TQ

_QA = (0, NA_WIDTH)
_VA = (_QA[1], _QA[1] + NA_WIDTH)
_KB = (_VA[1], _VA[1] + SWA_KV_WIDTH)
TOK_WIDTH = _KB[1]
GATE_BLOCK = 256
_T_KA = (0, NA_WIDTH)
_T_QB = (_T_KA[1], _T_KA[1] + SWA_WIDTH)
_T_VB = (_T_QB[1], _T_QB[1] + SWA_KV_WIDTH)
FEAT_ROWS = _T_VB[1]
_W_QA = 0
_W_KA = _W_QA + NA_WIDTH
_W_VA = _W_KA + NA_WIDTH
_W_ZA = _W_VA + NA_WIDTH
_W_QB = _W_ZA + NA_WIDTH
_W_KB = _W_QB + SWA_WIDTH
_W_VB = _W_KB + SWA_KV_WIDTH
_W_ZB = _W_VB + SWA_KV_WIDTH
IN_WIDTH = _W_ZB + SWA_WIDTH


def _swa_band_tiles():
    q = np.arange(TQ)[None, :]
    k = np.arange(SWA_KEYS)[:, None] - TQ
    band = np.abs(k - q) <= SWA_WINDOW
    left_ok = np.concatenate([np.zeros((TQ, 1), bool), np.ones((2 * TQ, 1), bool)], axis=0)
    right_ok = np.concatenate([np.ones((2 * TQ, 1), bool), np.zeros((TQ, 1), bool)], axis=0)
    variants = np.stack([band & left_ok, band, band & right_ok])
    assert variants[:, TQ:2 * TQ].all()
    return np.where(variants, 0.0, MASK_VALUE).astype(np.float32)


def _rope_tables(seq):
    inv_freq = ROPE_THETA ** (-jnp.arange(0, HEAD_DIM, 2, dtype=jnp.float32) / HEAD_DIM)
    ang = jnp.arange(seq, dtype=jnp.int32).astype(jnp.float32)[:, None] * inv_freq[None, :]
    cos = jnp.cos(ang)
    sin = jnp.sin(ang)
    cos_head = jnp.concatenate([cos, cos], axis=1)
    sin_head = jnp.concatenate([-sin, sin], axis=1)
    cos_tok = jnp.concatenate([cos_head, cos_head], axis=1)
    sin_tok = jnp.concatenate([sin_head, sin_head], axis=1)
    return cos_tok, sin_tok, cos_head.T * Q_SCALE, sin_head.T * Q_SCALE


def _build_proj_weights(win_ref, wtok_ref, wft_ref):
    bf16 = jnp.bfloat16
    wtok_ref[:, _QA[0]:_QA[1]] = win_ref[:, _W_QA:_W_QA + NA_WIDTH].astype(bf16)
    wtok_ref[:, _VA[0]:_VA[1]] = win_ref[:, _W_VA:_W_VA + NA_WIDTH].astype(bf16)
    wtok_ref[:, _KB[0]:_KB[1]] = win_ref[:, _W_KB:_W_KB + SWA_KV_WIDTH].astype(bf16)
    for dst, src, width in ((_T_KA[0], _W_KA, NA_WIDTH), (_T_QB[0], _W_QB, SWA_WIDTH), (_T_VB[0], _W_VB, SWA_KV_WIDTH)):
        for c in range(width // LANES):
            wft_ref[dst + c * LANES:dst + (c + 1) * LANES, :] = \
                win_ref[:, src + c * LANES:src + (c + 1) * LANES].T.astype(bf16)


def _proj_kernel(x_ref, win_ref, cos_ref, sin_ref, cost_ref, sint_ref,
                 qa_ref, va_ref, kb_ref, kta_ref, qbt_ref, vbt_ref,
                 wtok_ref, wft_ref, carry_ref, *, tiles_per_seq):
    bf16 = jnp.bfloat16
    i = pl.program_id(0)
    half = HEAD_DIM // 2
    n_blk = PROJ_TM // LANES

    @pl.when(i == 0)
    def _():
        _build_proj_weights(win_ref, wtok_ref, wft_ref)

    @pl.when(i % tiles_per_seq == 0)
    def _():
        carry_ref[...] = jnp.zeros_like(carry_ref)

    xb = x_ref[...].astype(bf16)

    def proj(span):
        return jnp.dot(xb, wtok_ref[:, span[0]:span[1]], preferred_element_type=jnp.float32)

    qa_ref[...] = (proj(_QA) * Q_SCALE).astype(bf16)
    va_ref[...] = proj(_VA).astype(bf16)

    kb = proj(_KB)
    lane = lax.broadcasted_iota(jnp.int32, (PROJ_TM, LANES), 1)
    low_half = (lane % HEAD_DIM) < half
    partner = jnp.where(low_half, pltpu.roll(kb, LANES - half, 1), pltpu.roll(kb, half, 1))
    kb_ref[...] = (kb * cos_ref[...] + partner * sin_ref[...]).astype(bf16)

    ft = lax.dot_general(wft_ref[...], xb, (((1,), (1,)), ((), ())), preferred_element_type=jnp.float32)
    cols = [carry_ref[...]] + [ft[_T_KA[0]:_T_KA[1], j * LANES:(j + 1) * LANES] for j in range(n_blk)]
    low_k = lax.broadcasted_iota(jnp.int32, (NA_WIDTH, LANES), 1) < GRID_W
    rolled = [pltpu.roll(c, GRID_W, 1) for c in cols]
    for j in range(n_blk):
        kta_ref[0, 0, j] = cols[j + 1].astype(bf16)
        kta_ref[0, 1, j] = jnp.where(low_k, rolled[j], rolled[j + 1]).astype(bf16)
    carry_ref[...] = cols[n_blk]

    cos_t = cost_ref[...]
    sin_t = sint_ref[...]
    for h in range(SWA_Q_HEADS):
        blk = ft[_T_QB[0] + h * HEAD_DIM:_T_QB[0] + (h + 1) * HEAD_DIM]
        partner_t = jnp.concatenate([blk[half:], blk[:half]], axis=0)
        roped = (blk * cos_t + partner_t * sin_t).astype(bf16)
        for j in range(n_blk):
            qbt_ref[0, j, h * HEAD_DIM:(h + 1) * HEAD_DIM, :] = roped[:, j * LANES:(j + 1) * LANES]

    vt = ft[_T_VB[0]:_T_VB[1]].astype(bf16)
    for j in range(n_blk):
        vbt_ref[0, j] = vt[:, j * LANES:(j + 1) * LANES]


def _build_na_bias_tiles(rel_ref, tbl_ref):
    c = lax.broadcasted_iota(jnp.int32, (GRID_W, LANES), 0)
    lane = lax.broadcasted_iota(jnp.int32, (GRID_W, LANES), 1)
    kc = lane & (GRID_W - 1)
    start = jnp.clip(c - NA_KW // 2, 0, GRID_W - NA_KW)
    in_win = (kc >= start) & (kc < start + NA_KW)
    low = lane < GRID_W
    masked = jnp.full((GRID_W, LANES), MASK_VALUE, jnp.float32)
    for h in range(NA_HEADS):
        def half(dr, upper):
            row = jnp.broadcast_to(rel_ref[pl.ds(h * N_DR + dr, 1), :] * LOG2E, (GRID_W, LANES))
            shift = (LANES - (NA_KW - 1) + (GRID_W if upper else 0)) % LANES
            return pltpu.roll(row, shift, 1, stride=1, stride_axis=0)

        lower = [half(dr, False) for dr in range(N_DR - 1)]
        upper = [half(dr, True) for dr in range(1, N_DR)]
        for d in range(N_BIAS_TILES):
            tbl_ref[d, h] = jnp.where(in_win, jnp.where(low, lower[d], upper[d]), masked)


def _attn_kernel(sink_ref, *refs, rows, n_blocks, n_total):
    refs = list(refs)
    take = lambda n: [refs.pop(0) for _ in range(n)]
    qa_refs = take(TICKS)
    kta_lo, kta_hi, va_ref = take(3)
    qbt_refs = take(TICKS)
    kb_lo, kb_hi, vbt_ref, x_ref, wout_ref = take(5)
    wz_refs = take(MIX_WIDTH // GATE_BLOCK)
    (rel_ref, band_ref, gain_ref, beta_ref, out_ref,
     woutb_ref, wzb_ref, tbl_ref, sna_ref, ssw_ref, mna_ref, msw_ref) = refs
    j = pl.program_id(0)

    @pl.when(j == 0)
    def _():
        woutb_ref[...] = wout_ref[...].astype(jnp.bfloat16)
        for c, wz_ref in enumerate(wz_refs):
            wzb_ref[:, c * GATE_BLOCK:(c + 1) * GATE_BLOCK] = wz_ref[...].astype(jnp.bfloat16)
        _build_na_bias_tiles(rel_ref, tbl_ref)
        for ref in (sna_ref, ssw_ref, mna_ref, msw_ref):
            ref[0] = jnp.zeros(ref.shape[1:], ref.dtype)

    n_units = Q_ROWS * NA_HEADS // 2
    pending = []
    for h in range(TICKS):
        i = TICKS * (j - 1) + 1 + h
        last = h == TICKS - 1
        rd, wr = h % 2, 1 - h % 2
        ena, esw, snk = [None] * n_units, [None] * SWA_KV_HEADS, [None] * SWA_KV_HEADS
        sm_na, sm_sw = _softmax_steps(sink_ref, sna_ref.at[rd], ssw_ref.at[rd], mna_ref.at[rd], msw_ref.at[rd],
                                      ena, esw, snk)
        out_na, out_sw, mixed = _output_steps(jnp.maximum(i - 1, 0) % n_blocks, va_ref, vbt_ref, ena, esw, snk,
                                              rows=rows, n_blocks=n_blocks)
        sc_na, sc_sw = _score_steps(jnp.clip(i, 0, n_total - 1) % n_blocks, qa_refs[h], kta_hi if last else kta_lo,
                                    qbt_refs[h], kb_hi if last else kb_lo, band_ref, tbl_ref,
                                    sna_ref.at[wr], ssw_ref.at[wr], mna_ref.at[wr], msw_ref.at[wr],
                                    rows=rows, n_blocks=n_blocks)
        for step in sm_na + sm_sw + out_na + out_sw:
            step()
        pending.append(mixed())
        if len(pending) == PROJ_TICKS:
            rows_p = pl.ds((h + 1 - PROJ_TICKS) * TQ, PROJ_TICKS * TQ)
            _project_rows(jnp.concatenate(pending, axis=0), wzb_ref, woutb_ref, x_ref.at[rows_p],
                          gain_ref, beta_ref, out_ref.at[rows_p])
            pending = []
        for step in sc_na + sc_sw:
            step()


def _window_start(t, qrow, rows):
    kh = min(NA_KH, rows)
    r = Q_ROWS * t + qrow
    return r, jnp.clip(r - kh // 2, 0, rows - kh)


def _softmax_steps(sink_ref, sna_rd, ssw_rd, mna_rd, msw_rd, ena_wr, esw_wr, snk_wr):
    f32, bf16 = jnp.float32, jnp.bfloat16

    def na(u):
        s_ref = sna_rd.at[u]
        m = jnp.max(mna_rd[u], axis=-1, keepdims=True)
        ena_wr[u] = jnp.exp2(s_ref[...] - m).astype(bf16)

    def sw(g):
        s_ref = ssw_rd.at[g]
        sink = jnp.concatenate(
            [jnp.full((1, TQ), sink_ref[SWA_GROUP * g + j] * LOG2E, f32) for j in range(SWA_GROUP)], axis=1)
        m = jnp.maximum(jnp.max(msw_rd[g], axis=0, keepdims=True), sink)
        esw_wr[g] = jnp.exp2(s_ref[...] - m).astype(bf16)
        snk_wr[g] = jnp.exp2(sink - m)

    return ([functools.partial(na, u) for u in range(Q_ROWS * NA_HEADS // 2)],
            [functools.partial(sw, g) for g in range(SWA_KV_HEADS)])


def _output_steps(t, va_ref, vbt_ref, ena_rd, esw_rd, snk_rd, *, rows, n_blocks):
    f32, bf16 = jnp.float32, jnp.bfloat16
    n_pairs = NA_HEADS // 2
    low = lax.broadcasted_iota(jnp.int32, (GRID_W, LANES), 1) < HEAD_DIM
    jl = jnp.maximum(t - 1, 0)
    jr = jnp.minimum(t + 1, n_blocks - 1)
    na_out = [None] * (Q_ROWS * n_pairs)
    sw_cols = [None] * (SWA_Q_HEADS // 2)

    def na(u):
        qrow, p = divmod(u, n_pairs)
        _, r0 = _window_start(t, qrow, rows)
        key0 = pl.multiple_of(r0 * GRID_W, GRID_W)
        v = va_ref[0, pl.ds(key0, NA_KEYS), p * LANES:(p + 1) * LANES]
        v1 = jnp.concatenate([v, jnp.ones_like(v)], axis=1)
        o = jnp.dot(ena_rd[u], v1, preferred_element_type=f32)
        o = o[:, :LANES] * (1.0 / o[:, LANES:])
        na_out[u] = jnp.where(low, o[:GRID_W], o[GRID_W:])

    def sw(g):
        vt = jnp.concatenate([vbt_ref[0, j, g * HEAD_DIM:(g + 1) * HEAD_DIM, :] for j in (jl, t, jr)], axis=1)
        v1 = jnp.concatenate([vt, jnp.ones((BF16_ROWS, SWA_KEYS), bf16)], axis=0)
        o = jnp.dot(v1, esw_rd[g], preferred_element_type=f32)
        o = o[:HEAD_DIM] * (1.0 / (o[HEAD_DIM:HEAD_DIM + 1] + snk_rd[g]))
        for pp in range(SWA_GROUP // 2):
            both = jnp.concatenate([o[:, (2 * pp) * TQ:(2 * pp + 1) * TQ],
                                    o[:, (2 * pp + 1) * TQ:(2 * pp + 2) * TQ]], axis=0)
            sw_cols[(SWA_GROUP // 2) * g + pp] = both.T

    def mixed():
        mixed_cols = [jnp.concatenate([na_out[qrow * n_pairs + p] for qrow in range(Q_ROWS)], axis=0)
                      for p in range(n_pairs)]
        return jnp.concatenate(mixed_cols + sw_cols, axis=1)

    return ([functools.partial(na, u) for u in range(Q_ROWS * n_pairs)],
            [functools.partial(sw, g) for g in range(SWA_KV_HEADS)], mixed)


def _project_rows(mixed, wzb_ref, woutb_ref, x_ref, gain_ref, beta_ref, out_ref):
    f32, bf16 = jnp.float32, jnp.bfloat16
    x_blk = x_ref[...]
    z = jnp.dot(x_blk.astype(bf16), wzb_ref[...], preferred_element_type=f32)
    gated = (mixed * (z * jax.nn.sigmoid(z))).astype(bf16)
    y = jnp.dot(gated, woutb_ref[...], preferred_element_type=f32)
    r = DEEPNORM_ALPHA * x_blk + y
    mu = jnp.mean(r, axis=-1, keepdims=True)
    d = r - mu
    var = jnp.mean(d * d, axis=-1, keepdims=True)
    out_ref[...] = d * lax.rsqrt(var + LN_EPS) * gain_ref[...] + beta_ref[...]


def _score_steps(t1, qa_ref, kta_ref, qbt_ref, kb_ref, band_ref, tbl_ref, sna_wr, ssw_wr, mna_wr, msw_wr,
                 *, rows, n_blocks):
    f32 = jnp.float32
    n_pairs = NA_HEADS // 2
    low = lax.broadcasted_iota(jnp.int32, (GRID_W, LANES), 1) < HEAD_DIM

    def na(u):
        qrow, p = divmod(u, n_pairs)
        r, r0 = _window_start(t1, qrow, rows)
        par = r0 & 1
        d0 = r0 - r + (NA_KH - 1)
        cols = slice(p * LANES, (p + 1) * LANES)
        qp = qa_ref[qrow * GRID_W:(qrow + 1) * GRID_W, cols]
        zero = jnp.zeros_like(qp)
        q2 = jnp.concatenate([jnp.where(low, qp, zero), jnp.where(low, zero, qp)], axis=0)
        kt = jnp.concatenate([kta_ref[0, par, (r0 + 2 * j + 1) // 2, cols, :] for j in range(KEY_PAIRS)],
                             axis=1)
        bias = jnp.concatenate(
            [jnp.concatenate([tbl_ref[d0 + 2 * j, 2 * p + hp] for j in range(KEY_PAIRS)], axis=1)
             for hp in range(2)], axis=0)
        s = jnp.dot(q2, kt, preferred_element_type=f32) + bias
        sna_wr[u] = s
        mna_wr[u] = functools.reduce(jnp.maximum, [s[:, c * LANES:(c + 1) * LANES] for c in range(KEY_PAIRS)])

    def sw(g):
        jl1 = jnp.maximum(t1 - 1, 0)
        jr1 = jnp.minimum(t1 + 1, n_blocks - 1)
        variant = jnp.where(t1 == 0, 0, jnp.where(t1 == n_blocks - 1, 2, 1))
        k3 = jnp.concatenate([kb_ref[0, pl.ds(pl.multiple_of(j * TQ, TQ), TQ), :] for j in (jl1, t1, jr1)],
                             axis=0)
        qt = jnp.concatenate(
            [qbt_ref[0, 0, (SWA_GROUP * g + j) * HEAD_DIM:(SWA_GROUP * g + j + 1) * HEAD_DIM, :]
             for j in range(SWA_GROUP)], axis=1)
        parts = [jnp.zeros_like(qt)] * SWA_KV_HEADS
        parts[g] = qt
        s = jnp.dot(k3, jnp.concatenate(parts, axis=0), preferred_element_type=f32)
        blocks = []
        for blk in range(3):
            sb = s[blk * TQ:(blk + 1) * TQ]
            if blk != 1:
                band = band_ref[variant, blk * TQ:(blk + 1) * TQ, :]
                sb = sb + jnp.concatenate([band] * SWA_GROUP, axis=1)
            ssw_wr[g, blk * TQ:(blk + 1) * TQ, :] = sb
            blocks.append(sb)
        s_all = jnp.concatenate(blocks, axis=0)
        msw_wr[g] = jnp.max(s_all.reshape(SWA_KEYS // 8, 8, SWA_GROUP * TQ), axis=0)

    return ([functools.partial(na, u) for u in range(Q_ROWS * n_pairs)],
            [functools.partial(sw, g) for g in range(SWA_KV_HEADS)])


def kernel(x, w_in, rel_pos_bias, sink_logits, w_out, ln_gain, ln_bias):
    B, S, D = x.shape
    assert D == D_MODEL and S % PROJ_TM == 0 and S % GRID_W == 0
    rows = S // GRID_W
    assert rows >= NA_KH and rows % Q_ROWS == 0
    n_blocks = S // TQ
    n_tok = B * S
    bf16 = jnp.bfloat16
    x2 = x.reshape(n_tok, D)

    cos_tok, sin_tok, cos_feat, sin_feat = _rope_tables(S)
    band = jnp.asarray(_swa_band_tiles())

    for layer in range(DEPTH):
        tiles_per_seq = S // PROJ_TM
        blk128 = PROJ_TM // LANES
        seq_tile = lambda i: (i // tiles_per_seq, i % tiles_per_seq)
        qa, va, kb, kta, qbt, vbt = pl.pallas_call(
            functools.partial(_proj_kernel, tiles_per_seq=tiles_per_seq),
            grid=(n_tok // PROJ_TM,),
            in_specs=[
                pl.BlockSpec((PROJ_TM, D), lambda i: (i, 0)),
                pl.BlockSpec((D, IN_WIDTH), lambda i: (0, 0), pipeline_mode=pl.Buffered(1)),
                pl.BlockSpec((PROJ_TM, LANES), lambda i: (i % tiles_per_seq, 0)),
                pl.BlockSpec((PROJ_TM, LANES), lambda i: (i % tiles_per_seq, 0)),
                pl.BlockSpec((HEAD_DIM, PROJ_TM), lambda i: (0, i % tiles_per_seq)),
                pl.BlockSpec((HEAD_DIM, PROJ_TM), lambda i: (0, i % tiles_per_seq)),
            ],
            out_specs=[
                pl.BlockSpec((PROJ_TM, NA_WIDTH), lambda i: (i, 0)),
                pl.BlockSpec((PROJ_TM, NA_WIDTH), lambda i: (i, 0)),
                pl.BlockSpec((PROJ_TM, SWA_KV_WIDTH), lambda i: (i, 0)),
                pl.BlockSpec((1, 2, blk128, NA_WIDTH, LANES), lambda i: (seq_tile(i)[0], 0, seq_tile(i)[1], 0, 0)),
                pl.BlockSpec((1, blk128, SWA_WIDTH, LANES), lambda i: (seq_tile(i)[0], seq_tile(i)[1], 0, 0)),
                pl.BlockSpec((1, blk128, SWA_KV_WIDTH, LANES), lambda i: (seq_tile(i)[0], seq_tile(i)[1], 0, 0)),
            ],
            out_shape=[
                jax.ShapeDtypeStruct((n_tok, NA_WIDTH), bf16),
                jax.ShapeDtypeStruct((n_tok, NA_WIDTH), bf16),
                jax.ShapeDtypeStruct((n_tok, SWA_KV_WIDTH), bf16),
                jax.ShapeDtypeStruct((B, 2, S // LANES, NA_WIDTH, LANES), bf16),
                jax.ShapeDtypeStruct((B, S // LANES, SWA_WIDTH, LANES), bf16),
                jax.ShapeDtypeStruct((B, S // LANES, SWA_KV_WIDTH, LANES), bf16),
            ],
            scratch_shapes=[pltpu.VMEM((D, TOK_WIDTH), bf16), pltpu.VMEM((FEAT_ROWS, D), bf16),
                            pltpu.VMEM((NA_WIDTH, LANES), jnp.float32)],
            compiler_params=pltpu.CompilerParams(
                dimension_semantics=("arbitrary",), vmem_limit_bytes=PROJ_VMEM_LIMIT),
            name="in_proj",
        )(x2, w_in[layer], cos_tok, sin_tok, cos_feat, sin_feat)

        rel_rows = jnp.pad(rel_pos_bias[layer].astype(jnp.float32).reshape(NA_HEADS * N_DR, N_DC),
                           ((0, 0), (0, LANES - N_DC)))
        n_total = B * n_blocks
        gate_blocks = [(_W_ZA + c * GATE_BLOCK) // GATE_BLOCK for c in range(NA_WIDTH // GATE_BLOCK)] + \
                      [(_W_ZB + c * GATE_BLOCK) // GATE_BLOCK for c in range(SWA_WIDTH // GATE_BLOCK)]
        assert _W_ZA % GATE_BLOCK == 0 and _W_ZB % GATE_BLOCK == 0
        assert TICKS % 2 == 0 and n_blocks % TICKS == 0
        blk_of = lambda h: (lambda j: jnp.clip(TICKS * (j - 1) + 1 + h, 0, n_total - 1))
        k_blks = [blk_of(TICKS - 2), blk_of(TICKS - 1)]
        fin = lambda j: jnp.maximum(j - 1, 0)
        q_spec = lambda blk: pl.BlockSpec((TQ, NA_WIDTH), lambda j, *_: (blk(j), 0))
        kta_spec = lambda blk: pl.BlockSpec((1, 2, S // LANES, NA_WIDTH, LANES),
                                            lambda j, *_: (blk(j) // n_blocks, 0, 0, 0, 0))
        qbt_spec = lambda blk: pl.BlockSpec((1, 1, SWA_WIDTH, TQ),
                                            lambda j, *_: (blk(j) // n_blocks, blk(j) % n_blocks, 0, 0))
        kb_spec = lambda blk: pl.BlockSpec((1, S, SWA_KV_WIDTH), lambda j, *_: (blk(j) // n_blocks, 0, 0))
        fin_batch = lambda j: (TICKS * fin(j)) // n_blocks
        grid_spec = pltpu.PrefetchScalarGridSpec(
            num_scalar_prefetch=1,
            grid=(n_total // TICKS + 1,),
            in_specs=[
                *[q_spec(blk_of(h)) for h in range(TICKS)],
                *[kta_spec(blk) for blk in k_blks],
                pl.BlockSpec((1, S, NA_WIDTH), lambda j, *_: (fin_batch(j), 0, 0)),
                *[qbt_spec(blk_of(h)) for h in range(TICKS)],
                *[kb_spec(blk) for blk in k_blks],
                pl.BlockSpec((1, S // LANES, SWA_KV_WIDTH, LANES), lambda j, *_: (fin_batch(j), 0, 0, 0)),
                pl.BlockSpec((TICKS * TQ, D), lambda j, *_: (fin(j), 0)),
                pl.BlockSpec((MIX_WIDTH, D), lambda j, *_: (0, 0), pipeline_mode=pl.Buffered(1)),
                *[pl.BlockSpec((D, GATE_BLOCK), functools.partial(lambda c, j, *_: (0, c), c),
                               pipeline_mode=pl.Buffered(1)) for c in gate_blocks],
                pl.BlockSpec((NA_HEADS * N_DR, LANES), lambda j, *_: (0, 0)),
                pl.BlockSpec((3, SWA_KEYS, TQ), lambda j, *_: (0, 0, 0)),
                pl.BlockSpec((1, D), lambda j, *_: (0, 0)),
                pl.BlockSpec((1, D), lambda j, *_: (0, 0)),
            ],
            out_specs=pl.BlockSpec((TICKS * TQ, D), lambda j, *_: (fin(j), 0)),
            scratch_shapes=[pltpu.VMEM((MIX_WIDTH, D), bf16),
                            pltpu.VMEM((D, MIX_WIDTH), bf16),
                            pltpu.VMEM((N_BIAS_TILES, NA_HEADS, GRID_W, 2 * GRID_W), jnp.float32),
                            pltpu.VMEM((2, Q_ROWS * NA_HEADS // 2, TQ, NA_KEYS), jnp.float32),
                            pltpu.VMEM((2, SWA_KV_HEADS, SWA_KEYS, SWA_GROUP * TQ), jnp.float32),
                            pltpu.VMEM((2, Q_ROWS * NA_HEADS // 2, TQ, LANES), jnp.float32),
                            pltpu.VMEM((2, SWA_KV_HEADS, 8, SWA_GROUP * TQ), jnp.float32)],
        )
        x2 = pl.pallas_call(
            functools.partial(_attn_kernel, rows=rows, n_blocks=n_blocks, n_total=n_total),
            grid_spec=grid_spec,
            out_shape=jax.ShapeDtypeStruct((n_tok, D), jnp.float32),
            compiler_params=pltpu.CompilerParams(
                dimension_semantics=("arbitrary",), vmem_limit_bytes=ATTN_VMEM_LIMIT),
            name="attn_out",
        )(sink_logits[layer].astype(jnp.float32),
          *([qa] * TICKS), *([kta] * len(k_blks)), va.reshape(B, S, NA_WIDTH), *([qbt] * TICKS),
          *([kb.reshape(B, S, SWA_KV_WIDTH)] * len(k_blks)), vbt, x2,
          w_out[layer], *([w_in[layer]] * len(gate_blocks)), rel_rows, band, ln_gain[layer].reshape(1, D), ln_bias[layer].reshape(1, D))
    return x2.reshape(B, S, D)
```

```python
import functools

import numpy as np
import jax
import jax.numpy as jnp
from jax import lax
from jax.experimental import pallas as pl
from jax.experimental.pallas import tpu as pltpu

D_MODEL = 1024
HEAD_DIM = 64
NA_HEADS = 8
NA_WIDTH = NA_HEADS * HEAD_DIM
NA_KH = 8
NA_KW = 16
GRID_W = 64
SWA_Q_HEADS = 8
SWA_KV_HEADS = 2
SWA_GROUP = SWA_Q_HEADS // SWA_KV_HEADS
SWA_WIDTH = SWA_Q_HEADS * HEAD_DIM
SWA_KV_WIDTH = SWA_KV_HEADS * HEAD_DIM
SWA_WINDOW = 128
ROPE_THETA = 10000.0
MIX_WIDTH = NA_WIDTH + SWA_WIDTH
LN_EPS = 1e-5
MASK_VALUE = -1e30
DEPTH = 1
DEEPNORM_ALPHA = (2.0 * DEPTH) ** 0.25
LOG2E = 1.4426950408889634
Q_SCALE = HEAD_DIM ** -0.5 * LOG2E
N_DR = 2 * NA_KH - 1
N_DC = 2 * NA_KW - 1

LANES = 128
BF16_ROWS = 16
PAIR = 2 * HEAD_DIM
assert PAIR == LANES and 2 * GRID_W == LANES and SWA_KV_WIDTH == LANES

PROJ_TM = 1024
TQ = 128
TICKS = 4
V7X_VMEM_BYTES = 64 * 1024 * 1024
PROJ_VMEM_LIMIT = V7X_VMEM_BYTES * 7 // 8
ATTN_VMEM_LIMIT = V7X_VMEM_BYTES * 31 // 32
Q_ROWS = TQ // GRID_W
NA_KEYS = NA_KH * GRID_W
KEY_PAIRS = NA_KEYS // LANES
N_BIAS_TILES = N_DR - 1
SWA_KEYS = 3 * TQ

_QA = (0, NA_WIDTH)
_VA = (_QA[1], _QA[1] + NA_WIDTH)
_KB = (_VA[1], _VA[1] + SWA_KV_WIDTH)
TOK_WIDTH = _KB[1]
GATE_BLOCK = 256
_T_KA = (0, NA_WIDTH)
_T_QB = (_T_KA[1], _T_KA[1] + SWA_WIDTH)
_T_VB = (_T_QB[1], _T_QB[1] + SWA_KV_WIDTH)
FEAT_ROWS = _T_VB[1]
_W_QA = 0
_W_KA = _W_QA + NA_WIDTH
_W_VA = _W_KA + NA_WIDTH
_W_ZA = _W_VA + NA_WIDTH
_W_QB = _W_ZA + NA_WIDTH
_W_KB = _W_QB + SWA_WIDTH
_W_VB = _W_KB + SWA_KV_WIDTH
_W_ZB = _W_VB + SWA_KV_WIDTH
IN_WIDTH = _W_ZB + SWA_WIDTH


def _swa_band_tiles():
    q = np.arange(TQ)[None, :]
    k = np.arange(SWA_KEYS)[:, None] - TQ
    band = np.abs(k - q) <= SWA_WINDOW
    left_ok = np.concatenate([np.zeros((TQ, 1), bool), np.ones((2 * TQ, 1), bool)], axis=0)
    right_ok = np.concatenate([np.ones((2 * TQ, 1), bool), np.zeros((TQ, 1), bool)], axis=0)
    variants = np.stack([band & left_ok, band, band & right_ok])
    assert variants[:, TQ:2 * TQ].all()
    return np.where(variants, 0.0, MASK_VALUE).astype(np.float32)


def _rope_tables(seq):
    inv_freq = ROPE_THETA ** (-jnp.arange(0, HEAD_DIM, 2, dtype=jnp.float32) / HEAD_DIM)
    ang = jnp.arange(seq, dtype=jnp.int32).astype(jnp.float32)[:, None] * inv_freq[None, :]
    cos = jnp.cos(ang)
    sin = jnp.sin(ang)
    cos_head = jnp.concatenate([cos, cos], axis=1)
    sin_head = jnp.concatenate([-sin, sin], axis=1)
    cos_tok = jnp.concatenate([cos_head, cos_head], axis=1)
    sin_tok = jnp.concatenate([sin_head, sin_head], axis=1)
    return cos_tok, sin_tok, cos_head.T * Q_SCALE, sin_head.T * Q_SCALE


def _build_proj_weights(win_ref, wtok_ref, wft_ref):
    bf16 = jnp.bfloat16
    wtok_ref[:, _QA[0]:_QA[1]] = win_ref[:, _W_QA:_W_QA + NA_WIDTH].astype(bf16)
    wtok_ref[:, _VA[0]:_VA[1]] = win_ref[:, _W_VA:_W_VA + NA_WIDTH].astype(bf16)
    wtok_ref[:, _KB[0]:_KB[1]] = win_ref[:, _W_KB:_W_KB + SWA_KV_WIDTH].astype(bf16)
    for dst, src, width in ((_T_KA[0], _W_KA, NA_WIDTH), (_T_QB[0], _W_QB, SWA_WIDTH), (_T_VB[0], _W_VB, SWA_KV_WIDTH)):
        for c in range(width // LANES):
            wft_ref[dst + c * LANES:dst + (c + 1) * LANES, :] = \
                win_ref[:, src + c * LANES:src + (c + 1) * LANES].T.astype(bf16)


def _proj_kernel(x_ref, win_ref, cos_ref, sin_ref, cost_ref, sint_ref,
                 qa_ref, va_ref, kb_ref, kta_ref, qbt_ref, vbt_ref,
                 wtok_ref, wft_ref, carry_ref, *, tiles_per_seq):
    bf16 = jnp.bfloat16
    i = pl.program_id(0)
    half = HEAD_DIM // 2
    n_blk = PROJ_TM // LANES

    @pl.when(i == 0)
    def _():
        _build_proj_weights(win_ref, wtok_ref, wft_ref)

    @pl.when(i % tiles_per_seq == 0)
    def _():
        carry_ref[...] = jnp.zeros_like(carry_ref)

    xb = x_ref[...].astype(bf16)

    def proj(span):
        return jnp.dot(xb, wtok_ref[:, span[0]:span[1]], preferred_element_type=jnp.float32)

    qa_ref[...] = (proj(_QA) * Q_SCALE).astype(bf16)
    va_ref[...] = proj(_VA).astype(bf16)

    kb = proj(_KB)
    lane = lax.broadcasted_iota(jnp.int32, (PROJ_TM, LANES), 1)
    low_half = (lane % HEAD_DIM) < half
    partner = jnp.where(low_half, pltpu.roll(kb, LANES - half, 1), pltpu.roll(kb, half, 1))
    kb_ref[...] = (kb * cos_ref[...] + partner * sin_ref[...]).astype(bf16)

    ft = lax.dot_general(wft_ref[...], xb, (((1,), (1,)), ((), ())), preferred_element_type=jnp.float32)
    cols = [carry_ref[...]] + [ft[_T_KA[0]:_T_KA[1], j * LANES:(j + 1) * LANES] for j in range(n_blk)]
    low_k = lax.broadcasted_iota(jnp.int32, (NA_WIDTH, LANES), 1) < GRID_W
    rolled = [pltpu.roll(c, GRID_W, 1) for c in cols]
    for j in range(n_blk):
        kta_ref[0, 0, j] = cols[j + 1].astype(bf16)
        kta_ref[0, 1, j] = jnp.where(low_k, rolled[j], rolled[j + 1]).astype(bf16)
    carry_ref[...] = cols[n_blk]

    cos_t = cost_ref[...]
    sin_t = sint_ref[...]
    for h in range(SWA_Q_HEADS):
        blk = ft[_T_QB[0] + h * HEAD_DIM:_T_QB[0] + (h + 1) * HEAD_DIM]
        partner_t = jnp.concatenate([blk[half:], blk[:half]], axis=0)
        roped = (blk * cos_t + partner_t * sin_t).astype(bf16)
        for j in range(n_blk):
            qbt_ref[0, j, h * HEAD_DIM:(h + 1) * HEAD_DIM, :] = roped[:, j * LANES:(j + 1) * LANES]

    vt = ft[_T_VB[0]:_T_VB[1]].astype(bf16)
    for j in range(n_blk):
        vbt_ref[0, j] = vt[:, j * LANES:(j + 1) * LANES]


def _build_na_bias_tiles(rel_ref, tbl_ref):
    c = lax.broadcasted_iota(jnp.int32, (GRID_W, LANES), 0)
    lane = lax.broadcasted_iota(jnp.int32, (GRID_W, LANES), 1)
    kc = lane & (GRID_W - 1)
    start = jnp.clip(c - NA_KW // 2, 0, GRID_W - NA_KW)
    in_win = (kc >= start) & (kc < start + NA_KW)
    low = lane < GRID_W
    masked = jnp.full((GRID_W, LANES), MASK_VALUE, jnp.float32)
    for h in range(NA_HEADS):
        def half(dr, upper):
            row = jnp.broadcast_to(rel_ref[pl.ds(h * N_DR + dr, 1), :] * LOG2E, (GRID_W, LANES))
            shift = (LANES - (NA_KW - 1) + (GRID_W if upper else 0)) % LANES
            return pltpu.roll(row, shift, 1, stride=1, stride_axis=0)

        lower = [half(dr, False) for dr in range(N_DR - 1)]
        upper = [half(dr, True) for dr in range(1, N_DR)]
        for d in range(N_BIAS_TILES):
            tbl_ref[d, h] = jnp.where(in_win, jnp.where(low, lower[d], upper[d]), masked)


def _attn_kernel(sink_ref, *refs, rows, n_blocks, n_total):
    refs = list(refs)
    take = lambda n: [refs.pop(0) for _ in range(n)]
    qa_refs = take(TICKS)
    kta_lo, kta_hi, va_ref = take(3)
    qbt_refs = take(TICKS)
    kb_lo, kb_hi, vbt_ref, x_ref, wout_ref = take(5)
    wz_refs = take(MIX_WIDTH // GATE_BLOCK)
    (rel_ref, band_ref, gain_ref, beta_ref, out_ref,
     woutb_ref, wzb_ref, tbl_ref, sna_ref, ssw_ref, mna_ref, msw_ref) = refs
    j = pl.program_id(0)

    @pl.when(j == 0)
    def _():
        woutb_ref[...] = wout_ref[...].astype(jnp.bfloat16)
        for c, wz_ref in enumerate(wz_refs):
            wzb_ref[:, c * GATE_BLOCK:(c + 1) * GATE_BLOCK] = wz_ref[...].astype(jnp.bfloat16)
        _build_na_bias_tiles(rel_ref, tbl_ref)
        for ref in (sna_ref, ssw_ref, mna_ref, msw_ref):
            ref[0] = jnp.zeros(ref.shape[1:], ref.dtype)

    n_units = Q_ROWS * NA_HEADS // 2
    for h in range(TICKS):
        i = TICKS * (j - 1) + 1 + h
        last = h == TICKS - 1
        rd, wr = h % 2, 1 - h % 2
        ena, esw, snk = [None] * n_units, [None] * SWA_KV_HEADS, [None] * SWA_KV_HEADS
        sm_na, sm_sw = _softmax_steps(sink_ref, sna_ref.at[rd], ssw_ref.at[rd], mna_ref.at[rd], msw_ref.at[rd],
                                      ena, esw, snk)
        out_na, out_sw, mixed = _output_steps(jnp.maximum(i - 1, 0) % n_blocks, va_ref, vbt_ref, ena, esw, snk,
                                              rows=rows, n_blocks=n_blocks)
        sc_na, sc_sw = _score_steps(jnp.clip(i, 0, n_total - 1) % n_blocks, qa_refs[h], kta_hi if last else kta_lo,
                                    qbt_refs[h], kb_hi if last else kb_lo, band_ref, tbl_ref,
                                    sna_ref.at[wr], ssw_ref.at[wr], mna_ref.at[wr], msw_ref.at[wr],
                                    rows=rows, n_blocks=n_blocks)
        for step in sm_na + sm_sw + out_na + out_sw:
            step()
        rows_h = pl.ds(h * TQ, TQ)
        mixed_h = mixed()
        for step in sc_na + sc_sw:
            step()
        _project_rows(mixed_h, wzb_ref, woutb_ref, x_ref.at[rows_h], gain_ref, beta_ref, out_ref.at[rows_h])


def _window_start(t, qrow, rows):
    kh = min(NA_KH, rows)
    r = Q_ROWS * t + qrow
    return r, jnp.clip(r - kh // 2, 0, rows - kh)


def _softmax_steps(sink_ref, sna_rd, ssw_rd, mna_rd, msw_rd, ena_wr, esw_wr, snk_wr):
    f32, bf16 = jnp.float32, jnp.bfloat16

    def na(u):
        s_ref = sna_rd.at[u]
        m = jnp.max(mna_rd[u], axis=-1, keepdims=True)
        ena_wr[u] = jnp.exp2(s_ref[...] - m).astype(bf16)

    def sw(g):
        s_ref = ssw_rd.at[g]
        sink = jnp.concatenate(
            [jnp.full((1, TQ), sink_ref[SWA_GROUP * g + j] * LOG2E, f32) for j in range(SWA_GROUP)], axis=1)
        m = jnp.maximum(jnp.max(msw_rd[g], axis=0, keepdims=True), sink)
        esw_wr[g] = jnp.exp2(s_ref[...] - m).astype(bf16)
        snk_wr[g] = jnp.exp2(sink - m)

    return ([functools.partial(na, u) for u in range(Q_ROWS * NA_HEADS // 2)],
            [functools.partial(sw, g) for g in range(SWA_KV_HEADS)])


def _output_steps(t, va_ref, vbt_ref, ena_rd, esw_rd, snk_rd, *, rows, n_blocks):
    f32, bf16 = jnp.float32, jnp.bfloat16
    n_pairs = NA_HEADS // 2
    low = lax.broadcasted_iota(jnp.int32, (GRID_W, LANES), 1) < HEAD_DIM
    jl = jnp.maximum(t - 1, 0)
    jr = jnp.minimum(t + 1, n_blocks - 1)
    na_out = [None] * (Q_ROWS * n_pairs)
    sw_cols = [None] * (SWA_Q_HEADS // 2)

    def na(u):
        qrow, p = divmod(u, n_pairs)
        _, r0 = _window_start(t, qrow, rows)
        key0 = pl.multiple_of(r0 * GRID_W, GRID_W)
        v = va_ref[0, pl.ds(key0, NA_KEYS), p * LANES:(p + 1) * LANES]
        v1 = jnp.concatenate([v, jnp.ones_like(v)], axis=1)
        o = jnp.dot(ena_rd[u], v1, preferred_element_type=f32)
        o = o[:, :LANES] * (1.0 / o[:, LANES:])
        na_out[u] = jnp.where(low, o[:GRID_W], o[GRID_W:])

    def sw(g):
        vt = jnp.concatenate([vbt_ref[0, j, g * HEAD_DIM:(g + 1) * HEAD_DIM, :] for j in (jl, t, jr)], axis=1)
        v1 = jnp.concatenate([vt, jnp.ones((BF16_ROWS, SWA_KEYS), bf16)], axis=0)
        o = jnp.dot(v1, esw_rd[g], preferred_element_type=f32)
        o = o[:HEAD_DIM] * (1.0 / (o[HEAD_DIM:HEAD_DIM + 1] + snk_rd[g]))
        for pp in range(SWA_GROUP // 2):
            both = jnp.concatenate([o[:, (2 * pp) * TQ:(2 * pp + 1) * TQ],
                                    o[:, (2 * pp + 1) * TQ:(2 * pp + 2) * TQ]], axis=0)
            sw_cols[(SWA_GROUP // 2) * g + pp] = both.T

    def mixed():
        mixed_cols = [jnp.concatenate([na_out[qrow * n_pairs + p] for qrow in range(Q_ROWS)], axis=0)
                      for p in range(n_pairs)]
        return jnp.concatenate(mixed_cols + sw_cols, axis=1)

    return ([functools.partial(na, u) for u in range(Q_ROWS * n_pairs)],
            [functools.partial(sw, g) for g in range(SWA_KV_HEADS)], mixed)


def _project_rows(mixed, wzb_ref, woutb_ref, x_ref, gain_ref, beta_ref, out_ref):
    f32, bf16 = jnp.float32, jnp.bfloat16
    x_blk = x_ref[...]
    z = jnp.dot(x_blk.astype(bf16), wzb_ref[...], preferred_element_type=f32)
    gated = (mixed * (z * jax.nn.sigmoid(z))).astype(bf16)
    y = jnp.dot(gated, woutb_ref[...], preferred_element_type=f32)
    r = DEEPNORM_ALPHA * x_blk + y
    mu = jnp.mean(r, axis=-1, keepdims=True)
    d = r - mu
    var = jnp.mean(d * d, axis=-1, keepdims=True)
    out_ref[...] = d * lax.rsqrt(var + LN_EPS) * gain_ref[...] + beta_ref[...]


def _score_steps(t1, qa_ref, kta_ref, qbt_ref, kb_ref, band_ref, tbl_ref, sna_wr, ssw_wr, mna_wr, msw_wr,
                 *, rows, n_blocks):
    f32 = jnp.float32
    n_pairs = NA_HEADS // 2
    low = lax.broadcasted_iota(jnp.int32, (GRID_W, LANES), 1) < HEAD_DIM

    def na(u):
        qrow, p = divmod(u, n_pairs)
        r, r0 = _window_start(t1, qrow, rows)
        par = r0 & 1
        d0 = r0 - r + (NA_KH - 1)
        cols = slice(p * LANES, (p + 1) * LANES)
        qp = qa_ref[qrow * GRID_W:(qrow + 1) * GRID_W, cols]
        zero = jnp.zeros_like(qp)
        q2 = jnp.concatenate([jnp.where(low, qp, zero), jnp.where(low, zero, qp)], axis=0)
        kt = jnp.concatenate([kta_ref[0, par, (r0 + 2 * j + 1) // 2, cols, :] for j in range(KEY_PAIRS)],
                             axis=1)
        bias = jnp.concatenate(
            [jnp.concatenate([tbl_ref[d0 + 2 * j, 2 * p + hp] for j in range(KEY_PAIRS)], axis=1)
             for hp in range(2)], axis=0)
        s = jnp.dot(q2, kt, preferred_element_type=f32) + bias
        sna_wr[u] = s
        mna_wr[u] = functools.reduce(jnp.maximum, [s[:, c * LANES:(c + 1) * LANES] for c in range(KEY_PAIRS)])

    def sw(g):
        jl1 = jnp.maximum(t1 - 1, 0)
        jr1 = jnp.minimum(t1 + 1, n_blocks - 1)
        variant = jnp.where(t1 == 0, 0, jnp.where(t1 == n_blocks - 1, 2, 1))
        k3 = jnp.concatenate([kb_ref[0, pl.ds(pl.multiple_of(j * TQ, TQ), TQ), :] for j in (jl1, t1, jr1)],
                             axis=0)
        qt = jnp.concatenate(
            [qbt_ref[0, 0, (SWA_GROUP * g + j) * HEAD_DIM:(SWA_GROUP * g + j + 1) * HEAD_DIM, :]
             for j in range(SWA_GROUP)], axis=1)
        parts = [jnp.zeros_like(qt)] * SWA_KV_HEADS
        parts[g] = qt
        s = jnp.dot(k3, jnp.concatenate(parts, axis=0), preferred_element_type=f32)
        blocks = []
        for blk in range(3):
            sb = s[blk * TQ:(blk + 1) * TQ]
            if blk != 1:
                band = band_ref[variant, blk * TQ:(blk + 1) * TQ, :]
                sb = sb + jnp.concatenate([band] * SWA_GROUP, axis=1)
            ssw_wr[g, blk * TQ:(blk + 1) * TQ, :] = sb
            blocks.append(sb)
        s_all = jnp.concatenate(blocks, axis=0)
        msw_wr[g] = jnp.max(s_all.reshape(SWA_KEYS // 8, 8, SWA_GROUP * TQ), axis=0)

    return ([functools.partial(na, u) for u in range(Q_ROWS * n_pairs)],
            [functools.partial(sw, g) for g in range(SWA_KV_HEADS)])


def kernel(x, w_in, rel_pos_bias, sink_logits, w_out, ln_gain, ln_bias):
    B, S, D = x.shape
    assert D == D_MODEL and S % PROJ_TM == 0 and S % GRID_W == 0
    rows = S // GRID_W
    assert rows >= NA_KH and rows % Q_ROWS == 0
    n_blocks = S // TQ
    n_tok = B * S
    bf16 = jnp.bfloat16
    x2 = x.reshape(n_tok, D)

    cos_tok, sin_tok, cos_feat, sin_feat = _rope_tables(S)
    band = jnp.asarray(_swa_band_tiles())

    for layer in range(DEPTH):
        tiles_per_seq = S // PROJ_TM
        blk128 = PROJ_TM // LANES
        seq_tile = lambda i: (i // tiles_per_seq, i % tiles_per_seq)
        qa, va, kb, kta, qbt, vbt = pl.pallas_call(
            functools.partial(_proj_kernel, tiles_per_seq=tiles_per_seq),
            grid=(n_tok // PROJ_TM,),
            in_specs=[
                pl.BlockSpec((PROJ_TM, D), lambda i: (i, 0)),
                pl.BlockSpec((D, IN_WIDTH), lambda i: (0, 0), pipeline_mode=pl.Buffered(1)),
                pl.BlockSpec((PROJ_TM, LANES), lambda i: (i % tiles_per_seq, 0)),
                pl.BlockSpec((PROJ_TM, LANES), lambda i: (i % tiles_per_seq, 0)),
                pl.BlockSpec((HEAD_DIM, PROJ_TM), lambda i: (0, i % tiles_per_seq)),
                pl.BlockSpec((HEAD_DIM, PROJ_TM), lambda i: (0, i % tiles_per_seq)),
            ],
            out_specs=[
                pl.BlockSpec((PROJ_TM, NA_WIDTH), lambda i: (i, 0)),
                pl.BlockSpec((PROJ_TM, NA_WIDTH), lambda i: (i, 0)),
                pl.BlockSpec((PROJ_TM, SWA_KV_WIDTH), lambda i: (i, 0)),
                pl.BlockSpec((1, 2, blk128, NA_WIDTH, LANES), lambda i: (seq_tile(i)[0], 0, seq_tile(i)[1], 0, 0)),
                pl.BlockSpec((1, blk128, SWA_WIDTH, LANES), lambda i: (seq_tile(i)[0], seq_tile(i)[1], 0, 0)),
                pl.BlockSpec((1, blk128, SWA_KV_WIDTH, LANES), lambda i: (seq_tile(i)[0], seq_tile(i)[1], 0, 0)),
            ],
            out_shape=[
                jax.ShapeDtypeStruct((n_tok, NA_WIDTH), bf16),
                jax.ShapeDtypeStruct((n_tok, NA_WIDTH), bf16),
                jax.ShapeDtypeStruct((n_tok, SWA_KV_WIDTH), bf16),
                jax.ShapeDtypeStruct((B, 2, S // LANES, NA_WIDTH, LANES), bf16),
                jax.ShapeDtypeStruct((B, S // LANES, SWA_WIDTH, LANES), bf16),
                jax.ShapeDtypeStruct((B, S // LANES, SWA_KV_WIDTH, LANES), bf16),
            ],
            scratch_shapes=[pltpu.VMEM((D, TOK_WIDTH), bf16), pltpu.VMEM((FEAT_ROWS, D), bf16),
                            pltpu.VMEM((NA_WIDTH, LANES), jnp.float32)],
            compiler_params=pltpu.CompilerParams(
                dimension_semantics=("arbitrary",), vmem_limit_bytes=PROJ_VMEM_LIMIT),
            name="in_proj",
        )(x2, w_in[layer], cos_tok, sin_tok, cos_feat, sin_feat)

        rel_rows = jnp.pad(rel_pos_bias[layer].astype(jnp.float32).reshape(NA_HEADS * N_DR, N_DC),
                           ((0, 0), (0, LANES - N_DC)))
        n_total = B * n_blocks
        gate_blocks = [(_W_ZA + c * GATE_BLOCK) // GATE_BLOCK for c in range(NA_WIDTH // GATE_BLOCK)] + \
                      [(_W_ZB + c * GATE_BLOCK) // GATE_BLOCK for c in range(SWA_WIDTH // GATE_BLOCK)]
        assert _W_ZA % GATE_BLOCK == 0 and _W_ZB % GATE_BLOCK == 0
        assert TICKS % 2 == 0 and n_blocks % TICKS == 0
        blk_of = lambda h: (lambda j: jnp.clip(TICKS * (j - 1) + 1 + h, 0, n_total - 1))
        k_blks = [blk_of(TICKS - 2), blk_of(TICKS - 1)]
        fin = lambda j: jnp.maximum(j - 1, 0)
        q_spec = lambda blk: pl.BlockSpec((TQ, NA_WIDTH), lambda j, *_: (blk(j), 0))
        kta_spec = lambda blk: pl.BlockSpec((1, 2, S // LANES, NA_WIDTH, LANES),
                                            lambda j, *_: (blk(j) // n_blocks, 0, 0, 0, 0))
        qbt_spec = lambda blk: pl.BlockSpec((1, 1, SWA_WIDTH, TQ),
                                            lambda j, *_: (blk(j) // n_blocks, blk(j) % n_blocks, 0, 0))
        kb_spec = lambda blk: pl.BlockSpec((1, S, SWA_KV_WIDTH), lambda j, *_: (blk(j) // n_blocks, 0, 0))
        fin_batch = lambda j: (TICKS * fin(j)) // n_blocks
        grid_spec = pltpu.PrefetchScalarGridSpec(
            num_scalar_prefetch=1,
            grid=(n_total // TICKS + 1,),
            in_specs=[
                *[q_spec(blk_of(h)) for h in range(TICKS)],
                *[kta_spec(blk) for blk in k_blks],
                pl.BlockSpec((1, S, NA_WIDTH), lambda j, *_: (fin_batch(j), 0, 0)),
                *[qbt_spec(blk_of(h)) for h in range(TICKS)],
                *[kb_spec(blk) for blk in k_blks],
                pl.BlockSpec((1, S // LANES, SWA_KV_WIDTH, LANES), lambda j, *_: (fin_batch(j), 0, 0, 0)),
                pl.BlockSpec((TICKS * TQ, D), lambda j, *_: (fin(j), 0)),
                pl.BlockSpec((MIX_WIDTH, D), lambda j, *_: (0, 0), pipeline_mode=pl.Buffered(1)),
                *[pl.BlockSpec((D, GATE_BLOCK), functools.partial(lambda c, j, *_: (0, c), c),
                               pipeline_mode=pl.Buffered(1)) for c in gate_blocks],
                pl.BlockSpec((NA_HEADS * N_DR, LANES), lambda j, *_: (0, 0)),
                pl.BlockSpec((3, SWA_KEYS, TQ), lambda j, *_: (0, 0, 0)),
                pl.BlockSpec((1, D), lambda j, *_: (0, 0)),
                pl.BlockSpec((1, D), lambda j, *_: (0, 0)),
            ],
            out_specs=pl.BlockSpec((TICKS * TQ, D), lambda j, *_: (fin(j), 0)),
            scratch_shapes=[pltpu.VMEM((MIX_WIDTH, D), bf16),
                            pltpu.VMEM((D, MIX_WIDTH), bf16),
                            pltpu.VMEM((N_BIAS_TILES, NA_HEADS, GRID_W, 2 * GRID_W), jnp.float32),
                            pltpu.VMEM((2, Q_ROWS * NA_HEADS // 2, TQ, NA_KEYS), jnp.float32),
                            pltpu.VMEM((2, SWA_KV_HEADS, SWA_KEYS, SWA_GROUP * TQ), jnp.float32),
                            pltpu.VMEM((2, Q_ROWS * NA_HEADS // 2, TQ, LANES), jnp.float32),
                            pltpu.VMEM((2, SWA_KV_HEADS, 8, SWA_GROUP * TQ), jnp.float32)],
        )
        x2 = pl.pallas_call(
            functools.partial(_attn_kernel, rows=rows, n_blocks=n_blocks, n_total=n_total),
            grid_spec=grid_spec,
            out_shape=jax.ShapeDtypeStruct((n_tok, D), jnp.float32),
            compiler_params=pltpu.CompilerParams(
                dimension_semantics=("arbitrary",), vmem_limit_bytes=ATTN_VMEM_LIMIT),
            name="attn_out",
        )(sink_logits[layer].astype(jnp.float32),
          *([qa] * TICKS), *([kta] * len(k_blks)), va.reshape(B, S, NA_WIDTH), *([qbt] * TICKS),
          *([kb.reshape(B, S, SWA_KV_WIDTH)] * len(k_blks)), vbt, x2,
          w_out[layer], *([w_in[layer]] * len(gate_blocks)), rel_rows, band, ln_gain[layer].reshape(1, D), ln_bias[layer].reshape(1, D))
    return x2.reshape(B, S, D)
```
